```python
import math
import jax
import jax.numpy as jnp
from jax import lax
import numpy as np

D_MODEL = 4096
BATCH = 4
SEQ = 2048
DEPTH = 1
DEC_BATCH = 128
DEC_SEQ = 1
PAST_LEN = 2048
PAGE_SIZE = 128

HEAD_DIM = 128
HEADS_PER_GROUP = 8
ATTN_GROUPS = ((128, 1), (512, 4), (2048, 16))
N_GROUPS = len(ATTN_GROUPS)
N_ATTN_HEADS = N_GROUPS * HEADS_PER_GROUP
ATTN_WIDTH = N_ATTN_HEADS * HEAD_DIM
ATTN_OUT_WIDTH = HEADS_PER_GROUP * HEAD_DIM
Q_BLOCK = 128
N_BUCKETS = 32
RP_MAX_DIST = 2048
LRU_WIDTH = D_MODEL // 2
LRU_BLOCKS = 16
LRU_BLOCK = LRU_WIDTH // LRU_BLOCKS
CONV_WIDTH = 4
LRU_C = 8.0
PEER_HEADS = 8
PEER_KEYS = 128
PEER_EXPERTS = PEER_KEYS * PEER_KEYS
PEER_DKEY = 256
PEER_TOPK = 16
PEER_BLOCK = 64
EPS = 1e-6

Q_OFF = 0
K_OFF = ATTN_WIDTH
V_OFF = 2 * ATTN_WIDTH
XB_OFF = 3 * ATTN_WIDTH
GB_OFF = XB_OFF + LRU_WIDTH
GA_OFF = GB_OFF + LRU_WIDTH
GR_OFF = GA_OFF + D_MODEL
IN_COLS = GR_OFF + D_MODEL

kernel_name = "hybrid_dilated_rglru_peer_step"


def rmsnorm(x, g):
    xf = x.astype(jnp.float32)
    y = xf * lax.rsqrt(jnp.mean(xf * xf, axis=-1, keepdims=True) + EPS)
    return (y * g.astype(jnp.float32)).astype(x.dtype)


def t5_bucket(dist):
    max_exact = N_BUCKETS // 2
    d = jnp.maximum(dist, max_exact).astype(jnp.float32)
    large = max_exact + (jnp.log(d / max_exact) / math.log(RP_MAX_DIST / max_exact)
                         * (N_BUCKETS - max_exact)).astype(jnp.int32)
    large = jnp.minimum(large, N_BUCKETS - 1)
    return jnp.where(dist < max_exact, dist, large)


def group_bias(table, window, dilation):
    dist = dilation * jnp.arange(window // dilation + 1, dtype=jnp.int32)
    return table[t5_bucket(dist)]


def dilated_attention(q, k_src, v_src, idx, valid, bias):
    kg = k_src[:, idx]
    vg = v_src[:, idx]
    logits = jnp.einsum('bqhd,bqkhd->bhqk', q.astype(jnp.float32), kg.astype(jnp.float32)) * (HEAD_DIM ** -0.5)
    logits = logits + bias.astype(jnp.float32).T[None, :, None, :]
    logits = jnp.where(valid[None, None], logits, -jnp.inf)
    lse = jax.nn.logsumexp(logits, axis=-1)
    p = jnp.exp(logits - lse[..., None])
    out = jnp.einsum('bhqk,bqkhd->bqhd', p, vg.astype(jnp.float32))
    return out, jnp.transpose(lse, (0, 2, 1))


def prompt_group_attention(q, k, v, window, dilation, bias):
    B, S, H, hd = q.shape
    nk = window // dilation + 1
    rel = jnp.arange(S)[:, None] - dilation * jnp.arange(nk)[None, :]
    valid = rel >= 0
    idx = rel + window
    pad = ((0, 0), (window, 0), (0, 0), (0, 0))
    kp = jnp.pad(k, pad)
    vp = jnp.pad(v, pad)
    nb = S // Q_BLOCK
    qb = q.reshape(B, nb, Q_BLOCK, H, hd).transpose(1, 0, 2, 3, 4)
    idxb = idx.reshape(nb, Q_BLOCK, nk)
    validb = valid.reshape(nb, Q_BLOCK, nk)
    out, lse = lax.map(lambda a: dilated_attention(a[0], kp, vp, a[1], a[2], bias), (qb, idxb, validb))
    out = out.transpose(1, 0, 2, 3, 4).reshape(B, S, H, hd)
    lse = lse.transpose(1, 0, 2, 3).reshape(B, S, H)
    return out, lse


def sample_group_attention(q, k_new, v_new, cache_k, cache_v, window, dilation, bias):
    wb = cache_k.shape[1]
    T = q.shape[1]
    nk = window // dilation + 1
    k_src = jnp.concatenate([cache_k.astype(k_new.dtype), k_new], axis=1)
    v_src = jnp.concatenate([cache_v.astype(v_new.dtype), v_new], axis=1)
    rel = wb + jnp.arange(T)[:, None] - dilation * jnp.arange(nk)[None, :]
    valid = rel >= 0
    idx = jnp.maximum(rel, 0)
    return dilated_attention(q, k_src, v_src, idx, valid, bias)


def _lin_combine(c1, c2):
    a1, b1 = c1
    a2, b2 = c2
    return a1 * a2, a2 * b1 + b2


def recurrent_branch(xb, gb, conv_buf, h0, first_pos, conv_w, conv_b, w_rg, b_rg, w_ig, b_ig, lru_L):
    B, T, W = xb.shape
    xc = jnp.concatenate([conv_buf.astype(xb.dtype), xb], axis=1)
    conv = conv_b.astype(jnp.float32)
    for i in range(CONV_WIDTH):
        conv = conv + xc[:, i:i + T].astype(jnp.float32) * conv_w[i].astype(jnp.float32)
    new_buf = xc[:, T:]
    xblk = conv.reshape(B, T, LRU_BLOCKS, LRU_BLOCK)
    r = jax.nn.sigmoid(jnp.einsum('btnc,ncd->btnd', xblk, w_rg.astype(jnp.float32)).reshape(B, T, W)
                       + b_rg.astype(jnp.float32))
    ig = jax.nn.sigmoid(jnp.einsum('btnc,ncd->btnd', xblk, w_ig.astype(jnp.float32)).reshape(B, T, W)
                        + b_ig.astype(jnp.float32))
    log_a = -LRU_C * r * jax.nn.softplus(-lru_L.astype(jnp.float32))
    reset = ((first_pos + jnp.arange(T)) == 0)[None, :, None]
    a = jnp.where(reset, 0.0, jnp.exp(log_a))
    mult = jnp.where(reset, 1.0, jnp.sqrt(-jnp.expm1(2.0 * log_a)))
    b = mult * ig * conv
    b = b.at[:, 0].add(a[:, 0] * h0.astype(jnp.float32))
    _, h = lax.associative_scan(_lin_combine, (a, b), axis=1)
    y = h * jax.nn.gelu(gb.astype(jnp.float32))
    return y.astype(xb.dtype), new_buf, h[:, -1]


def peer_block(xb, w_query, sub_keys_1, sub_keys_2, peer_u, peer_v):
    tb = xb.shape[0]
    half = PEER_DKEY // 2
    q = jnp.einsum('td,de->te', xb, w_query).astype(jnp.float32).reshape(tb, PEER_HEADS, PEER_DKEY)
    s1 = jnp.einsum('thc,hkc->thk', q[..., :half], sub_keys_1.astype(jnp.float32))
    s2 = jnp.einsum('thc,hkc->thk', q[..., half:], sub_keys_2.astype(jnp.float32))
    v1, i1 = lax.top_k(s1, PEER_TOPK)
    v2, i2 = lax.top_k(s2, PEER_TOPK)
    cand = (v1[..., :, None] + v2[..., None, :]).reshape(tb, PEER_HEADS, PEER_TOPK * PEER_TOPK)
    cidx = (i1[..., :, None] * PEER_KEYS + i2[..., None, :]).reshape(tb, PEER_HEADS, PEER_TOPK * PEER_TOPK)
    top_s, pos = lax.top_k(cand, PEER_TOPK)
    eidx = jnp.take_along_axis(cidx, pos, axis=-1)
    g = jax.nn.softmax(top_s, axis=-1)
    ue = peer_u[eidx]
    act = jax.nn.gelu(jnp.einsum('td,thkd->thk', xb.astype(jnp.float32), ue.astype(jnp.float32)))
    ve = peer_v[eidx]
    return jnp.einsum('thk,thkd->td', g * act, ve.astype(jnp.float32)).astype(xb.dtype)


def peer_ffn(xn, w_query, sub_keys_1, sub_keys_2, peer_u, peer_v):
    B, T, D = xn.shape
    n = B * T
    pad = (-n) % PEER_BLOCK
    xp = jnp.pad(xn.reshape(n, D), ((0, pad), (0, 0))).reshape(-1, PEER_BLOCK, D)
    out = lax.map(lambda xb: peer_block(xb, w_query, sub_keys_1, sub_keys_2, peer_u, peer_v), xp)
    return out.reshape(-1, D)[:n].reshape(B, T, D)


def layer_forward(x, kv_cache, conv_buf, h0, first_pos, rel_bias, ln1_g, w_in, q_norm_g, k_norm_g,
                  conv_w, conv_b, w_rg, b_rg, w_ig, b_ig, lru_L, w_pa, w_pb, w_o, ln2_g,
                  w_query, sub_keys_1, sub_keys_2, peer_u, peer_v):
    B, T, _ = x.shape
    xn = rmsnorm(x, ln1_g)
    z = jnp.einsum('btd,de->bte', xn, w_in)
    q = rmsnorm(z[..., Q_OFF:K_OFF].reshape(B, T, N_ATTN_HEADS, HEAD_DIM), q_norm_g)
    k = rmsnorm(z[..., K_OFF:V_OFF].reshape(B, T, N_ATTN_HEADS, HEAD_DIM), k_norm_g)
    v = z[..., V_OFF:XB_OFF].reshape(B, T, N_ATTN_HEADS, HEAD_DIM)
    outs, lses, new_k, new_v = [], [], [], []
    for g, (window, dilation) in enumerate(ATTN_GROUPS):
        hs = slice(g * HEADS_PER_GROUP, (g + 1) * HEADS_PER_GROUP)
        bias = group_bias(rel_bias[:, hs], window, dilation)
        q_g, k_g, v_g = q[:, :, hs], k[:, :, hs], v[:, :, hs]
        if kv_cache is None:
            o, l = prompt_group_attention(q_g, k_g, v_g, window, dilation, bias)
            keep = min(window, T)
            new_k.append(k_g[:, T - keep:])
            new_v.append(v_g[:, T - keep:])
        else:
            o, l = sample_group_attention(q_g, k_g, v_g, kv_cache[2 * g], kv_cache[2 * g + 1],
                                          window, dilation, bias)
            new_k.append(k_g)
            new_v.append(v_g)
        outs.append(o)
        lses.append(l)
    w_mix = jax.nn.softmax(jnp.stack(lses), axis=0)
    attn = jnp.einsum('gbth,gbthd->bthd', w_mix, jnp.stack(outs)).reshape(B, T, ATTN_OUT_WIDTH).astype(x.dtype)
    rec, new_buf, h_last = recurrent_branch(z[..., XB_OFF:GB_OFF], z[..., GB_OFF:GA_OFF], conv_buf, h0, first_pos,
                                            conv_w, conv_b, w_rg, b_rg, w_ig, b_ig, lru_L)
    merged = (jax.nn.sigmoid(z[..., GA_OFF:GR_OFF]) * jnp.einsum('bte,ed->btd', attn, w_pa)
              + jax.nn.sigmoid(z[..., GR_OFF:]) * jnp.einsum('bte,ed->btd', rec, w_pb))
    x = x + jnp.einsum('btd,de->bte', merged, w_o)
    x = x + peer_ffn(rmsnorm(x, ln2_g), w_query, sub_keys_1, sub_keys_2, peer_u, peer_v)
    return x, new_k, new_v, h_last, new_buf


def setup_inputs(seed: int = 0) -> dict:
    key = jax.random.key(seed)
    ks = iter(jax.random.split(key, 48))

    def nrm(shape, scale):
        return scale * jax.random.normal(next(ks), shape, jnp.float32)

    inp = {}
    inp["x_prompt"] = nrm((BATCH, SEQ, D_MODEL), 1.0)
    inp["x_sample"] = nrm((DEC_BATCH, DEC_SEQ, D_MODEL), 1.0)
    for (window, _) in ATTN_GROUPS:
        wb = min(window, PAST_LEN)
        inp[f"cache_k_w{window}"] = nrm((DEPTH, DEC_BATCH, wb, HEADS_PER_GROUP, HEAD_DIM), 1.0)
        inp[f"cache_v_w{window}"] = nrm((DEPTH, DEC_BATCH, wb, HEADS_PER_GROUP, HEAD_DIM), 1.0)
    inp["state_lru_h"] = nrm((DEPTH, DEC_BATCH, LRU_WIDTH), 0.5)
    inp["state_conv"] = nrm((DEPTH, DEC_BATCH, CONV_WIDTH - 1, LRU_WIDTH), 1.0)
    inp["rel_bias"] = nrm((N_BUCKETS, N_ATTN_HEADS), 0.5)
    inp["ln1_g"] = 1.0 + nrm((DEPTH, D_MODEL), 0.01)
    inp["w_in"] = nrm((DEPTH, D_MODEL, IN_COLS), D_MODEL ** -0.5)
    inp["q_norm_g"] = 1.0 + nrm((DEPTH, HEAD_DIM), 0.01)
    inp["k_norm_g"] = 1.0 + nrm((DEPTH, HEAD_DIM), 0.01)
    inp["conv_w"] = nrm((DEPTH, CONV_WIDTH, LRU_WIDTH), CONV_WIDTH ** -0.5)
    inp["conv_b"] = nrm((DEPTH, LRU_WIDTH), 0.01)
    inp["w_rg"] = nrm((DEPTH, LRU_BLOCKS, LRU_BLOCK, LRU_BLOCK), LRU_BLOCK ** -0.5)
    inp["b_rg"] = nrm((DEPTH, LRU_WIDTH), 0.01)
    inp["w_ig"] = nrm((DEPTH, LRU_BLOCKS, LRU_BLOCK, LRU_BLOCK), LRU_BLOCK ** -0.5)
    inp["b_ig"] = nrm((DEPTH, LRU_WIDTH), 0.01)
    a0 = jax.random.uniform(next(ks), (DEPTH, LRU_WIDTH), jnp.float32, 0.9, 0.999)
    base = a0 ** (1.0 / LRU_C)
    inp["lru_L"] = jnp.log(base) - jnp.log1p(-base)
    inp["w_pa"] = nrm((DEPTH, ATTN_OUT_WIDTH, D_MODEL), ATTN_OUT_WIDTH ** -0.5)
    inp["w_pb"] = nrm((DEPTH, LRU_WIDTH, D_MODEL), LRU_WIDTH ** -0.5)
    inp["w_o"] = nrm((DEPTH, D_MODEL, D_MODEL), D_MODEL ** -0.5)
    inp["ln2_g"] = 1.0 + nrm((DEPTH, D_MODEL), 0.01)
    inp["w_query"] = nrm((DEPTH, D_MODEL, PEER_HEADS * PEER_DKEY), D_MODEL ** -0.5)
    inp["sub_keys_1"] = nrm((DEPTH, PEER_HEADS, PEER_KEYS, PEER_DKEY // 2), (PEER_DKEY // 2) ** -0.5)
    inp["sub_keys_2"] = nrm((DEPTH, PEER_HEADS, PEER_KEYS, PEER_DKEY // 2), (PEER_DKEY // 2) ** -0.5)
    inp["peer_u"] = nrm((DEPTH, PEER_EXPERTS, D_MODEL), D_MODEL ** -0.5)
    inp["peer_v"] = nrm((DEPTH, PEER_EXPERTS, D_MODEL), 0.1)
    return inp


def reference(x_prompt, x_sample, cache_k_w128, cache_v_w128, cache_k_w512, cache_v_w512,
              cache_k_w2048, cache_v_w2048, state_lru_h, state_conv, rel_bias, ln1_g, w_in,
              q_norm_g, k_norm_g, conv_w, conv_b, w_rg, b_rg, w_ig, b_ig, lru_L, w_pa, w_pb, w_o,
              ln2_g, w_query, sub_keys_1, sub_keys_2, peer_u, peer_v):
    y_p, y_s = x_prompt, x_sample
    kp = [[] for _ in range(N_GROUPS)]
    vp = [[] for _ in range(N_GROUPS)]
    ksm = [[] for _ in range(N_GROUPS)]
    vsm = [[] for _ in range(N_GROUPS)]
    hp, hsm, cp, csm = [], [], [], []
    for l in range(DEPTH):
        lw = (ln1_g[l], w_in[l], q_norm_g[l], k_norm_g[l], conv_w[l], conv_b[l], w_rg[l], b_rg[l],
              w_ig[l], b_ig[l], lru_L[l], w_pa[l], w_pb[l], w_o[l], ln2_g[l], w_query[l],
              sub_keys_1[l], sub_keys_2[l], peer_u[l], peer_v[l])
        zero_buf = jnp.zeros((y_p.shape[0], CONV_WIDTH - 1, LRU_WIDTH), y_p.dtype)
        zero_h = jnp.zeros((y_p.shape[0], LRU_WIDTH), jnp.float32)
        y_p, nk, nv, h_last, buf = layer_forward(y_p, None, zero_buf, zero_h, 0, rel_bias, *lw)
        for g in range(N_GROUPS):
            kp[g].append(nk[g])
            vp[g].append(nv[g])
        hp.append(h_last)
        cp.append(buf)
        caches = (cache_k_w128[l], cache_v_w128[l], cache_k_w512[l], cache_v_w512[l],
                  cache_k_w2048[l], cache_v_w2048[l])
        y_s, nk, nv, h_last, buf = layer_forward(y_s, caches, state_conv[l], state_lru_h[l], PAST_LEN, rel_bias, *lw)
        for g in range(N_GROUPS):
            ksm[g].append(nk[g])
            vsm[g].append(nv[g])
        hsm.append(h_last)
        csm.append(buf)
    return (y_p, y_s,
            jnp.stack(kp[0]), jnp.stack(vp[0]), jnp.stack(ksm[0]), jnp.stack(vsm[0]),
            jnp.stack(kp[1]), jnp.stack(vp[1]), jnp.stack(ksm[1]), jnp.stack(vsm[1]),
            jnp.stack(kp[2]), jnp.stack(vp[2]), jnp.stack(ksm[2]), jnp.stack(vsm[2]),
            jnp.stack(hp), jnp.stack(hsm), jnp.stack(cp), jnp.stack(csm))
```

```python
import functools
import math

import jax
import jax.numpy as jnp
import numpy as np
from jax import lax
from jax.experimental import pallas as pl
from jax.experimental.pallas import tpu as pltpu

F32 = jnp.float32
BF16 = jnp.bfloat16

D_MODEL = 4096
BATCH = 4
SEQ = 2048
DEC_BATCH = 128
N_PROMPT = BATCH * SEQ
N_TOK = N_PROMPT + DEC_BATCH
HEAD_DIM = 128
HEADS_PER_GROUP = 8
GROUP_COLS = HEADS_PER_GROUP * HEAD_DIM
ATTN_GROUPS = ((128, 1), (512, 4), (2048, 16))
N_GROUPS = 3
ATTN_WIDTH = N_GROUPS * GROUP_COLS
N_KEYS = 129
N_BUCKETS = 32
RP_MAX_DIST = 2048
LRU_WIDTH = 2048
LRU_BLOCKS = 16
LRU_BLOCK = 128
CONV_WIDTH = 4
LRU_C = 8.0
PEER_HEADS = 8
PEER_KEYS = 128
PEER_EXPERTS = PEER_KEYS * PEER_KEYS
PEER_DKEY = 256
PEER_TOPK = 16
EPS = 1e-6

Q_OFF = 0
K_OFF = ATTN_WIDTH
V_OFF = 2 * ATTN_WIDTH
XB_OFF = 3 * ATTN_WIDTH
GA_OFF = XB_OFF + 2 * LRU_WIDTH

ROW_BLK = 128
N_PROMPT_BLK = N_PROMPT // ROW_BLK
N_ROW_BLK = N_TOK // ROW_BLK
MM_TM = 640
MM_TN = 512
PEER_TM = 512
PEER_TE = 512
NEG = -1e30
VMEM_LIMIT = 56 * 1024 * 1024


def _cparams(*sem):
    return pltpu.CompilerParams(dimension_semantics=sem, vmem_limit_bytes=VMEM_LIMIT)


def _gelu(x):
    return jax.nn.gelu(x)


def _sigmoid(x):
    return 1.0 / (1.0 + jnp.exp(-x))


def _norm1_kernel(xp_ref, xs_ref, g_ref, xn_ref, xall_ref):
    i = pl.program_id(0)

    def body(x):
        ms = jnp.mean(x * x, axis=-1, keepdims=True)
        xn_ref[...] = (x * lax.rsqrt(ms + EPS) * g_ref[...]).astype(BF16)
        xall_ref[...] = x

    @pl.when(i < N_PROMPT_BLK)
    def _():
        body(xp_ref[...])

    @pl.when(i == N_PROMPT_BLK)
    def _():
        body(xs_ref[...])


def _norm1(xp, xs, g):
    return pl.pallas_call(
        _norm1_kernel,
        grid=(N_ROW_BLK,),
        in_specs=[
            pl.BlockSpec((ROW_BLK, D_MODEL), lambda i: (jnp.minimum(i, N_PROMPT_BLK - 1), 0)),
            pl.BlockSpec((ROW_BLK, D_MODEL), lambda i: (0, 0)),
            pl.BlockSpec((1, D_MODEL), lambda i: (0, 0)),
        ],
        out_specs=[
            pl.BlockSpec((ROW_BLK, D_MODEL), lambda i: (i, 0)),
            pl.BlockSpec((ROW_BLK, D_MODEL), lambda i: (i, 0)),
        ],
        out_shape=[
            jax.ShapeDtypeStruct((N_TOK, D_MODEL), BF16),
            jax.ShapeDtypeStruct((N_TOK, D_MODEL), F32),
        ],
        compiler_params=_cparams("arbitrary"),
        name="norm1",
    )(xp, xs, g)


def _proj_kernel(*refs, n_extra, epilogue):
    x_ref, w_ref = refs[0], refs[1]
    extra = refs[2:2 + n_extra]
    outs = refs[2 + n_extra:-1]
    wbf_ref = refs[-1]

    @pl.when(pl.program_id(1) == 0)
    def _():
        wbf_ref[...] = w_ref[...].astype(BF16)

    acc = jnp.dot(x_ref[...], wbf_ref[...], preferred_element_type=F32)
    epilogue(acc, extra, outs)


def _proj(x, w, col0, ncols, epilogue, out_dtypes, extra=(), extra_specs=()):
    m, k = x.shape
    grid = (ncols // MM_TN, m // MM_TM)
    cb0 = col0 // MM_TN
    kern = functools.partial(_proj_kernel, n_extra=len(extra), epilogue=epilogue)
    return pl.pallas_call(
        kern,
        grid=grid,
        in_specs=[
            pl.BlockSpec((MM_TM, k), lambda j, i: (i, 0)),
            pl.BlockSpec((k, MM_TN), lambda j, i: (0, cb0 + j)),
            *extra_specs,
        ],
        out_specs=[pl.BlockSpec((MM_TM, MM_TN), lambda j, i: (i, j)) for _ in out_dtypes],
        out_shape=[jax.ShapeDtypeStruct((m, ncols), dt) for dt in out_dtypes],
        scratch_shapes=[pltpu.VMEM((k, MM_TN), BF16)],
        compiler_params=_cparams("arbitrary", "arbitrary"),
        name="proj",
    )(x, w, *extra)


def _head_norm(a, g):
    ms = jnp.mean(a * a, axis=-1, keepdims=True)
    return a * lax.rsqrt(ms + EPS) * g


def _epi_q(acc, extra, outs):
    g = extra[0][...]
    for h in range(MM_TN // HEAD_DIM):
        sl = slice(h * HEAD_DIM, (h + 1) * HEAD_DIM)
        outs[0][:, sl] = (_head_norm(acc[:, sl], g) * (HEAD_DIM ** -0.5)).astype(BF16)


def _epi_k(acc, extra, outs):
    g = extra[0][...]
    for h in range(MM_TN // HEAD_DIM):
        sl = slice(h * HEAD_DIM, (h + 1) * HEAD_DIM)
        y = _head_norm(acc[:, sl], g)
        outs[0][:, sl] = y
        outs[1][:, sl] = y.astype(BF16)


def _epi_v(acc, extra, outs):
    outs[0][...] = acc
    outs[1][...] = acc.astype(BF16)


def _epi_f32(acc, extra, outs):
    outs[0][...] = acc


def _epi_sigmoid(acc, extra, outs):
    outs[0][...] = _sigmoid(acc).astype(BF16)


def _attn_p_kernel(q_ref, kp_ref, kc_ref, vp_ref, vc_ref, t_ref, o_ref, lse_ref):
    first = pl.program_id(2) == 0
    col = lax.broadcasted_iota(jnp.int32, (ROW_BLK, 2 * ROW_BLK), 1)
    dead = jnp.logical_and(col < ROW_BLK, first)
    for h in range(HEADS_PER_GROUP):
        sl = slice(h * HEAD_DIM, (h + 1) * HEAD_DIM)
        kk = jnp.concatenate([kp_ref[:, sl], kc_ref[:, sl]], axis=0)
        vv = jnp.concatenate([vp_ref[:, sl], vc_ref[:, sl]], axis=0)
        lg = lax.dot_general(q_ref[:, sl], kk, (((1,), (1,)), ((), ())),
                             preferred_element_type=F32) + t_ref[h]
        lg = jnp.where(dead, NEG, lg)
        m = jnp.max(lg, axis=-1, keepdims=True)
        p = jnp.exp(lg - m)
        s = jnp.sum(p, axis=-1, keepdims=True)
        o_ref[:, sl] = jnp.dot(p.astype(BF16), vv, preferred_element_type=F32) / s
        lse_ref[:, h:h + 1] = m + jnp.log(s)


def _attn_prompt(qh, kb, vb, tbias, g, d):
    nblk = SEQ // d // ROW_BLK
    rs = lambda a: a.reshape(N_TOK // d, d * ATTN_WIDTH)
    cur = lambda b, r, i: (b * nblk + i, r * N_GROUPS + g)
    prev = lambda b, r, i: (b * nblk + jnp.maximum(i - 1, 0), r * N_GROUPS + g)
    blk = (ROW_BLK, GROUP_COLS)
    o, lse = pl.pallas_call(
        _attn_p_kernel,
        grid=(BATCH, d, nblk),
        in_specs=[
            pl.BlockSpec(blk, cur),
            pl.BlockSpec(blk, prev),
            pl.BlockSpec(blk, cur),
            pl.BlockSpec(blk, prev),
            pl.BlockSpec(blk, cur),
            pl.BlockSpec((HEADS_PER_GROUP, ROW_BLK, 2 * ROW_BLK), lambda b, r, i: (0, 0, 0)),
        ],
        out_specs=[
            pl.BlockSpec(blk, lambda b, r, i: (b * nblk + i, r)),
            pl.BlockSpec((ROW_BLK, HEADS_PER_GROUP), lambda b, r, i: ((b * d + r) * nblk + i, 0)),
        ],
        out_shape=[
            jax.ShapeDtypeStruct((N_PROMPT // d, d * GROUP_COLS), F32),
            jax.ShapeDtypeStruct((N_PROMPT, HEADS_PER_GROUP), F32),
        ],
        compiler_params=_cparams("arbitrary", "arbitrary", "arbitrary"),
        name="attn_p",
    )(rs(qh), rs(kb), rs(kb), rs(vb), rs(vb), tbias)
    return o.reshape(N_PROMPT, GROUP_COLS), lse


SAMPLE_BB = 8


def _attn_s_kernel(q_ref, kn_ref, vn_ref, ck_ref, cv_ref, bc_ref, b0_ref, o_ref, lse_ref):
    def one(bi, carry):
        for h in range(HEADS_PER_GROUP):
            sl = slice(h * HEAD_DIM, (h + 1) * HEAD_DIM)
            q = q_ref[bi, :, sl]
            lc = jnp.sum(ck_ref[bi, :, sl] * q, axis=-1, keepdims=True) + bc_ref[:, h:h + 1]
            ln = jnp.sum(kn_ref[bi, :, sl] * q, axis=-1, keepdims=True) + b0_ref[:, h:h + 1]
            m = jnp.maximum(jnp.max(lc, axis=0, keepdims=True), ln)
            p = jnp.exp(lc - m)
            pn = jnp.exp(ln - m)
            s = jnp.sum(p, axis=0, keepdims=True) + pn
            o = jnp.sum(p * cv_ref[bi, :, sl], axis=0, keepdims=True) + pn * vn_ref[bi, :, sl]
            o_ref[bi, :, sl] = o / s
            lse_ref[bi, :, h:h + 1] = m + jnp.log(s)
        return carry

    lax.fori_loop(0, SAMPLE_BB, one, 0)


def _attn_sample(qh, k32, v32, cache_k, cache_v, bias, g, d):
    wb = cache_k.shape[1]
    ck = cache_k.reshape(DEC_BATCH, wb // d, d * GROUP_COLS)
    cv = cache_v.reshape(DEC_BATCH, wb // d, d * GROUP_COLS)
    bc = bias[::-1][:N_KEYS - 1]
    b0 = bias[:1]
    cols = slice(g * GROUP_COLS, (g + 1) * GROUP_COLS)
    rows3 = lambda a: a[N_PROMPT:, cols].astype(F32).reshape(DEC_BATCH, 1, GROUP_COLS)
    new = pl.BlockSpec((SAMPLE_BB, 1, GROUP_COLS), lambda s: (s, 0, 0))
    cache = pl.BlockSpec((SAMPLE_BB, N_KEYS - 1, GROUP_COLS), lambda s: (s, 0, 0))
    o, lse = pl.pallas_call(
        _attn_s_kernel,
        grid=(DEC_BATCH // SAMPLE_BB,),
        in_specs=[
            new, new, new, cache, cache,
            pl.BlockSpec((N_KEYS - 1, HEADS_PER_GROUP), lambda s: (0, 0)),
            pl.BlockSpec((1, HEADS_PER_GROUP), lambda s: (0, 0)),
        ],
        out_specs=[
            new,
            pl.BlockSpec((SAMPLE_BB, 1, HEADS_PER_GROUP), lambda s: (s, 0, 0)),
        ],
        out_shape=[
            jax.ShapeDtypeStruct((DEC_BATCH, 1, GROUP_COLS), F32),
            jax.ShapeDtypeStruct((DEC_BATCH, 1, HEADS_PER_GROUP), F32),
        ],
        compiler_params=_cparams("arbitrary"),
        name="attn_s",
    )(rows3(qh), rows3(k32), rows3(v32), ck, cv, bc, b0)
    return o.reshape(DEC_BATCH, GROUP_COLS), lse.reshape(DEC_BATCH, HEADS_PER_GROUP)


def _merge_kernel(*refs):
    op, lp, osm, ls = refs[0:3], refs[3:6], refs[6:9], refs[9:12]
    out_ref = refs[12]
    i = pl.program_id(0)

    def body(o, l):
        l0, l1, l2 = l[0][...], l[1][...], l[2][...]
        m = jnp.maximum(jnp.maximum(l0, l1), l2)
        e = [jnp.exp(l0 - m), jnp.exp(l1 - m), jnp.exp(l2 - m)]
        inv = 1.0 / (e[0] + e[1] + e[2])
        for h in range(HEADS_PER_GROUP):
            sl = slice(h * HEAD_DIM, (h + 1) * HEAD_DIM)
            acc = (e[0][:, h:h + 1] * inv[:, h:h + 1]) * o[0][:, sl]
            acc = acc + (e[1][:, h:h + 1] * inv[:, h:h + 1]) * o[1][:, sl]
            acc = acc + (e[2][:, h:h + 1] * inv[:, h:h + 1]) * o[2][:, sl]
            out_ref[:, sl] = acc.astype(BF16)

    @pl.when(i < N_PROMPT_BLK)
    def _():
        body(op, lp)

    @pl.when(i == N_PROMPT_BLK)
    def _():
        body(osm, ls)


def _merge(o_p, lse_p, o_s, lse_s):
    pidx = lambda i: (jnp.minimum(i, N_PROMPT_BLK - 1), 0)
    sidx = lambda i: (0, 0)
    return pl.pallas_call(
        _merge_kernel,
        grid=(N_ROW_BLK,),
        in_specs=(
            [pl.BlockSpec((ROW_BLK, GROUP_COLS), pidx)] * 3
            + [pl.BlockSpec((ROW_BLK, HEADS_PER_GROUP), pidx)] * 3
            + [pl.BlockSpec((ROW_BLK, GROUP_COLS), sidx)] * 3
            + [pl.BlockSpec((ROW_BLK, HEADS_PER_GROUP), sidx)] * 3
        ),
        out_specs=pl.BlockSpec((ROW_BLK, GROUP_COLS), lambda i: (i, 0)),
        out_shape=jax.ShapeDtypeStruct((N_TOK, GROUP_COLS), BF16),
        compiler_params=_cparams("arbitrary"),
        name="merge",
    )(*o_p, *lse_p, *o_s, *lse_s)


SCAN_CHUNK = 64


def _softplus(x):
    return jnp.maximum(x, 0.0) + jnp.log1p(jnp.exp(-jnp.abs(x)))


def _gates(conv, wr_ref, br_ref, wi_ref, bi_ref, l_ref):
    cb = conv.astype(BF16)
    r = _sigmoid(jnp.dot(cb, wr_ref[0].astype(BF16), preferred_element_type=F32) + br_ref[...])
    ig = _sigmoid(jnp.dot(cb, wi_ref[0].astype(BF16), preferred_element_type=F32) + bi_ref[...])
    log_a = -LRU_C * r * _softplus(-l_ref[...])
    return log_a, ig


def _decay(log_a):
    a = jnp.exp(log_a)
    return a, jnp.sqrt(-jnp.tanh(log_a) * (a * a + 1.0))


def _rec_p_kernel(xb_ref, gb_ref, cw_ref, cb_ref, wr_ref, br_ref, wi_ref, bi_ref, l_ref,
                  y_ref, h_ref, a_scr, b_scr):
    t = SEQ
    x = xb_ref[...]
    row = lax.broadcasted_iota(jnp.int32, (t, LRU_BLOCK), 0)
    conv = cb_ref[...] + x * cw_ref[CONV_WIDTH - 1:CONV_WIDTH, :]
    for s in range(1, CONV_WIDTH):
        xs = jnp.where(row >= s, pltpu.roll(x, s, 0), 0.0)
        conv = conv + xs * cw_ref[CONV_WIDTH - 1 - s:CONV_WIDTH - s, :]
    log_a, ig = _gates(conv, wr_ref, br_ref, wi_ref, bi_ref, l_ref)
    a, mult = _decay(log_a)
    start = row == 0
    a_scr[...] = jnp.where(start, 0.0, a)
    b_scr[...] = jnp.where(start, 1.0, mult) * ig * conv

    crow = lax.broadcasted_iota(jnp.int32, (SCAN_CHUNK, LRU_BLOCK), 0)

    def chunk(c, h_prev):
        rows = pl.ds(pl.multiple_of(c * SCAN_CHUNK, SCAN_CHUNK), SCAN_CHUNK)
        a_c = a_scr[rows, :]
        b_c = b_scr[rows, :]
        s = 1
        while s < SCAN_CHUNK:
            keep = crow >= s
            a_sh = jnp.where(keep, pltpu.roll(a_c, s, 0), 1.0)
            b_sh = jnp.where(keep, pltpu.roll(b_c, s, 0), 0.0)
            b_c = a_c * b_sh + b_c
            a_c = a_c * a_sh
            s *= 2
        h = a_c * h_prev + b_c
        y_ref[rows, :] = (h * _gelu(gb_ref[rows, :])).astype(BF16)
        return h[SCAN_CHUNK - 1:SCAN_CHUNK, :]

    h_last = lax.fori_loop(0, t // SCAN_CHUNK, chunk, jnp.zeros((1, LRU_BLOCK), F32))
    h_ref[0] = h_last


def _rec_prompt(xg, conv_w, conv_b, w_rg, b_rg, w_ig, b_ig, lru_l):
    nb = LRU_BLOCKS
    vec = pl.BlockSpec((1, LRU_BLOCK), lambda b, n: (0, n))
    wblk = pl.BlockSpec((1, LRU_BLOCK, LRU_BLOCK), lambda b, n: (n, 0, 0))
    y, h = pl.pallas_call(
        _rec_p_kernel,
        grid=(BATCH, nb),
        in_specs=[
            pl.BlockSpec((SEQ, LRU_BLOCK), lambda b, n: (b, n)),
            pl.BlockSpec((SEQ, LRU_BLOCK), lambda b, n: (b, nb + n)),
            pl.BlockSpec((CONV_WIDTH, LRU_BLOCK), lambda b, n: (0, n)),
            vec, wblk, vec, wblk, vec, vec,
        ],
        out_specs=[
            pl.BlockSpec((SEQ, LRU_BLOCK), lambda b, n: (b, n)),
            pl.BlockSpec((1, 1, LRU_BLOCK), lambda b, n: (b, 0, n)),
        ],
        out_shape=[
            jax.ShapeDtypeStruct((N_PROMPT, LRU_WIDTH), BF16),
            jax.ShapeDtypeStruct((BATCH, 1, LRU_WIDTH), F32),
        ],
        scratch_shapes=[pltpu.VMEM((SEQ, LRU_BLOCK), F32), pltpu.VMEM((SEQ, LRU_BLOCK), F32)],
        compiler_params=_cparams("arbitrary", "arbitrary"),
        name="rec_p",
    )(xg, xg, conv_w, conv_b, w_rg, b_rg, w_ig, b_ig, lru_l)
    return y, h.reshape(BATCH, LRU_WIDTH)


def _rec_s_kernel(xb_ref, gb_ref, c0_ref, c1_ref, c2_ref, h0_ref, cw_ref, cb_ref,
                  wr_ref, br_ref, wi_ref, bi_ref, l_ref, y_ref, h_ref):
    conv = (cb_ref[...] + c0_ref[...] * cw_ref[0:1, :] + c1_ref[...] * cw_ref[1:2, :]
            + c2_ref[...] * cw_ref[2:3, :] + xb_ref[...] * cw_ref[3:4, :])
    log_a, ig = _gates(conv, wr_ref, br_ref, wi_ref, bi_ref, l_ref)
    a, mult = _decay(log_a)
    h = a * h0_ref[...] + mult * ig * conv
    h_ref[...] = h
    y_ref[...] = (h * _gelu(gb_ref[...])).astype(BF16)


def _rec_sample(xg, state_conv, h0, conv_w, conv_b, w_rg, b_rg, w_ig, b_ig, lru_l):
    nb = LRU_BLOCKS
    sc = state_conv.reshape(DEC_BATCH, (CONV_WIDTH - 1) * LRU_WIDTH)
    srow = N_PROMPT // DEC_BATCH
    vec = pl.BlockSpec((1, LRU_BLOCK), lambda n: (0, n))
    wblk = pl.BlockSpec((1, LRU_BLOCK, LRU_BLOCK), lambda n: (n, 0, 0))
    tile = lambda f: pl.BlockSpec((DEC_BATCH, LRU_BLOCK), f)
    return pl.pallas_call(
        _rec_s_kernel,
        grid=(nb,),
        in_specs=[
            tile(lambda n: (srow, n)),
            tile(lambda n: (srow, nb + n)),
            tile(lambda n: (0, n)),
            tile(lambda n: (0, nb + n)),
            tile(lambda n: (0, 2 * nb + n)),
            tile(lambda n: (0, n)),
            pl.BlockSpec((CONV_WIDTH, LRU_BLOCK), lambda n: (0, n)),
            vec, wblk, vec, wblk, vec, vec,
        ],
        out_specs=[tile(lambda n: (0, n)), tile(lambda n: (0, n))],
        out_shape=[
            jax.ShapeDtypeStruct((DEC_BATCH, LRU_WIDTH), BF16),
            jax.ShapeDtypeStruct((DEC_BATCH, LRU_WIDTH), F32),
        ],
        compiler_params=_cparams("arbitrary"),
        name="rec_s",
    )(xg, xg, sc, sc, sc, h0, conv_w, conv_b, w_rg, b_rg, w_ig, b_ig, lru_l)


def _mix_kernel(at_ref, rc_ref, ga_ref, gr_ref, wa_ref, wb_ref, o_ref, wa_bf, wb_bf):
    @pl.when(pl.program_id(1) == 0)
    def _():
        wa_bf[...] = wa_ref[...].astype(BF16)
        wb_bf[...] = wb_ref[...].astype(BF16)

    pa = jnp.dot(at_ref[...], wa_bf[...], preferred_element_type=F32)
    pb = jnp.dot(rc_ref[...], wb_bf[...], preferred_element_type=F32)
    o_ref[...] = (ga_ref[...].astype(F32) * pa + gr_ref[...].astype(F32) * pb).astype(BF16)


def _mix(attn, rec, gates, w_pa, w_pb):
    nj = D_MODEL // MM_TN
    return pl.pallas_call(
        _mix_kernel,
        grid=(nj, N_TOK // MM_TM),
        in_specs=[
            pl.BlockSpec((MM_TM, GROUP_COLS), lambda j, i: (i, 0)),
            pl.BlockSpec((MM_TM, LRU_WIDTH), lambda j, i: (i, 0)),
            pl.BlockSpec((MM_TM, MM_TN), lambda j, i: (i, j)),
            pl.BlockSpec((MM_TM, MM_TN), lambda j, i: (i, nj + j)),
            pl.BlockSpec((GROUP_COLS, MM_TN), lambda j, i: (0, j)),
            pl.BlockSpec((LRU_WIDTH, MM_TN), lambda j, i: (0, j)),
        ],
        out_specs=pl.BlockSpec((MM_TM, MM_TN), lambda j, i: (i, j)),
        out_shape=jax.ShapeDtypeStruct((N_TOK, D_MODEL), BF16),
        scratch_shapes=[pltpu.VMEM((GROUP_COLS, MM_TN), BF16), pltpu.VMEM((LRU_WIDTH, MM_TN), BF16)],
        compiler_params=_cparams("arbitrary", "arbitrary"),
        name="mix",
    )(attn, rec, gates, gates, w_pa, w_pb)


def _epi_residual(acc, extra, outs):
    outs[0][...] = extra[0][...] + acc


def _norm2_kernel(x_ref, g_ref, xn_ref, xnt_ref):
    x = x_ref[...]
    ms = jnp.mean(x * x, axis=-1, keepdims=True)
    y = x * lax.rsqrt(ms + EPS) * g_ref[...]
    xn_ref[...] = y.astype(BF16)
    xnt_ref[...] = y.T.astype(BF16)


def _norm2(x1, g):
    return pl.pallas_call(
        _norm2_kernel,
        grid=(N_ROW_BLK,),
        in_specs=[
            pl.BlockSpec((ROW_BLK, D_MODEL), lambda i: (i, 0)),
            pl.BlockSpec((1, D_MODEL), lambda i: (0, 0)),
        ],
        out_specs=[
            pl.BlockSpec((ROW_BLK, D_MODEL), lambda i: (i, 0)),
            pl.BlockSpec((D_MODEL, ROW_BLK), lambda i: (0, i)),
        ],
        out_shape=[
            jax.ShapeDtypeStruct((N_TOK, D_MODEL), BF16),
            jax.ShapeDtypeStruct((D_MODEL, N_TOK), BF16),
        ],
        compiler_params=_cparams("arbitrary"),
        name="norm2",
    )(x1, g)


def _top_values(x, k):
    rows = lax.broadcasted_iota(jnp.int32, x.shape, 0)
    out = []
    for _ in range(k):
        m = jnp.max(x, axis=0, keepdims=True)
        first = jnp.min(jnp.where(x == m, rows, x.shape[0]), axis=0, keepdims=True)
        out.append(m)
        x = jnp.where(rows == first, -jnp.inf, x)
    return out


def _peer_prep_kernel(x_ref, wq_ref, k1_ref, k2_ref, s1_ref, s2_ref, e1_ref, e2_ref, tau_ref, wq_bf):
    @pl.when(pl.program_id(1) == 0)
    def _():
        wq_bf[...] = wq_ref[...].astype(BF16)

    half = PEER_DKEY // 2
    q = jnp.dot(x_ref[...], wq_bf[...], preferred_element_type=F32).astype(BF16)
    nt = (((1,), (1,)), ((), ()))
    s1 = lax.dot_general(k1_ref[0].astype(BF16), q[:, :half], nt, preferred_element_type=F32)
    s2 = lax.dot_general(k2_ref[0].astype(BF16), q[:, half:], nt, preferred_element_type=F32)
    a = _top_values(s1, PEER_TOPK)
    b = _top_values(s2, PEER_TOPK)
    cand = jnp.concatenate(
        [a[i] + b[j] for i in range(PEER_TOPK) for j in range(PEER_TOPK) if (i + 1) * (j + 1) <= PEER_TOPK],
        axis=0)
    c = _top_values(cand, PEER_TOPK)
    z = jnp.ones_like(c[0])
    for v in c[1:]:
        z = z + jnp.exp(v - c[0])
    s1_ref[0] = s1
    s2_ref[0] = s2
    e1_ref[0] = jnp.exp(s1 - a[0]) / z
    e2_ref[0] = jnp.exp(s2 - b[0])
    tau_ref[0] = c[PEER_TOPK - 1]


def _peer_prep(xn2, w_query, k1, k2):
    tm = MM_TM
    half = PEER_DKEY // 2
    big = pl.BlockSpec((1, PEER_KEYS, tm), lambda h, i: (h, 0, i))
    keys = pl.BlockSpec((1, PEER_KEYS, half), lambda h, i: (h, 0, 0))
    shp = jax.ShapeDtypeStruct((PEER_HEADS, PEER_KEYS, N_TOK), F32)
    return pl.pallas_call(
        _peer_prep_kernel,
        grid=(PEER_HEADS, N_TOK // tm),
        in_specs=[
            pl.BlockSpec((tm, D_MODEL), lambda h, i: (i, 0)),
            pl.BlockSpec((D_MODEL, PEER_DKEY), lambda h, i: (0, h)),
            keys, keys,
        ],
        out_specs=[big, big, big, big, pl.BlockSpec((1, 1, tm), lambda h, i: (h, 0, i))],
        out_shape=[shp, shp, shp, shp, jax.ShapeDtypeStruct((PEER_HEADS, 1, N_TOK), F32)],
        scratch_shapes=[pltpu.VMEM((D_MODEL, PEER_DKEY), BF16)],
        compiler_params=_cparams("arbitrary", "arbitrary"),
        name="peer_prep",
    )(xn2, w_query, k1, k2)


I1_PER_TILE = PEER_TE // PEER_KEYS
I1_BLK = 8


def _peer_kernel(xt_ref, u_ref, vt_ref, s1_ref, e1_ref, s2_ref, e2_ref, tau_ref, o_ref, w_scr):
    e = pl.program_id(1)

    @pl.when(e == 0)
    def _():
        o_ref[...] = jnp.zeros_like(o_ref)

    st = jnp.dot(u_ref[...], xt_ref[...], preferred_element_type=F32)
    off = (e % (I1_BLK // I1_PER_TILE)) * I1_PER_TILE
    for r in range(I1_PER_TILE):
        rows = slice(r * PEER_KEYS, (r + 1) * PEER_KEYS)
        coef = jnp.zeros((PEER_KEYS, PEER_TM), F32)
        for h in range(PEER_HEADS):
            s = s1_ref[h, pl.ds(off + r, 1), :] + s2_ref[h]
            p = e1_ref[h, pl.ds(off + r, 1), :] * e2_ref[h]
            coef = coef + jnp.where(s >= tau_ref[h], p, 0.0)
        w_scr[rows, :] = (coef * _gelu(st[rows, :])).astype(BF16)
    o_ref[...] += jnp.dot(vt_ref[...], w_scr[...], preferred_element_type=F32)


def _peer(xn2t, u_bf, vt_bf, s1, s2, e1, e2, tau):
    nt = pl.cdiv(N_TOK, PEER_TM)
    ne = PEER_EXPERTS // PEER_TE
    per = I1_BLK // I1_PER_TILE
    small = pl.BlockSpec((PEER_HEADS, I1_BLK, PEER_TM), lambda t, e: (0, e // per, t))
    big = pl.BlockSpec((PEER_HEADS, PEER_KEYS, PEER_TM), lambda t, e: (0, 0, t))
    return pl.pallas_call(
        _peer_kernel,
        grid=(nt, ne),
        in_specs=[
            pl.BlockSpec((D_MODEL, PEER_TM), lambda t, e: (0, t)),
            pl.BlockSpec((PEER_TE, D_MODEL), lambda t, e: (e, 0)),
            pl.BlockSpec((D_MODEL, PEER_TE), lambda t, e: (0, e)),
            small, small, big, big,
            pl.BlockSpec((PEER_HEADS, 1, PEER_TM), lambda t, e: (0, 0, t)),
        ],
        out_specs=pl.BlockSpec((D_MODEL, PEER_TM), lambda t, e: (0, t)),
        out_shape=jax.ShapeDtypeStruct((D_MODEL, N_TOK), F32),
        scratch_shapes=[pltpu.VMEM((PEER_TE, PEER_TM), BF16)],
        compiler_params=_cparams("arbitrary", "arbitrary"),
        name="peer",
    )(xn2t, u_bf, vt_bf, s1, e1, s2, e2, tau)


def _final_kernel(x_ref, pt_ref, yp_ref, ys_ref):
    i = pl.program_id(0)
    y = x_ref[...] + pt_ref[...].T

    @pl.when(i < N_PROMPT_BLK)
    def _():
        yp_ref[...] = y

    @pl.when(i == N_PROMPT_BLK)
    def _():
        ys_ref[...] = y


def _final(x1, peer_t):
    return pl.pallas_call(
        _final_kernel,
        grid=(N_ROW_BLK,),
        in_specs=[
            pl.BlockSpec((ROW_BLK, D_MODEL), lambda i: (i, 0)),
            pl.BlockSpec((D_MODEL, ROW_BLK), lambda i: (0, i)),
        ],
        out_specs=[
            pl.BlockSpec((ROW_BLK, D_MODEL), lambda i: (jnp.minimum(i, N_PROMPT_BLK - 1), 0)),
            pl.BlockSpec((ROW_BLK, D_MODEL), lambda i: (0, 0)),
        ],
        out_shape=[
            jax.ShapeDtypeStruct((N_PROMPT, D_MODEL), F32),
            jax.ShapeDtypeStruct((DEC_BATCH, D_MODEL), F32),
        ],
        compiler_params=_cparams("arbitrary"),
        name="final",
    )(x1, peer_t)


def _bucket(dist):
    max_exact = N_BUCKETS // 2
    dd = jnp.maximum(dist, max_exact).astype(F32)
    large = max_exact + (jnp.log(dd / max_exact) / math.log(RP_MAX_DIST / max_exact)
                         * (N_BUCKETS - max_exact)).astype(jnp.int32)
    large = jnp.minimum(large, N_BUCKETS - 1)
    return jnp.where(dist < max_exact, dist, large)


def _group_bias(rel_bias, g, d):
    dist = d * jnp.arange(N_KEYS, dtype=jnp.int32)
    bias = rel_bias[:, g * HEADS_PER_GROUP:(g + 1) * HEADS_PER_GROUP][_bucket(dist)]
    slot = (np.arange(ROW_BLK)[:, None] + ROW_BLK) - np.arange(2 * ROW_BLK)[None, :]
    ok = (slot >= 0) & (slot < N_KEYS)
    band = jnp.where(ok[None], bias.T[:, np.clip(slot, 0, N_KEYS - 1)], NEG)
    return bias, band.astype(F32)


def kernel(x_prompt, x_sample, cache_k_w128, cache_v_w128, cache_k_w512, cache_v_w512, cache_k_w2048,
           cache_v_w2048, state_lru_h, state_conv, rel_bias, ln1_g, w_in, q_norm_g, k_norm_g, conv_w,
           conv_b, w_rg, b_rg, w_ig, b_ig, lru_L, w_pa, w_pb, w_o, ln2_g, w_query, sub_keys_1,
           sub_keys_2, peer_u, peer_v):
    caches = ((cache_k_w128, cache_v_w128), (cache_k_w512, cache_v_w512), (cache_k_w2048, cache_v_w2048))
    xp = x_prompt.reshape(N_PROMPT, D_MODEL)
    xs = x_sample.reshape(DEC_BATCH, D_MODEL)
    w = w_in[0]

    xn, x_all = _norm1(xp, xs, ln1_g)
    gspec = [pl.BlockSpec((1, HEAD_DIM), lambda j, i: (0, 0))]
    (qh,) = _proj(xn, w, Q_OFF, ATTN_WIDTH, _epi_q, (BF16,), (q_norm_g,), gspec)
    k32, kb = _proj(xn, w, K_OFF, ATTN_WIDTH, _epi_k, (F32, BF16), (k_norm_g,), gspec)
    v32, vb = _proj(xn, w, V_OFF, ATTN_WIDTH, _epi_v, (F32, BF16))
    (xg,) = _proj(xn, w, XB_OFF, 2 * LRU_WIDTH, _epi_f32, (F32,))
    (gates,) = _proj(xn, w, GA_OFF, 2 * D_MODEL, _epi_sigmoid, (BF16,))

    o_p, lse_p, o_s, lse_s = [], [], [], []
    for g, (window, d) in enumerate(ATTN_GROUPS):
        bias, band = _group_bias(rel_bias, g, d)
        o, lse = _attn_prompt(qh, kb, vb, band, g, d)
        o_p.append(o)
        lse_p.append(lse.reshape(BATCH, d, SEQ // d, HEADS_PER_GROUP).transpose(0, 2, 1, 3)
                     .reshape(N_PROMPT, HEADS_PER_GROUP))
        o, lse = _attn_sample(qh, k32, v32, caches[g][0][0], caches[g][1][0], bias, g, d)
        o_s.append(o)
        lse_s.append(lse)
    attn = _merge(o_p, lse_p, o_s, lse_s)

    rec_p, h_p = _rec_prompt(xg, conv_w[0], conv_b, w_rg[0], b_rg, w_ig[0], b_ig, lru_L)
    rec_s, h_s = _rec_sample(xg, state_conv[0], state_lru_h[0], conv_w[0], conv_b, w_rg[0], b_rg,
                             w_ig[0], b_ig, lru_L)
    rec = jnp.concatenate([rec_p, rec_s], axis=0)

    merged = _mix(attn, rec, gates, w_pa[0], w_pb[0])
    (x1,) = _proj(merged, w_o[0], 0, D_MODEL, _epi_residual, (F32,), (x_all,),
                  [pl.BlockSpec((MM_TM, MM_TN), lambda j, i: (i, j))])

    xn2, xn2t = _norm2(x1, ln2_g)
    s1, s2, e1, e2, tau = _peer_prep(xn2, w_query[0], sub_keys_1[0], sub_keys_2[0])
    peer_t = _peer(xn2t, peer_u[0].astype(BF16), peer_v[0].T.astype(BF16), s1, s2, e1, e2, tau)
    y_p, y_s = _final(x1, peer_t)

    outs = [y_p.reshape(BATCH, SEQ, D_MODEL), y_s.reshape(DEC_BATCH, 1, D_MODEL)]
    kp = k32[:N_PROMPT].reshape(BATCH, SEQ, N_GROUPS, HEADS_PER_GROUP, HEAD_DIM)
    vp = v32[:N_PROMPT].reshape(BATCH, SEQ, N_GROUPS, HEADS_PER_GROUP, HEAD_DIM)
    ks = k32[N_PROMPT:].reshape(DEC_BATCH, 1, N_GROUPS, HEADS_PER_GROUP, HEAD_DIM)
    vs = v32[N_PROMPT:].reshape(DEC_BATCH, 1, N_GROUPS, HEADS_PER_GROUP, HEAD_DIM)
    for g, (window, d) in enumerate(ATTN_GROUPS):
        keep = min(window, SEQ)
        outs += [kp[:, SEQ - keep:, g][None], vp[:, SEQ - keep:, g][None], ks[:, :, g][None], vs[:, :, g][None]]
    xb_p = xg[:N_PROMPT, :LRU_WIDTH].reshape(BATCH, SEQ, LRU_WIDTH)
    xb_s = xg[N_PROMPT:, :LRU_WIDTH]
    outs += [
        h_p[None], h_s[None],
        xb_p[:, SEQ - (CONV_WIDTH - 1):][None],
        jnp.concatenate([state_conv[0][:, 1:], xb_s[:, None, :]], axis=1)[None],
    ]
    return tuple(outs)
```

```python
import functools
import math

import jax
import jax.numpy as jnp
import numpy as np
from jax import lax
from jax.experimental import pallas as pl
from jax.experimental.pallas import tpu as pltpu

F32 = jnp.float32
BF16 = jnp.bfloat16

D_MODEL = 4096
BATCH = 4
SEQ = 2048
DEC_BATCH = 128
N_PROMPT = BATCH * SEQ
N_TOK = N_PROMPT + DEC_BATCH
HEAD_DIM = 128
HEADS_PER_GROUP = 8
GROUP_COLS = HEADS_PER_GROUP * HEAD_DIM
ATTN_GROUPS = ((128, 1), (512, 4), (2048, 16))
N_GROUPS = 3
ATTN_WIDTH = N_GROUPS * GROUP_COLS
N_KEYS = 129
N_BUCKETS = 32
RP_MAX_DIST = 2048
LRU_WIDTH = 2048
LRU_BLOCKS = 16
LRU_BLOCK = 128
CONV_WIDTH = 4
LRU_C = 8.0
PEER_HEADS = 8
PEER_KEYS = 128
PEER_EXPERTS = PEER_KEYS * PEER_KEYS
PEER_DKEY = 256
PEER_TOPK = 16
EPS = 1e-6

Q_OFF = 0
K_OFF = ATTN_WIDTH
V_OFF = 2 * ATTN_WIDTH
XB_OFF = 3 * ATTN_WIDTH
GA_OFF = XB_OFF + 2 * LRU_WIDTH

ROW_BLK = 128
N_PROMPT_BLK = N_PROMPT // ROW_BLK
N_ROW_BLK = N_TOK // ROW_BLK
MM_TM = 640
MM_TN = 512
PEER_TM = 512
PEER_TE = 512
NEG = -1e30
VMEM_LIMIT = 56 * 1024 * 1024


def _cparams(*sem):
    return pltpu.CompilerParams(dimension_semantics=sem, vmem_limit_bytes=VMEM_LIMIT)


def _gelu(x):
    return jax.nn.gelu(x)


def _sigmoid(x):
    return 1.0 / (1.0 + jnp.exp(-x))


def _norm1_kernel(xp_ref, xs_ref, g_ref, xn_ref, xall_ref):
    i = pl.program_id(0)

    def body(x):
        ms = jnp.mean(x * x, axis=-1, keepdims=True)
        xn_ref[...] = (x * lax.rsqrt(ms + EPS) * g_ref[...]).astype(BF16)
        xall_ref[...] = x

    @pl.when(i < N_PROMPT_BLK)
    def _():
        body(xp_ref[...])

    @pl.when(i == N_PROMPT_BLK)
    def _():
        body(xs_ref[...])


def _norm1(xp, xs, g):
    return pl.pallas_call(
        _norm1_kernel,
        grid=(N_ROW_BLK,),
        in_specs=[
            pl.BlockSpec((ROW_BLK, D_MODEL), lambda i: (jnp.minimum(i, N_PROMPT_BLK - 1), 0)),
            pl.BlockSpec((ROW_BLK, D_MODEL), lambda i: (0, 0)),
            pl.BlockSpec((1, D_MODEL), lambda i: (0, 0)),
        ],
        out_specs=[
            pl.BlockSpec((ROW_BLK, D_MODEL), lambda i: (i, 0)),
            pl.BlockSpec((ROW_BLK, D_MODEL), lambda i: (i, 0)),
        ],
        out_shape=[
            jax.ShapeDtypeStruct((N_TOK, D_MODEL), BF16),
            jax.ShapeDtypeStruct((N_TOK, D_MODEL), F32),
        ],
        compiler_params=_cparams("arbitrary"),
        name="norm1",
    )(xp, xs, g)


def _proj_kernel(*refs, n_extra, epilogue):
    x_ref, w_ref = refs[0], refs[1]
    extra = refs[2:2 + n_extra]
    outs = refs[2 + n_extra:-1]
    wbf_ref = refs[-1]

    @pl.when(pl.program_id(1) == 0)
    def _():
        wbf_ref[...] = w_ref[...].astype(BF16)

    acc = jnp.dot(x_ref[...], wbf_ref[...], preferred_element_type=F32)
    epilogue(acc, extra, outs)


def _proj(x, w, col0, ncols, epilogue, out_dtypes, extra=(), extra_specs=()):
    m, k = x.shape
    grid = (ncols // MM_TN, m // MM_TM)
    cb0 = col0 // MM_TN
    kern = functools.partial(_proj_kernel, n_extra=len(extra), epilogue=epilogue)
    return pl.pallas_call(
        kern,
        grid=grid,
        in_specs=[
            pl.BlockSpec((MM_TM, k), lambda j, i: (i, 0)),
            pl.BlockSpec((k, MM_TN), lambda j, i: (0, cb0 + j)),
            *extra_specs,
        ],
        out_specs=[pl.BlockSpec((MM_TM, MM_TN), lambda j, i: (i, j)) for _ in out_dtypes],
        out_shape=[jax.ShapeDtypeStruct((m, ncols), dt) for dt in out_dtypes],
        scratch_shapes=[pltpu.VMEM((k, MM_TN), BF16)],
        compiler_params=_cparams("arbitrary", "arbitrary"),
        name="proj",
    )(x, w, *extra)


def _head_norm(a, g):
    ms = jnp.mean(a * a, axis=-1, keepdims=True)
    return a * lax.rsqrt(ms + EPS) * g


def _epi_q(acc, extra, outs):
    g = extra[0][...]
    for h in range(MM_TN // HEAD_DIM):
        sl = slice(h * HEAD_DIM, (h + 1) * HEAD_DIM)
        outs[0][:, sl] = _head_norm(acc[:, sl], g) * (HEAD_DIM ** -0.5)


def _epi_k(acc, extra, outs):
    g = extra[0][...]
    for h in range(MM_TN // HEAD_DIM):
        sl = slice(h * HEAD_DIM, (h + 1) * HEAD_DIM)
        outs[0][:, sl] = _head_norm(acc[:, sl], g)


def _epi_f32(acc, extra, outs):
    outs[0][...] = acc


def _epi_sigmoid(acc, extra, outs):
    outs[0][...] = _sigmoid(acc).astype(BF16)


def _attn_p_kernel(*refs):
    q_refs, k_refs, v_refs, ext_refs = refs[0:3], refs[3:6], refs[6:9], refs[9:12]
    out_ref = refs[12]
    o_scr, l_scr = refs[13:16], refs[16:19]
    nt = (((1,), (1,)), ((), ()))

    for g, (_, d) in enumerate(ATTN_GROUPS):
        q_ref, k_ref, v_ref = q_refs[g], k_refs[g], v_refs[g]
        band = pltpu.roll(jnp.broadcast_to(ext_refs[g][0], (ROW_BLK, 2 * ROW_BLK)), 0, 1,
                          stride=1, stride_axis=0)
        band_cur = band[:, ROW_BLK:]
        nblk = SEQ // d // ROW_BLK

        def rows(r, i, d=d):
            start = r + (d * ROW_BLK) * i
            return pl.ds(start, ROW_BLK) if d == 1 else pl.ds(start, ROW_BLK, stride=d)

        def block(r, i, first, g=g, q_ref=q_ref, k_ref=k_ref, v_ref=v_ref, band=band,
                  band_cur=band_cur, rows=rows):
            cur = rows(r, i)
            q = q_ref[cur, :].astype(BF16)
            if first:
                kk = k_ref[cur, :].astype(BF16)
                vv = v_ref[cur, :].astype(BF16)
                bias = band_cur
            else:
                prev = rows(r, i - 1)
                kk = jnp.concatenate([k_ref[prev, :], k_ref[cur, :]], axis=0).astype(BF16)
                vv = jnp.concatenate([v_ref[prev, :], v_ref[cur, :]], axis=0).astype(BF16)
                bias = band
            lg = lax.dot_general(q, kk, nt, preferred_element_type=F32) + bias
            m = jnp.max(lg, axis=-1, keepdims=True)
            p = jnp.exp(lg - m)
            s = jnp.sum(p, axis=-1, keepdims=True)
            o_scr[g][cur, :] = jnp.dot(p.astype(BF16), vv, preferred_element_type=F32) / s
            l_scr[g][cur, :] = jnp.broadcast_to(m + jnp.log(s), (ROW_BLK, HEAD_DIM))

        def cls(r, carry, block=block, nblk=nblk):
            block(r, 0, True)
            if nblk > 1:
                def rest(i, c):
                    block(r, i, False)
                    return c
                lax.fori_loop(1, nblk, rest, 0)
            return carry

        if d == 1:
            cls(0, 0)
        else:
            lax.fori_loop(0, d, cls, 0)

    l0, l1, l2 = l_scr[0][...], l_scr[1][...], l_scr[2][...]
    m = jnp.maximum(jnp.maximum(l0, l1), l2)
    e0, e1, e2 = jnp.exp(l0 - m), jnp.exp(l1 - m), jnp.exp(l2 - m)
    acc = e0 * o_scr[0][...] + e1 * o_scr[1][...] + e2 * o_scr[2][...]
    out_ref[...] = (acc / (e0 + e1 + e2)).astype(BF16)


def _attn_prompt(q32, k32, v32, ext):
    def col(g):
        return pl.BlockSpec((SEQ, HEAD_DIM), lambda b, h: (b, g * HEADS_PER_GROUP + h))

    def erow(g):
        return pl.BlockSpec((1, 1, 2 * ROW_BLK), lambda b, h: (g * HEADS_PER_GROUP + h, 0, 0))

    groups = range(N_GROUPS)
    return pl.pallas_call(
        _attn_p_kernel,
        grid=(BATCH, HEADS_PER_GROUP),
        in_specs=[col(g) for g in groups] * 3 + [erow(g) for g in groups],
        out_specs=pl.BlockSpec((SEQ, HEAD_DIM), lambda b, h: (b, h)),
        out_shape=jax.ShapeDtypeStruct((N_PROMPT, GROUP_COLS), BF16),
        scratch_shapes=[pltpu.VMEM((SEQ, HEAD_DIM), F32) for _ in range(2 * N_GROUPS)],
        compiler_params=_cparams("arbitrary", "arbitrary"),
        name="attn_p",
    )(q32, q32, q32, k32, k32, k32, v32, v32, v32, ext, ext, ext)


SAMPLE_BB = 4


def _attn_s_kernel(*refs):
    q_ref, kn_ref, vn_ref = refs[0:3]
    ck_refs, cv_refs = refs[3:6], refs[6:9]
    bc_ref, b0_ref, o_ref = refs[9], refs[10], refs[11]

    def one(bi, carry):
        outs, lses = [], []
        for g in range(N_GROUPS):
            hs = slice(g * HEADS_PER_GROUP, (g + 1) * HEADS_PER_GROUP)
            q = q_ref[bi, hs, :]
            lc = jnp.sum(ck_refs[g][bi] * q[None], axis=-1, keepdims=True) + bc_ref[g]
            ln = jnp.sum(kn_ref[bi, hs, :] * q, axis=-1, keepdims=True) + b0_ref[g]
            m = jnp.maximum(jnp.max(lc, axis=0), ln)
            p = jnp.exp(lc - m[None])
            pn = jnp.exp(ln - m)
            s = jnp.sum(p, axis=0) + pn
            outs.append((jnp.sum(p * cv_refs[g][bi], axis=0) + pn * vn_ref[bi, hs, :]) / s)
            lses.append(m + jnp.log(s))
        m = jnp.maximum(jnp.maximum(lses[0], lses[1]), lses[2])
        e = [jnp.exp(l - m) for l in lses]
        acc = e[0] * outs[0] + e[1] * outs[1] + e[2] * outs[2]
        o_ref[bi] = acc / (e[0] + e[1] + e[2])
        return carry

    lax.fori_loop(0, SAMPLE_BB, one, 0)


def _attn_sample(q32, k32, v32, caches, biases):
    heads3 = lambda a: a[N_PROMPT:].reshape(DEC_BATCH, N_GROUPS * HEADS_PER_GROUP, HEAD_DIM)
    new = pl.BlockSpec((SAMPLE_BB, N_GROUPS * HEADS_PER_GROUP, HEAD_DIM), lambda s: (s, 0, 0))
    bc = jnp.stack([b[::-1][:N_KEYS - 1] for b in biases])[..., None]
    b0 = jnp.stack([b[0] for b in biases])[..., None]
    cache_specs, cache_args = [], []
    for which in range(2):
        for g, (window, d) in enumerate(ATTN_GROUPS):
            c = caches[g][which]
            cache_args.append(c.reshape(DEC_BATCH, c.shape[1] // d, d, HEADS_PER_GROUP, HEAD_DIM))
            cache_specs.append(pl.BlockSpec((SAMPLE_BB, N_KEYS - 1, None, HEADS_PER_GROUP, HEAD_DIM),
                                            lambda s: (s, 0, 0, 0, 0)))
    o = pl.pallas_call(
        _attn_s_kernel,
        grid=(DEC_BATCH // SAMPLE_BB,),
        in_specs=[
            new, new, new, *cache_specs,
            pl.BlockSpec((N_GROUPS, N_KEYS - 1, HEADS_PER_GROUP, 1), lambda s: (0, 0, 0, 0)),
            pl.BlockSpec((N_GROUPS, HEADS_PER_GROUP, 1), lambda s: (0, 0, 0)),
        ],
        out_specs=pl.BlockSpec((SAMPLE_BB, HEADS_PER_GROUP, HEAD_DIM), lambda s: (s, 0, 0)),
        out_shape=jax.ShapeDtypeStruct((DEC_BATCH, HEADS_PER_GROUP, HEAD_DIM), F32),
        compiler_params=_cparams("arbitrary"),
        name="attn_s",
    )(heads3(q32), heads3(k32), heads3(v32), *cache_args, bc, b0)
    return o.reshape(DEC_BATCH, GROUP_COLS).astype(BF16)


SCAN_CHUNK = 64


def _softplus(x):
    return jnp.maximum(x, 0.0) + jnp.log1p(jnp.exp(-jnp.abs(x)))


def _gates(conv, wr_ref, br_ref, wi_ref, bi_ref, l_ref):
    cb = conv.astype(BF16)
    r = _sigmoid(jnp.dot(cb, wr_ref[0].astype(BF16), preferred_element_type=F32) + br_ref[...])
    ig = _sigmoid(jnp.dot(cb, wi_ref[0].astype(BF16), preferred_element_type=F32) + bi_ref[...])
    log_a = -LRU_C * r * _softplus(-l_ref[...])
    return log_a, ig


def _decay(log_a):
    a = jnp.exp(log_a)
    return a, jnp.sqrt(-jnp.tanh(log_a) * (a * a + 1.0))


def _rec_p_kernel(xb_ref, gb_ref, cw_ref, cb_ref, wr_ref, br_ref, wi_ref, bi_ref, l_ref,
                  y_ref, h_ref, a_scr, b_scr):
    t = SEQ
    x = xb_ref[...]
    row = lax.broadcasted_iota(jnp.int32, (t, LRU_BLOCK), 0)
    conv = cb_ref[...] + x * cw_ref[CONV_WIDTH - 1:CONV_WIDTH, :]
    for s in range(1, CONV_WIDTH):
        xs = jnp.where(row >= s, pltpu.roll(x, s, 0), 0.0)
        conv = conv + xs * cw_ref[CONV_WIDTH - 1 - s:CONV_WIDTH - s, :]
    log_a, ig = _gates(conv, wr_ref, br_ref, wi_ref, bi_ref, l_ref)
    a, mult = _decay(log_a)
    start = row == 0
    a_scr[...] = jnp.where(start, 0.0, a)
    b_scr[...] = jnp.where(start, 1.0, mult) * ig * conv

    crow = lax.broadcasted_iota(jnp.int32, (SCAN_CHUNK, LRU_BLOCK), 0)

    def chunk(c, h_prev):
        rows = pl.ds(pl.multiple_of(c * SCAN_CHUNK, SCAN_CHUNK), SCAN_CHUNK)
        a_c = a_scr[rows, :]
        b_c = b_scr[rows, :]
        s = 1
        while s < SCAN_CHUNK:
            keep = crow >= s
            a_sh = jnp.where(keep, pltpu.roll(a_c, s, 0), 1.0)
            b_sh = jnp.where(keep, pltpu.roll(b_c, s, 0), 0.0)
            b_c = a_c * b_sh + b_c
            a_c = a_c * a_sh
            s *= 2
        h = a_c * h_prev + b_c
        y_ref[rows, :] = (h * _gelu(gb_ref[rows, :])).astype(BF16)
        return h[SCAN_CHUNK - 1:SCAN_CHUNK, :]

    h_last = lax.fori_loop(0, t // SCAN_CHUNK, chunk, jnp.zeros((1, LRU_BLOCK), F32))
    h_ref[0] = h_last


def _rec_prompt(xg, conv_w, conv_b, w_rg, b_rg, w_ig, b_ig, lru_l):
    nb = LRU_BLOCKS
    vec = pl.BlockSpec((1, LRU_BLOCK), lambda b, n: (0, n))
    wblk = pl.BlockSpec((1, LRU_BLOCK, LRU_BLOCK), lambda b, n: (n, 0, 0))
    y, h = pl.pallas_call(
        _rec_p_kernel,
        grid=(BATCH, nb),
        in_specs=[
            pl.BlockSpec((SEQ, LRU_BLOCK), lambda b, n: (b, n)),
            pl.BlockSpec((SEQ, LRU_BLOCK), lambda b, n: (b, nb + n)),
            pl.BlockSpec((CONV_WIDTH, LRU_BLOCK), lambda b, n: (0, n)),
            vec, wblk, vec, wblk, vec, vec,
        ],
        out_specs=[
            pl.BlockSpec((SEQ, LRU_BLOCK), lambda b, n: (b, n)),
            pl.BlockSpec((1, 1, LRU_BLOCK), lambda b, n: (b, 0, n)),
        ],
        out_shape=[
            jax.ShapeDtypeStruct((N_PROMPT, LRU_WIDTH), BF16),
            jax.ShapeDtypeStruct((BATCH, 1, LRU_WIDTH), F32),
        ],
        scratch_shapes=[pltpu.VMEM((SEQ, LRU_BLOCK), F32), pltpu.VMEM((SEQ, LRU_BLOCK), F32)],
        compiler_params=_cparams("arbitrary", "arbitrary"),
        name="rec_p",
    )(xg, xg, conv_w, conv_b, w_rg, b_rg, w_ig, b_ig, lru_l)
    return y, h.reshape(BATCH, LRU_WIDTH)


def _rec_s_kernel(xb_ref, gb_ref, c0_ref, c1_ref, c2_ref, h0_ref, cw_ref, cb_ref,
                  wr_ref, br_ref, wi_ref, bi_ref, l_ref, y_ref, h_ref):
    conv = (cb_ref[...] + c0_ref[...] * cw_ref[0:1, :] + c1_ref[...] * cw_ref[1:2, :]
            + c2_ref[...] * cw_ref[2:3, :] + xb_ref[...] * cw_ref[3:4, :])
    log_a, ig = _gates(conv, wr_ref, br_ref, wi_ref, bi_ref, l_ref)
    a, mult = _decay(log_a)
    h = a * h0_ref[...] + mult * ig * conv
    h_ref[...] = h
    y_ref[...] = (h * _gelu(gb_ref[...])).astype(BF16)


def _rec_sample(xg, state_conv, h0, conv_w, conv_b, w_rg, b_rg, w_ig, b_ig, lru_l):
    nb = LRU_BLOCKS
    sc = state_conv.reshape(DEC_BATCH, (CONV_WIDTH - 1) * LRU_WIDTH)
    srow = N_PROMPT // DEC_BATCH
    vec = pl.BlockSpec((1, LRU_BLOCK), lambda n: (0, n))
    wblk = pl.BlockSpec((1, LRU_BLOCK, LRU_BLOCK), lambda n: (n, 0, 0))
    tile = lambda f: pl.BlockSpec((DEC_BATCH, LRU_BLOCK), f)
    return pl.pallas_call(
        _rec_s_kernel,
        grid=(nb,),
        in_specs=[
            tile(lambda n: (srow, n)),
            tile(lambda n: (srow, nb + n)),
            tile(lambda n: (0, n)),
            tile(lambda n: (0, nb + n)),
            tile(lambda n: (0, 2 * nb + n)),
            tile(lambda n: (0, n)),
            pl.BlockSpec((CONV_WIDTH, LRU_BLOCK), lambda n: (0, n)),
            vec, wblk, vec, wblk, vec, vec,
        ],
        out_specs=[tile(lambda n: (0, n)), tile(lambda n: (0, n))],
        out_shape=[
            jax.ShapeDtypeStruct((DEC_BATCH, LRU_WIDTH), BF16),
            jax.ShapeDtypeStruct((DEC_BATCH, LRU_WIDTH), F32),
        ],
        compiler_params=_cparams("arbitrary"),
        name="rec_s",
    )(xg, xg, sc, sc, sc, h0, conv_w, conv_b, w_rg, b_rg, w_ig, b_ig, lru_l)


def _mix_kernel(at_ref, rc_ref, ga_ref, gr_ref, wa_ref, wb_ref, o_ref, wa_bf, wb_bf):
    @pl.when(pl.program_id(1) == 0)
    def _():
        wa_bf[...] = wa_ref[...].astype(BF16)
        wb_bf[...] = wb_ref[...].astype(BF16)

    pa = jnp.dot(at_ref[...], wa_bf[...], preferred_element_type=F32)
    pb = jnp.dot(rc_ref[...], wb_bf[...], preferred_element_type=F32)
    o_ref[...] = (ga_ref[...].astype(F32) * pa + gr_ref[...].astype(F32) * pb).astype(BF16)


def _mix(attn, rec, gates, w_pa, w_pb):
    nj = D_MODEL // MM_TN
    return pl.pallas_call(
        _mix_kernel,
        grid=(nj, N_TOK // MM_TM),
        in_specs=[
            pl.BlockSpec((MM_TM, GROUP_COLS), lambda j, i: (i, 0)),
            pl.BlockSpec((MM_TM, LRU_WIDTH), lambda j, i: (i, 0)),
            pl.BlockSpec((MM_TM, MM_TN), lambda j, i: (i, j)),
            pl.BlockSpec((MM_TM, MM_TN), lambda j, i: (i, nj + j)),
            pl.BlockSpec((GROUP_COLS, MM_TN), lambda j, i: (0, j)),
            pl.BlockSpec((LRU_WIDTH, MM_TN), lambda j, i: (0, j)),
        ],
        out_specs=pl.BlockSpec((MM_TM, MM_TN), lambda j, i: (i, j)),
        out_shape=jax.ShapeDtypeStruct((N_TOK, D_MODEL), BF16),
        scratch_shapes=[pltpu.VMEM((GROUP_COLS, MM_TN), BF16), pltpu.VMEM((LRU_WIDTH, MM_TN), BF16)],
        compiler_params=_cparams("arbitrary", "arbitrary"),
        name="mix",
    )(attn, rec, gates, gates, w_pa, w_pb)


def _epi_residual(acc, extra, outs):
    outs[0][...] = extra[0][...] + acc


def _norm2_kernel(x_ref, g_ref, xn_ref, xnt_ref):
    x = x_ref[...]
    ms = jnp.mean(x * x, axis=-1, keepdims=True)
    y = x * lax.rsqrt(ms + EPS) * g_ref[...]
    xn_ref[...] = y.astype(BF16)
    xnt_ref[...] = y.T.astype(BF16)


def _norm2(x1, g):
    return pl.pallas_call(
        _norm2_kernel,
        grid=(N_ROW_BLK,),
        in_specs=[
            pl.BlockSpec((ROW_BLK, D_MODEL), lambda i: (i, 0)),
            pl.BlockSpec((1, D_MODEL), lambda i: (0, 0)),
        ],
        out_specs=[
            pl.BlockSpec((ROW_BLK, D_MODEL), lambda i: (i, 0)),
            pl.BlockSpec((D_MODEL, ROW_BLK), lambda i: (0, i)),
        ],
        out_shape=[
            jax.ShapeDtypeStruct((N_TOK, D_MODEL), BF16),
            jax.ShapeDtypeStruct((D_MODEL, N_TOK), BF16),
        ],
        compiler_params=_cparams("arbitrary"),
        name="norm2",
    )(x1, g)


def _top_values(x, k):
    rows = lax.broadcasted_iota(jnp.int32, x.shape, 0)
    out = []
    for _ in range(k):
        m = jnp.max(x, axis=0, keepdims=True)
        first = jnp.min(jnp.where(x == m, rows, x.shape[0]), axis=0, keepdims=True)
        out.append(m)
        x = jnp.where(rows == first, -jnp.inf, x)
    return out


def _peer_prep_kernel(x_ref, wq_ref, k1_ref, k2_ref, s1_ref, s2_ref, e1_ref, e2_ref, tau_ref, wq_bf):
    @pl.when(pl.program_id(1) == 0)
    def _():
        wq_bf[...] = wq_ref[...].astype(BF16)

    half = PEER_DKEY // 2
    q = jnp.dot(x_ref[...], wq_bf[...], preferred_element_type=F32).astype(BF16)
    nt = (((1,), (1,)), ((), ()))
    s1 = lax.dot_general(k1_ref[0].astype(BF16), q[:, :half], nt, preferred_element_type=F32)
    s2 = lax.dot_general(k2_ref[0].astype(BF16), q[:, half:], nt, preferred_element_type=F32)
    a = _top_values(s1, PEER_TOPK)
    b = _top_values(s2, PEER_TOPK)
    cand = jnp.concatenate(
        [a[i] + b[j] for i in range(PEER_TOPK) for j in range(PEER_TOPK) if (i + 1) * (j + 1) <= PEER_TOPK],
        axis=0)
    c = _top_values(cand, PEER_TOPK)
    z = jnp.ones_like(c[0])
    for v in c[1:]:
        z = z + jnp.exp(v - c[0])
    s1_ref[0] = s1
    s2_ref[0] = s2
    e1_ref[0] = jnp.exp(s1 - a[0]) / z
    e2_ref[0] = jnp.exp(s2 - b[0])
    tau_ref[0] = c[PEER_TOPK - 1]


def _peer_prep(xn2, w_query, k1, k2):
    tm = MM_TM
    half = PEER_DKEY // 2
    big = pl.BlockSpec((1, PEER_KEYS, tm), lambda h, i: (h, 0, i))
    keys = pl.BlockSpec((1, PEER_KEYS, half), lambda h, i: (h, 0, 0))
    shp = jax.ShapeDtypeStruct((PEER_HEADS, PEER_KEYS, N_TOK), F32)
    return pl.pallas_call(
        _peer_prep_kernel,
        grid=(PEER_HEADS, N_TOK // tm),
        in_specs=[
            pl.BlockSpec((tm, D_MODEL), lambda h, i: (i, 0)),
            pl.BlockSpec((D_MODEL, PEER_DKEY), lambda h, i: (0, h)),
            keys, keys,
        ],
        out_specs=[big, big, big, big, pl.BlockSpec((1, 1, tm), lambda h, i: (h, 0, i))],
        out_shape=[shp, shp, shp, shp, jax.ShapeDtypeStruct((PEER_HEADS, 1, N_TOK), F32)],
        scratch_shapes=[pltpu.VMEM((D_MODEL, PEER_DKEY), BF16)],
        compiler_params=_cparams("arbitrary", "arbitrary"),
        name="peer_prep",
    )(xn2, w_query, k1, k2)


I1_PER_TILE = PEER_TE // PEER_KEYS
I1_BLK = 8


def _peer_kernel(xt_ref, u_ref, vt_ref, s1_ref, e1_ref, s2_ref, e2_ref, tau_ref, o_ref, w_scr):
    e = pl.program_id(1)

    @pl.when(e == 0)
    def _():
        o_ref[...] = jnp.zeros_like(o_ref)

    st = jnp.dot(u_ref[...], xt_ref[...], preferred_element_type=F32)
    off = (e % (I1_BLK // I1_PER_TILE)) * I1_PER_TILE
    for r in range(I1_PER_TILE):
        rows = slice(r * PEER_KEYS, (r + 1) * PEER_KEYS)
        coef = jnp.zeros((PEER_KEYS, PEER_TM), F32)
        for h in range(PEER_HEADS):
            s = s1_ref[h, pl.ds(off + r, 1), :] + s2_ref[h]
            p = e1_ref[h, pl.ds(off + r, 1), :] * e2_ref[h]
            coef = coef + jnp.where(s >= tau_ref[h], p, 0.0)
        w_scr[rows, :] = (coef * _gelu(st[rows, :])).astype(BF16)
    o_ref[...] += jnp.dot(vt_ref[...], w_scr[...], preferred_element_type=F32)


def _peer(xn2t, u_bf, vt_bf, s1, s2, e1, e2, tau):
    nt = pl.cdiv(N_TOK, PEER_TM)
    ne = PEER_EXPERTS // PEER_TE
    per = I1_BLK // I1_PER_TILE
    small = pl.BlockSpec((PEER_HEADS, I1_BLK, PEER_TM), lambda t, e: (0, e // per, t))
    big = pl.BlockSpec((PEER_HEADS, PEER_KEYS, PEER_TM), lambda t, e: (0, 0, t))
    return pl.pallas_call(
        _peer_kernel,
        grid=(nt, ne),
        in_specs=[
            pl.BlockSpec((D_MODEL, PEER_TM), lambda t, e: (0, t)),
            pl.BlockSpec((PEER_TE, D_MODEL), lambda t, e: (e, 0)),
            pl.BlockSpec((D_MODEL, PEER_TE), lambda t, e: (0, e)),
            small, small, big, big,
            pl.BlockSpec((PEER_HEADS, 1, PEER_TM), lambda t, e: (0, 0, t)),
        ],
        out_specs=pl.BlockSpec((D_MODEL, PEER_TM), lambda t, e: (0, t)),
        out_shape=jax.ShapeDtypeStruct((D_MODEL, N_TOK), F32),
        scratch_shapes=[pltpu.VMEM((PEER_TE, PEER_TM), BF16)],
        compiler_params=_cparams("arbitrary", "arbitrary"),
        name="peer",
    )(xn2t, u_bf, vt_bf, s1, e1, s2, e2, tau)


def _final_kernel(x_ref, pt_ref, yp_ref, ys_ref):
    i = pl.program_id(0)
    y = x_ref[...] + pt_ref[...].T

    @pl.when(i < N_PROMPT_BLK)
    def _():
        yp_ref[...] = y

    @pl.when(i == N_PROMPT_BLK)
    def _():
        ys_ref[...] = y


def _final(x1, peer_t):
    return pl.pallas_call(
        _final_kernel,
        grid=(N_ROW_BLK,),
        in_specs=[
            pl.BlockSpec((ROW_BLK, D_MODEL), lambda i: (i, 0)),
            pl.BlockSpec((D_MODEL, ROW_BLK), lambda i: (0, i)),
        ],
        out_specs=[
            pl.BlockSpec((ROW_BLK, D_MODEL), lambda i: (jnp.minimum(i, N_PROMPT_BLK - 1), 0)),
            pl.BlockSpec((ROW_BLK, D_MODEL), lambda i: (0, 0)),
        ],
        out_shape=[
            jax.ShapeDtypeStruct((N_PROMPT, D_MODEL), F32),
            jax.ShapeDtypeStruct((DEC_BATCH, D_MODEL), F32),
        ],
        compiler_params=_cparams("arbitrary"),
        name="final",
    )(x1, peer_t)


def _bucket(dist):
    max_exact = N_BUCKETS // 2
    dd = jnp.maximum(dist, max_exact).astype(F32)
    large = max_exact + (jnp.log(dd / max_exact) / math.log(RP_MAX_DIST / max_exact)
                         * (N_BUCKETS - max_exact)).astype(jnp.int32)
    large = jnp.minimum(large, N_BUCKETS - 1)
    return jnp.where(dist < max_exact, dist, large)


def _group_bias(rel_bias, g, d):
    dist = d * jnp.arange(N_KEYS, dtype=jnp.int32)
    bias = rel_bias[:, g * HEADS_PER_GROUP:(g + 1) * HEADS_PER_GROUP][_bucket(dist)]
    pad = jnp.full((HEADS_PER_GROUP, 2 * ROW_BLK - N_KEYS), NEG, F32)
    ext = jnp.concatenate([bias[::-1].T, pad], axis=1)
    return bias, ext[:, None, :]


def kernel(x_prompt, x_sample, cache_k_w128, cache_v_w128, cache_k_w512, cache_v_w512, cache_k_w2048,
           cache_v_w2048, state_lru_h, state_conv, rel_bias, ln1_g, w_in, q_norm_g, k_norm_g, conv_w,
           conv_b, w_rg, b_rg, w_ig, b_ig, lru_L, w_pa, w_pb, w_o, ln2_g, w_query, sub_keys_1,
           sub_keys_2, peer_u, peer_v):
    caches = ((cache_k_w128, cache_v_w128), (cache_k_w512, cache_v_w512), (cache_k_w2048, cache_v_w2048))
    xp = x_prompt.reshape(N_PROMPT, D_MODEL)
    xs = x_sample.reshape(DEC_BATCH, D_MODEL)
    w = w_in[0]

    xn, x_all = _norm1(xp, xs, ln1_g)
    gspec = [pl.BlockSpec((1, HEAD_DIM), lambda j, i: (0, 0))]
    (q32,) = _proj(xn, w, Q_OFF, ATTN_WIDTH, _epi_q, (F32,), (q_norm_g,), gspec)
    (k32,) = _proj(xn, w, K_OFF, ATTN_WIDTH, _epi_k, (F32,), (k_norm_g,), gspec)
    (v32,) = _proj(xn, w, V_OFF, ATTN_WIDTH, _epi_f32, (F32,))
    (xg,) = _proj(xn, w, XB_OFF, 2 * LRU_WIDTH, _epi_f32, (F32,))
    (gates,) = _proj(xn, w, GA_OFF, 2 * D_MODEL, _epi_sigmoid, (BF16,))

    biases, exts = zip(*[_group_bias(rel_bias, g, d) for g, (_, d) in enumerate(ATTN_GROUPS)])
    attn_p = _attn_prompt(q32, k32, v32, jnp.concatenate(exts, axis=0))
    attn_s = _attn_sample(q32, k32, v32, [(ck[0], cv[0]) for ck, cv in caches], biases)
    attn = jnp.concatenate([attn_p, attn_s], axis=0)

    rec_p, h_p = _rec_prompt(xg, conv_w[0], conv_b, w_rg[0], b_rg, w_ig[0], b_ig, lru_L)
    rec_s, h_s = _rec_sample(xg, state_conv[0], state_lru_h[0], conv_w[0], conv_b, w_rg[0], b_rg,
                             w_ig[0], b_ig, lru_L)
    rec = jnp.concatenate([rec_p, rec_s], axis=0)

    merged = _mix(attn, rec, gates, w_pa[0], w_pb[0])
    (x1,) = _proj(merged, w_o[0], 0, D_MODEL, _epi_residual, (F32,), (x_all,),
                  [pl.BlockSpec((MM_TM, MM_TN), lambda j, i: (i, j))])

    xn2, xn2t = _norm2(x1, ln2_g)
    s1, s2, e1, e2, tau = _peer_prep(xn2, w_query[0], sub_keys_1[0], sub_keys_2[0])
    peer_t = _peer(xn2t, peer_u[0].astype(BF16), peer_v[0].T.astype(BF16), s1, s2, e1, e2, tau)
    y_p, y_s = _final(x1, peer_t)

    outs = [y_p.reshape(BATCH, SEQ, D_MODEL), y_s.reshape(DEC_BATCH, 1, D_MODEL)]
    for g, (window, d) in enumerate(ATTN_GROUPS):
        keep = min(window, SEQ)
        cols = slice(g * GROUP_COLS, (g + 1) * GROUP_COLS)
        for a in (k32, v32):
            tail = a[:N_PROMPT].reshape(BATCH, SEQ, ATTN_WIDTH)[:, SEQ - keep:, cols]
            outs.append(tail.reshape(1, BATCH, keep, HEADS_PER_GROUP, HEAD_DIM))
        for a in (k32, v32):
            outs.append(a[N_PROMPT:, cols].reshape(1, DEC_BATCH, 1, HEADS_PER_GROUP, HEAD_DIM))
    xb_p = xg[:N_PROMPT, :LRU_WIDTH].reshape(BATCH, SEQ, LRU_WIDTH)
    xb_s = xg[N_PROMPT:, :LRU_WIDTH]
    outs += [
        h_p[None], h_s[None],
        xb_p[:, SEQ - (CONV_WIDTH - 1):][None],
        jnp.concatenate([state_conv[0][:, 1:], xb_s[:, None, :]], axis=1)[None],
    ]
    return tuple(outs)
```

```python
import functools
import math

import jax
import jax.numpy as jnp
import numpy as np
from jax import lax
from jax.experimental import pallas as pl
from jax.experimental.pallas import tpu as pltpu

F32 = jnp.float32
BF16 = jnp.bfloat16

D_MODEL = 4096
BATCH = 4
SEQ = 2048
DEC_BATCH = 128
N_PROMPT = BATCH * SEQ
N_TOK = N_PROMPT + DEC_BATCH
HEAD_DIM = 128
HEADS_PER_GROUP = 8
GROUP_COLS = HEADS_PER_GROUP * HEAD_DIM
ATTN_GROUPS = ((128, 1), (512, 4), (2048, 16))
N_GROUPS = 3
ATTN_WIDTH = N_GROUPS * GROUP_COLS
N_KEYS = 129
N_BUCKETS = 32
RP_MAX_DIST = 2048
LRU_WIDTH = 2048
LRU_BLOCKS = 16
LRU_BLOCK = 128
CONV_WIDTH = 4
LRU_C = 8.0
PEER_HEADS = 8
PEER_KEYS = 128
PEER_EXPERTS = PEER_KEYS * PEER_KEYS
PEER_DKEY = 256
PEER_TOPK = 16
EPS = 1e-6

Q_OFF = 0
K_OFF = ATTN_WIDTH
V_OFF = 2 * ATTN_WIDTH
XB_OFF = 3 * ATTN_WIDTH
GA_OFF = XB_OFF + 2 * LRU_WIDTH

ROW_BLK = 128
N_PROMPT_BLK = N_PROMPT // ROW_BLK
N_ROW_BLK = N_TOK // ROW_BLK
MM_TM = 1040
MM_TN = 512
PEER_TM = 512
PEER_TE = 512
NEG = -1e30
VMEM_LIMIT = 56 * 1024 * 1024


def _cparams(*sem):
    return pltpu.CompilerParams(dimension_semantics=sem, vmem_limit_bytes=VMEM_LIMIT)


def _gelu(x):
    return jax.nn.gelu(x)


def _sigmoid(x):
    return 1.0 / (1.0 + jnp.exp(-x))


def _norm1_kernel(xp_ref, xs_ref, g_ref, xn_ref, xall_ref):
    i = pl.program_id(0)

    def body(x):
        ms = jnp.mean(x * x, axis=-1, keepdims=True)
        xn_ref[...] = (x * lax.rsqrt(ms + EPS) * g_ref[...]).astype(BF16)
        xall_ref[...] = x

    @pl.when(i < N_PROMPT_BLK)
    def _():
        body(xp_ref[...])

    @pl.when(i == N_PROMPT_BLK)
    def _():
        body(xs_ref[...])


def _norm1(xp, xs, g):
    return pl.pallas_call(
        _norm1_kernel,
        grid=(N_ROW_BLK,),
        in_specs=[
            pl.BlockSpec((ROW_BLK, D_MODEL), lambda i: (jnp.minimum(i, N_PROMPT_BLK - 1), 0)),
            pl.BlockSpec((ROW_BLK, D_MODEL), lambda i: (0, 0)),
            pl.BlockSpec((1, D_MODEL), lambda i: (0, 0)),
        ],
        out_specs=[
            pl.BlockSpec((ROW_BLK, D_MODEL), lambda i: (i, 0)),
            pl.BlockSpec((ROW_BLK, D_MODEL), lambda i: (i, 0)),
        ],
        out_shape=[
            jax.ShapeDtypeStruct((N_TOK, D_MODEL), BF16),
            jax.ShapeDtypeStruct((N_TOK, D_MODEL), F32),
        ],
        compiler_params=_cparams("arbitrary"),
        name="norm1",
    )(xp, xs, g)


def _proj_kernel(*refs, n_extra, epilogue):
    x_ref, w_ref = refs[0], refs[1]
    extra = refs[2:2 + n_extra]
    outs = refs[2 + n_extra:-1]
    wbf_ref = refs[-1]

    @pl.when(pl.program_id(1) == 0)
    def _():
        wbf_ref[...] = w_ref[...].astype(BF16)

    acc = jnp.dot(x_ref[...], wbf_ref[...], preferred_element_type=F32)
    epilogue(acc, extra, outs)


def _proj(x, w, col0, ncols, epilogue, out_dtypes, extra=(), extra_specs=()):
    m, k = x.shape
    grid = (ncols // MM_TN, m // MM_TM)
    cb0 = col0 // MM_TN
    kern = functools.partial(_proj_kernel, n_extra=len(extra), epilogue=epilogue)
    return pl.pallas_call(
        kern,
        grid=grid,
        in_specs=[
            pl.BlockSpec((MM_TM, k), lambda j, i: (i, 0)),
            pl.BlockSpec((k, MM_TN), lambda j, i: (0, cb0 + j)),
            *extra_specs,
        ],
        out_specs=[pl.BlockSpec((MM_TM, MM_TN), lambda j, i: (i, j)) for _ in out_dtypes],
        out_shape=[jax.ShapeDtypeStruct((m, ncols), dt) for dt in out_dtypes],
        scratch_shapes=[pltpu.VMEM((k, MM_TN), BF16)],
        compiler_params=_cparams("arbitrary", "arbitrary"),
        name="proj",
    )(x, w, *extra)


def _head_norm(a, g):
    ms = jnp.mean(a * a, axis=-1, keepdims=True)
    return a * lax.rsqrt(ms + EPS) * g


def _epi_q(acc, extra, outs):
    g = extra[0][...]
    for h in range(MM_TN // HEAD_DIM):
        sl = slice(h * HEAD_DIM, (h + 1) * HEAD_DIM)
        outs[0][:, sl] = _head_norm(acc[:, sl], g) * (HEAD_DIM ** -0.5)


def _epi_k(acc, extra, outs):
    g = extra[0][...]
    for h in range(MM_TN // HEAD_DIM):
        sl = slice(h * HEAD_DIM, (h + 1) * HEAD_DIM)
        outs[0][:, sl] = _head_norm(acc[:, sl], g)


def _epi_f32(acc, extra, outs):
    outs[0][...] = acc


def _epi_sigmoid(acc, extra, outs):
    outs[0][...] = _sigmoid(acc).astype(BF16)


def _attn_p_kernel(*refs):
    q_refs, k_refs, v_refs, ext_refs = refs[0:3], refs[3:6], refs[6:9], refs[9:12]
    out_ref = refs[12]
    o_scr, l_scr = refs[13:16], refs[16:19]
    nt = (((1,), (1,)), ((), ()))

    for g, (_, d) in enumerate(ATTN_GROUPS):
        q_ref, k_ref, v_ref = q_refs[g], k_refs[g], v_refs[g]
        band = pltpu.roll(jnp.broadcast_to(ext_refs[g][0], (ROW_BLK, 2 * ROW_BLK)), 0, 1,
                          stride=1, stride_axis=0)
        band_cur = band[:, ROW_BLK:]
        nblk = SEQ // d // ROW_BLK

        def rows(r, i, d=d):
            start = r + (d * ROW_BLK) * i
            return pl.ds(start, ROW_BLK) if d == 1 else pl.ds(start, ROW_BLK, stride=d)

        def block(r, i, first, g=g, q_ref=q_ref, k_ref=k_ref, v_ref=v_ref, band=band,
                  band_cur=band_cur, rows=rows):
            cur = rows(r, i)
            q = q_ref[cur, :].astype(BF16)
            if first:
                kk = k_ref[cur, :].astype(BF16)
                vv = v_ref[cur, :].astype(BF16)
                bias = band_cur
            else:
                prev = rows(r, i - 1)
                kk = jnp.concatenate([k_ref[prev, :], k_ref[cur, :]], axis=0).astype(BF16)
                vv = jnp.concatenate([v_ref[prev, :], v_ref[cur, :]], axis=0).astype(BF16)
                bias = band
            lg = lax.dot_general(q, kk, nt, preferred_element_type=F32) + bias
            m = jnp.max(lg, axis=-1, keepdims=True)
            p = jnp.exp(lg - m)
            s = jnp.sum(p, axis=-1, keepdims=True)
            o_scr[g][cur, :] = jnp.dot(p.astype(BF16), vv, preferred_element_type=F32) / s
            l_scr[g][cur, :] = jnp.broadcast_to(m + jnp.log(s), (ROW_BLK, HEAD_DIM))

        if d == 1:
            block(0, 0, True)
            per_trip = 3

            def trip(t, carry, block=block):
                for u in range(per_trip):
                    block(0, 1 + per_trip * t + u, False)
                return carry

            lax.fori_loop(0, (nblk - 1) // per_trip, trip, 0)
        elif nblk > 1:
            def trip(r, carry, block=block, nblk=nblk):
                block(r, 0, True)
                for i in range(1, nblk):
                    block(r, i, False)
                return carry

            lax.fori_loop(0, d, trip, 0)
        else:
            per_trip = 4

            def trip(t, carry, block=block):
                for u in range(per_trip):
                    block(per_trip * t + u, 0, True)
                return carry

            lax.fori_loop(0, d // per_trip, trip, 0)

    l0, l1, l2 = l_scr[0][...], l_scr[1][...], l_scr[2][...]
    m = jnp.maximum(jnp.maximum(l0, l1), l2)
    e0, e1, e2 = jnp.exp(l0 - m), jnp.exp(l1 - m), jnp.exp(l2 - m)
    acc = e0 * o_scr[0][...] + e1 * o_scr[1][...] + e2 * o_scr[2][...]
    out_ref[...] = (acc / (e0 + e1 + e2)).astype(BF16)


def _attn_prompt(q32, k32, v32, ext):
    def col(g):
        return pl.BlockSpec((SEQ, HEAD_DIM), lambda b, h: (b, g * HEADS_PER_GROUP + h))

    def erow(g):
        return pl.BlockSpec((1, 1, 2 * ROW_BLK), lambda b, h: (g * HEADS_PER_GROUP + h, 0, 0))

    groups = range(N_GROUPS)
    return pl.pallas_call(
        _attn_p_kernel,
        grid=(BATCH, HEADS_PER_GROUP),
        in_specs=[col(g) for g in groups] * 3 + [erow(g) for g in groups],
        out_specs=pl.BlockSpec((SEQ, HEAD_DIM), lambda b, h: (b, h)),
        out_shape=jax.ShapeDtypeStruct((N_PROMPT, GROUP_COLS), BF16),
        scratch_shapes=[pltpu.VMEM((SEQ, HEAD_DIM), F32) for _ in range(2 * N_GROUPS)],
        compiler_params=_cparams("arbitrary", "arbitrary"),
        name="attn_p",
    )(q32, q32, q32, k32, k32, k32, v32, v32, v32, ext, ext, ext)


SAMPLE_BB = 4


def _attn_s_kernel(*refs):
    q_ref, kn_ref, vn_ref = refs[0:3]
    ck_refs, cv_refs = refs[3:6], refs[6:9]
    bc_ref, b0_ref, o_ref = refs[9], refs[10], refs[11]

    def one(bi, carry):
        outs, lses = [], []
        for g in range(N_GROUPS):
            hs = slice(g * HEADS_PER_GROUP, (g + 1) * HEADS_PER_GROUP)
            q = q_ref[bi, hs, :]
            lc = jnp.sum(ck_refs[g][bi] * q[None], axis=-1, keepdims=True) + bc_ref[g]
            ln = jnp.sum(kn_ref[bi, hs, :] * q, axis=-1, keepdims=True) + b0_ref[g]
            m = jnp.maximum(jnp.max(lc, axis=0), ln)
            p = jnp.exp(lc - m[None])
            pn = jnp.exp(ln - m)
            s = jnp.sum(p, axis=0) + pn
            outs.append((jnp.sum(p * cv_refs[g][bi], axis=0) + pn * vn_ref[bi, hs, :]) / s)
            lses.append(m + jnp.log(s))
        m = jnp.maximum(jnp.maximum(lses[0], lses[1]), lses[2])
        e = [jnp.exp(l - m) for l in lses]
        acc = e[0] * outs[0] + e[1] * outs[1] + e[2] * outs[2]
        o_ref[bi] = acc / (e[0] + e[1] + e[2])
        return carry

    lax.fori_loop(0, SAMPLE_BB, one, 0)


def _attn_sample(q32, k32, v32, caches, biases):
    heads3 = lambda a: a[N_PROMPT:].reshape(DEC_BATCH, N_GROUPS * HEADS_PER_GROUP, HEAD_DIM)
    new = pl.BlockSpec((SAMPLE_BB, N_GROUPS * HEADS_PER_GROUP, HEAD_DIM), lambda s: (s, 0, 0))
    bc = jnp.stack([b[::-1][:N_KEYS - 1] for b in biases])[..., None]
    b0 = jnp.stack([b[0] for b in biases])[..., None]
    cache_specs, cache_args = [], []
    for which in range(2):
        for g, (window, d) in enumerate(ATTN_GROUPS):
            c = caches[g][which]
            cache_args.append(c.reshape(DEC_BATCH, c.shape[1] // d, d, HEADS_PER_GROUP, HEAD_DIM))
            cache_specs.append(pl.BlockSpec((SAMPLE_BB, N_KEYS - 1, None, HEADS_PER_GROUP, HEAD_DIM),
                                            lambda s: (s, 0, 0, 0, 0)))
    o = pl.pallas_call(
        _attn_s_kernel,
        grid=(DEC_BATCH // SAMPLE_BB,),
        in_specs=[
            new, new, new, *cache_specs,
            pl.BlockSpec((N_GROUPS, N_KEYS - 1, HEADS_PER_GROUP, 1), lambda s: (0, 0, 0, 0)),
            pl.BlockSpec((N_GROUPS, HEADS_PER_GROUP, 1), lambda s: (0, 0, 0)),
        ],
        out_specs=pl.BlockSpec((SAMPLE_BB, HEADS_PER_GROUP, HEAD_DIM), lambda s: (s, 0, 0)),
        out_shape=jax.ShapeDtypeStruct((DEC_BATCH, HEADS_PER_GROUP, HEAD_DIM), F32),
        compiler_params=_cparams("arbitrary"),
        name="attn_s",
    )(heads3(q32), heads3(k32), heads3(v32), *cache_args, bc, b0)
    return o.reshape(DEC_BATCH, GROUP_COLS).astype(BF16)


SCAN_CHUNK = 64


def _softplus(x):
    return jnp.maximum(x, 0.0) + jnp.log1p(jnp.exp(-jnp.abs(x)))


def _gates(conv, wr_ref, br_ref, wi_ref, bi_ref, l_ref):
    cb = conv.astype(BF16)
    r = _sigmoid(jnp.dot(cb, wr_ref[0].astype(BF16), preferred_element_type=F32) + br_ref[...])
    ig = _sigmoid(jnp.dot(cb, wi_ref[0].astype(BF16), preferred_element_type=F32) + bi_ref[...])
    log_a = -LRU_C * r * _softplus(-l_ref[...])
    return log_a, ig


def _decay(log_a):
    a = jnp.exp(log_a)
    return a, jnp.sqrt(-jnp.tanh(log_a) * (a * a + 1.0))


def _rec_p_kernel(xb_ref, gb_ref, cw_ref, cb_ref, wr_ref, br_ref, wi_ref, bi_ref, l_ref,
                  y_ref, h_ref, a_scr, b_scr):
    t = SEQ
    x = xb_ref[...]
    row = lax.broadcasted_iota(jnp.int32, (t, LRU_BLOCK), 0)
    conv = cb_ref[...] + x * cw_ref[CONV_WIDTH - 1:CONV_WIDTH, :]
    for s in range(1, CONV_WIDTH):
        xs = jnp.where(row >= s, pltpu.roll(x, s, 0), 0.0)
        conv = conv + xs * cw_ref[CONV_WIDTH - 1 - s:CONV_WIDTH - s, :]
    log_a, ig = _gates(conv, wr_ref, br_ref, wi_ref, bi_ref, l_ref)
    a, mult = _decay(log_a)
    start = row == 0
    a_scr[...] = jnp.where(start, 0.0, a)
    b_scr[...] = jnp.where(start, 1.0, mult) * ig * conv

    crow = lax.broadcasted_iota(jnp.int32, (SCAN_CHUNK, LRU_BLOCK), 0)

    def chunk(c, h_prev):
        rows = pl.ds(pl.multiple_of(c * SCAN_CHUNK, SCAN_CHUNK), SCAN_CHUNK)
        a_c = a_scr[rows, :]
        b_c = b_scr[rows, :]
        s = 1
        while s < SCAN_CHUNK:
            keep = crow >= s
            a_sh = jnp.where(keep, pltpu.roll(a_c, s, 0), 1.0)
            b_sh = jnp.where(keep, pltpu.roll(b_c, s, 0), 0.0)
            b_c = a_c * b_sh + b_c
            a_c = a_c * a_sh
            s *= 2
        h = a_c * h_prev + b_c
        y_ref[rows, :] = (h * _gelu(gb_ref[rows, :])).astype(BF16)
        return h[SCAN_CHUNK - 1:SCAN_CHUNK, :]

    h_last = lax.fori_loop(0, t // SCAN_CHUNK, chunk, jnp.zeros((1, LRU_BLOCK), F32))
    h_ref[0] = h_last


def _rec_prompt(xg, conv_w, conv_b, w_rg, b_rg, w_ig, b_ig, lru_l):
    nb = LRU_BLOCKS
    vec = pl.BlockSpec((1, LRU_BLOCK), lambda b, n: (0, n))
    wblk = pl.BlockSpec((1, LRU_BLOCK, LRU_BLOCK), lambda b, n: (n, 0, 0))
    y, h = pl.pallas_call(
        _rec_p_kernel,
        grid=(BATCH, nb),
        in_specs=[
            pl.BlockSpec((SEQ, LRU_BLOCK), lambda b, n: (b, n)),
            pl.BlockSpec((SEQ, LRU_BLOCK), lambda b, n: (b, nb + n)),
            pl.BlockSpec((CONV_WIDTH, LRU_BLOCK), lambda b, n: (0, n)),
            vec, wblk, vec, wblk, vec, vec,
        ],
        out_specs=[
            pl.BlockSpec((SEQ, LRU_BLOCK), lambda b, n: (b, n)),
            pl.BlockSpec((1, 1, LRU_BLOCK), lambda b, n: (b, 0, n)),
        ],
        out_shape=[
            jax.ShapeDtypeStruct((N_PROMPT, LRU_WIDTH), BF16),
            jax.ShapeDtypeStruct((BATCH, 1, LRU_WIDTH), F32),
        ],
        scratch_shapes=[pltpu.VMEM((SEQ, LRU_BLOCK), F32), pltpu.VMEM((SEQ, LRU_BLOCK), F32)],
        compiler_params=_cparams("arbitrary", "arbitrary"),
        name="rec_p",
    )(xg, xg, conv_w, conv_b, w_rg, b_rg, w_ig, b_ig, lru_l)
    return y, h.reshape(BATCH, LRU_WIDTH)


def _rec_s_kernel(xb_ref, gb_ref, c0_ref, c1_ref, c2_ref, h0_ref, cw_ref, cb_ref,
                  wr_ref, br_ref, wi_ref, bi_ref, l_ref, y_ref, h_ref):
    conv = (cb_ref[...] + c0_ref[...] * cw_ref[0:1, :] + c1_ref[...] * cw_ref[1:2, :]
            + c2_ref[...] * cw_ref[2:3, :] + xb_ref[...] * cw_ref[3:4, :])
    log_a, ig = _gates(conv, wr_ref, br_ref, wi_ref, bi_ref, l_ref)
    a, mult = _decay(log_a)
    h = a * h0_ref[...] + mult * ig * conv
    h_ref[...] = h
    y_ref[...] = (h * _gelu(gb_ref[...])).astype(BF16)


def _rec_sample(xg, state_conv, h0, conv_w, conv_b, w_rg, b_rg, w_ig, b_ig, lru_l):
    nb = LRU_BLOCKS
    sc = state_conv.reshape(DEC_BATCH, (CONV_WIDTH - 1) * LRU_WIDTH)
    srow = N_PROMPT // DEC_BATCH
    vec = pl.BlockSpec((1, LRU_BLOCK), lambda n: (0, n))
    wblk = pl.BlockSpec((1, LRU_BLOCK, LRU_BLOCK), lambda n: (n, 0, 0))
    tile = lambda f: pl.BlockSpec((DEC_BATCH, LRU_BLOCK), f)
    return pl.pallas_call(
        _rec_s_kernel,
        grid=(nb,),
        in_specs=[
            tile(lambda n: (srow, n)),
            tile(lambda n: (srow, nb + n)),
            tile(lambda n: (0, n)),
            tile(lambda n: (0, nb + n)),
            tile(lambda n: (0, 2 * nb + n)),
            tile(lambda n: (0, n)),
            pl.BlockSpec((CONV_WIDTH, LRU_BLOCK), lambda n: (0, n)),
            vec, wblk, vec, wblk, vec, vec,
        ],
        out_specs=[tile(lambda n: (0, n)), tile(lambda n: (0, n))],
        out_shape=[
            jax.ShapeDtypeStruct((DEC_BATCH, LRU_WIDTH), BF16),
            jax.ShapeDtypeStruct((DEC_BATCH, LRU_WIDTH), F32),
        ],
        compiler_params=_cparams("arbitrary"),
        name="rec_s",
    )(xg, xg, sc, sc, sc, h0, conv_w, conv_b, w_rg, b_rg, w_ig, b_ig, lru_l)


def _mix_kernel(at_ref, rc_ref, ga_ref, gr_ref, wa_ref, wb_ref, o_ref, wa_bf, wb_bf):
    @pl.when(pl.program_id(1) == 0)
    def _():
        wa_bf[...] = wa_ref[...].astype(BF16)
        wb_bf[...] = wb_ref[...].astype(BF16)

    pa = jnp.dot(at_ref[...], wa_bf[...], preferred_element_type=F32)
    pb = jnp.dot(rc_ref[...], wb_bf[...], preferred_element_type=F32)
    o_ref[...] = (ga_ref[...].astype(F32) * pa + gr_ref[...].astype(F32) * pb).astype(BF16)


def _mix(attn, rec, gates, w_pa, w_pb):
    nj = D_MODEL // MM_TN
    return pl.pallas_call(
        _mix_kernel,
        grid=(nj, N_TOK // MM_TM),
        in_specs=[
            pl.BlockSpec((MM_TM, GROUP_COLS), lambda j, i: (i, 0)),
            pl.BlockSpec((MM_TM, LRU_WIDTH), lambda j, i: (i, 0)),
            pl.BlockSpec((MM_TM, MM_TN), lambda j, i: (i, j)),
            pl.BlockSpec((MM_TM, MM_TN), lambda j, i: (i, nj + j)),
            pl.BlockSpec((GROUP_COLS, MM_TN), lambda j, i: (0, j)),
            pl.BlockSpec((LRU_WIDTH, MM_TN), lambda j, i: (0, j)),
        ],
        out_specs=pl.BlockSpec((MM_TM, MM_TN), lambda j, i: (i, j)),
        out_shape=jax.ShapeDtypeStruct((N_TOK, D_MODEL), BF16),
        scratch_shapes=[pltpu.VMEM((GROUP_COLS, MM_TN), BF16), pltpu.VMEM((LRU_WIDTH, MM_TN), BF16)],
        compiler_params=_cparams("arbitrary", "arbitrary"),
        name="mix",
    )(attn, rec, gates, gates, w_pa, w_pb)


def _epi_residual(acc, extra, outs):
    outs[0][...] = extra[0][...] + acc


def _norm2_kernel(x_ref, g_ref, xn_ref, xnt_ref):
    x = x_ref[...]
    ms = jnp.mean(x * x, axis=-1, keepdims=True)
    y = x * lax.rsqrt(ms + EPS) * g_ref[...]
    xn_ref[...] = y.astype(BF16)
    xnt_ref[...] = y.T.astype(BF16)


def _norm2(x1, g):
    return pl.pallas_call(
        _norm2_kernel,
        grid=(N_ROW_BLK,),
        in_specs=[
            pl.BlockSpec((ROW_BLK, D_MODEL), lambda i: (i, 0)),
            pl.BlockSpec((1, D_MODEL), lambda i: (0, 0)),
        ],
        out_specs=[
            pl.BlockSpec((ROW_BLK, D_MODEL), lambda i: (i, 0)),
            pl.BlockSpec((D_MODEL, ROW_BLK), lambda i: (0, i)),
        ],
        out_shape=[
            jax.ShapeDtypeStruct((N_TOK, D_MODEL), BF16),
            jax.ShapeDtypeStruct((D_MODEL, N_TOK), BF16),
        ],
        compiler_params=_cparams("arbitrary"),
        name="norm2",
    )(x1, g)


def _top_values(x, k):
    rows = lax.broadcasted_iota(jnp.int32, x.shape, 0).astype(F32)
    out = []
    for _ in range(k):
        m = jnp.max(x, axis=0, keepdims=True)
        first = jnp.min(jnp.where(x == m, rows, float(x.shape[0])), axis=0, keepdims=True)
        out.append(m)
        x = jnp.where(rows == first, -jnp.inf, x)
    return out


def _peer_prep_kernel(x_ref, wq_ref, k1_ref, k2_ref, s1_ref, s2_ref, e1_ref, e2_ref, tau_ref, wq_bf):
    @pl.when(pl.program_id(1) == 0)
    def _():
        wq_bf[...] = wq_ref[...].astype(BF16)

    half = PEER_DKEY // 2
    q = jnp.dot(x_ref[...], wq_bf[...], preferred_element_type=F32).astype(BF16)
    nt = (((1,), (1,)), ((), ()))
    s1 = lax.dot_general(k1_ref[0].astype(BF16), q[:, :half], nt, preferred_element_type=F32)
    s2 = lax.dot_general(k2_ref[0].astype(BF16), q[:, half:], nt, preferred_element_type=F32)
    s1_ref[0] = s1
    s2_ref[0] = s2
    for c in range(PREP_TM // 128):
        cl = slice(c * 128, (c + 1) * 128)
        x1, x2 = s1[:, cl], s2[:, cl]
        a = _top_values(x1, PEER_TOPK)
        b = _top_values(x2, PEER_TOPK)
        cand = [a[i] + b[j] for i in range(PEER_TOPK) for j in range(PEER_TOPK) if (i + 1) * (j + 1) <= PEER_TOPK]
        cand += [jnp.full_like(a[0], -jnp.inf)] * (-len(cand) % 8)
        top = _top_values(jnp.concatenate(cand, axis=0), PEER_TOPK)
        z = jnp.ones_like(top[0])
        for v in top[1:]:
            z = z + jnp.exp(v - top[0])
        e1_ref[0, :, cl] = jnp.exp(x1 - a[0]) / z
        e2_ref[0, :, cl] = jnp.exp(x2 - b[0])
        tau_ref[0, :, cl] = top[PEER_TOPK - 1]


PREP_TM = 640


def _peer_prep(xn2, w_query, k1, k2):
    tm = PREP_TM
    half = PEER_DKEY // 2
    big = pl.BlockSpec((1, PEER_KEYS, tm), lambda h, i: (h, 0, i))
    keys = pl.BlockSpec((1, PEER_KEYS, half), lambda h, i: (h, 0, 0))
    shp = jax.ShapeDtypeStruct((PEER_HEADS, PEER_KEYS, N_TOK), F32)
    return pl.pallas_call(
        _peer_prep_kernel,
        grid=(PEER_HEADS, N_TOK // tm),
        in_specs=[
            pl.BlockSpec((tm, D_MODEL), lambda h, i: (i, 0)),
            pl.BlockSpec((D_MODEL, PEER_DKEY), lambda h, i: (0, h)),
            keys, keys,
        ],
        out_specs=[big, big, big, big, pl.BlockSpec((1, 1, tm), lambda h, i: (h, 0, i))],
        out_shape=[shp, shp, shp, shp, jax.ShapeDtypeStruct((PEER_HEADS, 1, N_TOK), F32)],
        scratch_shapes=[pltpu.VMEM((D_MODEL, PEER_DKEY), BF16)],
        compiler_params=_cparams("arbitrary", "arbitrary"),
        name="peer_prep",
    )(xn2, w_query, k1, k2)


I1_PER_TILE = PEER_TE // PEER_KEYS
I1_BLK = 8


def _peer_kernel(xt_ref, u_ref, vt_ref, s1_ref, e1_ref, s2_ref, e2_ref, tau_ref, o_ref, w_scr):
    e = pl.program_id(1)

    @pl.when(e == 0)
    def _():
        o_ref[...] = jnp.zeros_like(o_ref)

    st = jnp.dot(u_ref[...], xt_ref[...], preferred_element_type=F32)
    off = (e % (I1_BLK // I1_PER_TILE)) * I1_PER_TILE
    for r in range(I1_PER_TILE):
        rows = slice(r * PEER_KEYS, (r + 1) * PEER_KEYS)
        coef = jnp.zeros((PEER_KEYS, PEER_TM), F32)
        for h in range(PEER_HEADS):
            s = s1_ref[h, pl.ds(off + r, 1), :] + s2_ref[h]
            p = e1_ref[h, pl.ds(off + r, 1), :] * e2_ref[h]
            coef = coef + jnp.where(s >= tau_ref[h], p, 0.0)
        w_scr[rows, :] = (coef * _gelu(st[rows, :])).astype(BF16)
    o_ref[...] += jnp.dot(vt_ref[...], w_scr[...], preferred_element_type=F32)


def _peer(xn2t, u_bf, vt_bf, s1, s2, e1, e2, tau):
    nt = pl.cdiv(N_TOK, PEER_TM)
    ne = PEER_EXPERTS // PEER_TE
    per = I1_BLK // I1_PER_TILE
    small = pl.BlockSpec((PEER_HEADS, I1_BLK, PEER_TM), lambda t, e: (0, e // per, t))
    big = pl.BlockSpec((PEER_HEADS, PEER_KEYS, PEER_TM), lambda t, e: (0, 0, t))
    return pl.pallas_call(
        _peer_kernel,
        grid=(nt, ne),
        in_specs=[
            pl.BlockSpec((D_MODEL, PEER_TM), lambda t, e: (0, t)),
            pl.BlockSpec((PEER_TE, D_MODEL), lambda t, e: (e, 0)),
            pl.BlockSpec((D_MODEL, PEER_TE), lambda t, e: (0, e)),
            small, small, big, big,
            pl.BlockSpec((PEER_HEADS, 1, PEER_TM), lambda t, e: (0, 0, t)),
        ],
        out_specs=pl.BlockSpec((D_MODEL, PEER_TM), lambda t, e: (0, t)),
        out_shape=jax.ShapeDtypeStruct((D_MODEL, N_TOK), F32),
        scratch_shapes=[pltpu.VMEM((PEER_TE, PEER_TM), BF16)],
        compiler_params=_cparams("arbitrary", "arbitrary"),
        name="peer",
    )(xn2t, u_bf, vt_bf, s1, e1, s2, e2, tau)


def _final_kernel(x_ref, pt_ref, yp_ref, ys_ref):
    i = pl.program_id(0)
    y = x_ref[...] + pt_ref[...].T

    @pl.when(i < N_PROMPT_BLK)
    def _():
        yp_ref[...] = y

    @pl.when(i == N_PROMPT_BLK)
    def _():
        ys_ref[...] = y


def _final(x1, peer_t):
    return pl.pallas_call(
        _final_kernel,
        grid=(N_ROW_BLK,),
        in_specs=[
            pl.BlockSpec((ROW_BLK, D_MODEL), lambda i: (i, 0)),
            pl.BlockSpec((D_MODEL, ROW_BLK), lambda i: (0, i)),
        ],
        out_specs=[
            pl.BlockSpec((ROW_BLK, D_MODEL), lambda i: (jnp.minimum(i, N_PROMPT_BLK - 1), 0)),
            pl.BlockSpec((ROW_BLK, D_MODEL), lambda i: (0, 0)),
        ],
        out_shape=[
            jax.ShapeDtypeStruct((N_PROMPT, D_MODEL), F32),
            jax.ShapeDtypeStruct((DEC_BATCH, D_MODEL), F32),
        ],
        compiler_params=_cparams("arbitrary"),
        name="final",
    )(x1, peer_t)


def _bucket(dist):
    max_exact = N_BUCKETS // 2
    dd = jnp.maximum(dist, max_exact).astype(F32)
    large = max_exact + (jnp.log(dd / max_exact) / math.log(RP_MAX_DIST / max_exact)
                         * (N_BUCKETS - max_exact)).astype(jnp.int32)
    large = jnp.minimum(large, N_BUCKETS - 1)
    return jnp.where(dist < max_exact, dist, large)


def _group_bias(rel_bias, g, d):
    dist = d * jnp.arange(N_KEYS, dtype=jnp.int32)
    bias = rel_bias[:, g * HEADS_PER_GROUP:(g + 1) * HEADS_PER_GROUP][_bucket(dist)]
    pad = jnp.full((HEADS_PER_GROUP, 2 * ROW_BLK - N_KEYS), NEG, F32)
    ext = jnp.concatenate([bias[::-1].T, pad], axis=1)
    return bias, ext[:, None, :]


def kernel(x_prompt, x_sample, cache_k_w128, cache_v_w128, cache_k_w512, cache_v_w512, cache_k_w2048,
           cache_v_w2048, state_lru_h, state_conv, rel_bias, ln1_g, w_in, q_norm_g, k_norm_g, conv_w,
           conv_b, w_rg, b_rg, w_ig, b_ig, lru_L, w_pa, w_pb, w_o, ln2_g, w_query, sub_keys_1,
           sub_keys_2, peer_u, peer_v):
    caches = ((cache_k_w128, cache_v_w128), (cache_k_w512, cache_v_w512), (cache_k_w2048, cache_v_w2048))
    xp = x_prompt.reshape(N_PROMPT, D_MODEL)
    xs = x_sample.reshape(DEC_BATCH, D_MODEL)
    w = w_in[0]

    xn, x_all = _norm1(xp, xs, ln1_g)
    gspec = [pl.BlockSpec((1, HEAD_DIM), lambda j, i: (0, 0))]
    (q32,) = _proj(xn, w, Q_OFF, ATTN_WIDTH, _epi_q, (F32,), (q_norm_g,), gspec)
    (k32,) = _proj(xn, w, K_OFF, ATTN_WIDTH, _epi_k, (F32,), (k_norm_g,), gspec)
    (v32,) = _proj(xn, w, V_OFF, ATTN_WIDTH, _epi_f32, (F32,))
    (xg,) = _proj(xn, w, XB_OFF, 2 * LRU_WIDTH, _epi_f32, (F32,))
    (gates,) = _proj(xn, w, GA_OFF, 2 * D_MODEL, _epi_sigmoid, (BF16,))

    biases, exts = zip(*[_group_bias(rel_bias, g, d) for g, (_, d) in enumerate(ATTN_GROUPS)])
    attn_p = _attn_prompt(q32, k32, v32, jnp.concatenate(exts, axis=0))
    attn_s = _attn_sample(q32, k32, v32, [(ck[0], cv[0]) for ck, cv in caches], biases)
    attn = jnp.concatenate([attn_p, attn_s], axis=0)

    rec_p, h_p = _rec_prompt(xg, conv_w[0], conv_b, w_rg[0], b_rg, w_ig[0], b_ig, lru_L)
    rec_s, h_s = _rec_sample(xg, state_conv[0], state_lru_h[0], conv_w[0], conv_b, w_rg[0], b_rg,
                             w_ig[0], b_ig, lru_L)
    rec = jnp.concatenate([rec_p, rec_s], axis=0)

    merged = _mix(attn, rec, gates, w_pa[0], w_pb[0])
    (x1,) = _proj(merged, w_o[0], 0, D_MODEL, _epi_residual, (F32,), (x_all,),
                  [pl.BlockSpec((MM_TM, MM_TN), lambda j, i: (i, j))])

    xn2, xn2t = _norm2(x1, ln2_g)
    s1, s2, e1, e2, tau = _peer_prep(xn2, w_query[0], sub_keys_1[0], sub_keys_2[0])
    peer_t = _peer(xn2t, peer_u[0].astype(BF16), peer_v[0].T.astype(BF16), s1, s2, e1, e2, tau)
    y_p, y_s = _final(x1, peer_t)

    outs = [y_p.reshape(BATCH, SEQ, D_MODEL), y_s.reshape(DEC_BATCH, 1, D_MODEL)]
    for g, (window, d) in enumerate(ATTN_GROUPS):
        keep = min(window, SEQ)
        cols = slice(g * GROUP_COLS, (g + 1) * GROUP_COLS)
        for a in (k32, v32):
            tail = a[:N_PROMPT].reshape(BATCH, SEQ, ATTN_WIDTH)[:, SEQ - keep:, cols]
            outs.append(tail.reshape(1, BATCH, keep, HEADS_PER_GROUP, HEAD_DIM))
        for a in (k32, v32):
            outs.append(a[N_PROMPT:, cols].reshape(1, DEC_BATCH, 1, HEADS_PER_GROUP, HEAD_DIM))
    xb_p = xg[:N_PROMPT, :LRU_WIDTH].reshape(BATCH, SEQ, LRU_WIDTH)
    xb_s = xg[N_PROMPT:, :LRU_WIDTH]
    outs += [
        h_p[None], h_s[None],
        xb_p[:, SEQ - (CONV_WIDTH - 1):][None],
        jnp.concatenate([state_conv[0][:, 1:], xb_s[:, None, :]], axis=1)[None],
    ]
    return tuple(outs)
```

```python
import functools
import math

import jax
import jax.numpy as jnp
import numpy as np
from jax import lax
from jax.experimental import pallas as pl
from jax.experimental.pallas import tpu as pltpu

F32 = jnp.float32
BF16 = jnp.bfloat16

D_MODEL = 4096
BATCH = 4
SEQ = 2048
DEC_BATCH = 128
N_PROMPT = BATCH * SEQ
N_TOK = N_PROMPT + DEC_BATCH
HEAD_DIM = 128
HEADS_PER_GROUP = 8
GROUP_COLS = HEADS_PER_GROUP * HEAD_DIM
ATTN_GROUPS = ((128, 1), (512, 4), (2048, 16))
N_GROUPS = 3
ATTN_WIDTH = N_GROUPS * GROUP_COLS
N_KEYS = 129
N_BUCKETS = 32
RP_MAX_DIST = 2048
LRU_WIDTH = 2048
LRU_BLOCKS = 16
LRU_BLOCK = 128
CONV_WIDTH = 4
LRU_C = 8.0
PEER_HEADS = 8
PEER_KEYS = 128
PEER_EXPERTS = PEER_KEYS * PEER_KEYS
PEER_DKEY = 256
PEER_TOPK = 16
EPS = 1e-6

Q_OFF = 0
K_OFF = ATTN_WIDTH
V_OFF = 2 * ATTN_WIDTH
XB_OFF = 3 * ATTN_WIDTH
GA_OFF = XB_OFF + 2 * LRU_WIDTH

ROW_BLK = 128
N_PROMPT_BLK = N_PROMPT // ROW_BLK
N_ROW_BLK = N_TOK // ROW_BLK
MM_TM = 1040
MM_TN = 512
IN_TM = 416
IN_TN = 1024
PEER_TM = 512
PEER_TE = 512
NEG = -1e30
VMEM_LIMIT = 56 * 1024 * 1024


def _cparams(*sem):
    return pltpu.CompilerParams(dimension_semantics=sem, vmem_limit_bytes=VMEM_LIMIT)


def _gelu(x):
    return jax.nn.gelu(x)


def _sigmoid(x):
    return 1.0 / (1.0 + jnp.exp(-x))


def _norm1_kernel(xp_ref, xs_ref, g_ref, xn_ref, xall_ref):
    i = pl.program_id(0)

    def body(x):
        ms = jnp.mean(x * x, axis=-1, keepdims=True)
        xn_ref[...] = (x * lax.rsqrt(ms + EPS) * g_ref[...]).astype(BF16)
        xall_ref[...] = x

    @pl.when(i < N_PROMPT_BLK)
    def _():
        body(xp_ref[...])

    @pl.when(i == N_PROMPT_BLK)
    def _():
        body(xs_ref[...])


def _norm1(xp, xs, g):
    return pl.pallas_call(
        _norm1_kernel,
        grid=(N_ROW_BLK,),
        in_specs=[
            pl.BlockSpec((ROW_BLK, D_MODEL), lambda i: (jnp.minimum(i, N_PROMPT_BLK - 1), 0)),
            pl.BlockSpec((ROW_BLK, D_MODEL), lambda i: (0, 0)),
            pl.BlockSpec((1, D_MODEL), lambda i: (0, 0)),
        ],
        out_specs=[
            pl.BlockSpec((ROW_BLK, D_MODEL), lambda i: (i, 0)),
            pl.BlockSpec((ROW_BLK, D_MODEL), lambda i: (i, 0)),
        ],
        out_shape=[
            jax.ShapeDtypeStruct((N_TOK, D_MODEL), BF16),
            jax.ShapeDtypeStruct((N_TOK, D_MODEL), F32),
        ],
        compiler_params=_cparams("arbitrary"),
        name="norm1",
    )(xp, xs, g)


def _proj_kernel(*refs, n_extra, epilogue):
    x_ref, w_ref = refs[0], refs[1]
    extra = refs[2:2 + n_extra]
    outs = refs[2 + n_extra:-1]
    wbf_ref = refs[-1]

    @pl.when(pl.program_id(1) == 0)
    def _():
        wbf_ref[...] = w_ref[...].astype(BF16)

    acc = jnp.dot(x_ref[...], wbf_ref[...], preferred_element_type=F32)
    epilogue(acc, extra, outs)


def _proj(x, w, col0, ncols, epilogue, out_dtypes, extra=(), extra_specs=(), tm=MM_TM, tn=MM_TN):
    m, k = x.shape
    grid = (ncols // tn, m // tm)
    cb0 = col0 // tn
    kern = functools.partial(_proj_kernel, n_extra=len(extra), epilogue=epilogue)
    return pl.pallas_call(
        kern,
        grid=grid,
        in_specs=[
            pl.BlockSpec((tm, k), lambda j, i: (i, 0)),
            pl.BlockSpec((k, tn), lambda j, i: (0, cb0 + j)),
            *extra_specs,
        ],
        out_specs=[pl.BlockSpec((tm, tn), lambda j, i: (i, j)) for _ in out_dtypes],
        out_shape=[jax.ShapeDtypeStruct((m, ncols), dt) for dt in out_dtypes],
        scratch_shapes=[pltpu.VMEM((k, tn), BF16)],
        compiler_params=_cparams("arbitrary", "arbitrary"),
        name="proj",
    )(x, w, *extra)


def _head_norm(a, g):
    ms = jnp.mean(a * a, axis=-1, keepdims=True)
    return a * lax.rsqrt(ms + EPS) * g


def _epi_q(acc, extra, outs):
    g = extra[0][...]
    for h in range(acc.shape[1] // HEAD_DIM):
        sl = slice(h * HEAD_DIM, (h + 1) * HEAD_DIM)
        outs[0][:, sl] = _head_norm(acc[:, sl], g) * (HEAD_DIM ** -0.5)


def _epi_k(acc, extra, outs):
    g = extra[0][...]
    for h in range(acc.shape[1] // HEAD_DIM):
        sl = slice(h * HEAD_DIM, (h + 1) * HEAD_DIM)
        outs[0][:, sl] = _head_norm(acc[:, sl], g)


def _epi_f32(acc, extra, outs):
    outs[0][...] = acc


def _epi_sigmoid(acc, extra, outs):
    outs[0][...] = _sigmoid(acc).astype(BF16)


def _attn_p_kernel(*refs):
    q_refs, k_refs, v_refs, ext_refs = refs[0:3], refs[3:6], refs[6:9], refs[9:12]
    out_ref = refs[12]
    o_scr, l_scr = refs[13:16], refs[16:19]
    nt = (((1,), (1,)), ((), ()))

    for g, (_, d) in enumerate(ATTN_GROUPS):
        q_ref, k_ref, v_ref = q_refs[g], k_refs[g], v_refs[g]
        band = pltpu.roll(jnp.broadcast_to(ext_refs[g][0], (ROW_BLK, 2 * ROW_BLK)), 0, 1,
                          stride=1, stride_axis=0)
        band_cur = band[:, ROW_BLK:]
        nblk = SEQ // d // ROW_BLK

        def rows(r, i, d=d):
            start = r + (d * ROW_BLK) * i
            return pl.ds(start, ROW_BLK) if d == 1 else pl.ds(start, ROW_BLK, stride=d)

        def block(r, i, first, g=g, q_ref=q_ref, k_ref=k_ref, v_ref=v_ref, band=band,
                  band_cur=band_cur, rows=rows):
            cur = rows(r, i)
            q = q_ref[cur, :].astype(BF16)
            if first:
                kk = k_ref[cur, :].astype(BF16)
                vv = v_ref[cur, :].astype(BF16)
                bias = band_cur
            else:
                prev = rows(r, i - 1)
                kk = jnp.concatenate([k_ref[prev, :], k_ref[cur, :]], axis=0).astype(BF16)
                vv = jnp.concatenate([v_ref[prev, :], v_ref[cur, :]], axis=0).astype(BF16)
                bias = band
            lg = lax.dot_general(q, kk, nt, preferred_element_type=F32) + bias
            m = jnp.max(lg, axis=-1, keepdims=True)
            p = jnp.exp(lg - m)
            s = jnp.sum(p, axis=-1, keepdims=True)
            o_scr[g][cur, :] = jnp.dot(p.astype(BF16), vv, preferred_element_type=F32) / s
            l_scr[g][cur, :] = jnp.broadcast_to(m + jnp.log(s), (ROW_BLK, HEAD_DIM))

        if d == 1:
            block(0, 0, True)
            per_trip = 3

            def trip(t, carry, block=block):
                for u in range(per_trip):
                    block(0, 1 + per_trip * t + u, False)
                return carry

            lax.fori_loop(0, (nblk - 1) // per_trip, trip, 0)
        elif nblk > 1:
            def trip(r, carry, block=block, nblk=nblk):
                block(r, 0, True)
                for i in range(1, nblk):
                    block(r, i, False)
                return carry

            lax.fori_loop(0, d, trip, 0)
        else:
            per_trip = 4

            def trip(t, carry, block=block):
                for u in range(per_trip):
                    block(per_trip * t + u, 0, True)
                return carry

            lax.fori_loop(0, d // per_trip, trip, 0)

    l0, l1, l2 = l_scr[0][...], l_scr[1][...], l_scr[2][...]
    m = jnp.maximum(jnp.maximum(l0, l1), l2)
    e0, e1, e2 = jnp.exp(l0 - m), jnp.exp(l1 - m), jnp.exp(l2 - m)
    acc = e0 * o_scr[0][...] + e1 * o_scr[1][...] + e2 * o_scr[2][...]
    out_ref[...] = (acc / (e0 + e1 + e2)).astype(BF16)


def _attn_prompt(q32, k32, v32, ext):
    def col(g):
        return pl.BlockSpec((SEQ, HEAD_DIM), lambda b, h: (b, g * HEADS_PER_GROUP + h))

    def erow(g):
        return pl.BlockSpec((1, 1, 2 * ROW_BLK), lambda b, h: (g * HEADS_PER_GROUP + h, 0, 0))

    groups = range(N_GROUPS)
    return pl.pallas_call(
        _attn_p_kernel,
        grid=(BATCH, HEADS_PER_GROUP),
        in_specs=[col(g) for g in groups] * 3 + [erow(g) for g in groups],
        out_specs=pl.BlockSpec((SEQ, HEAD_DIM), lambda b, h: (b, h)),
        out_shape=jax.ShapeDtypeStruct((N_PROMPT, GROUP_COLS), BF16),
        scratch_shapes=[pltpu.VMEM((SEQ, HEAD_DIM), F32) for _ in range(2 * N_GROUPS)],
        compiler_params=_cparams("arbitrary", "arbitrary"),
        name="attn_p",
    )(q32, q32, q32, k32, k32, k32, v32, v32, v32, ext, ext, ext)


SAMPLE_BB = 4


def _attn_s_kernel(*refs):
    q_ref, kn_ref, vn_ref = refs[0:3]
    ck_refs, cv_refs = refs[3:6], refs[6:9]
    bc_ref, b0_ref, o_ref = refs[9], refs[10], refs[11]

    def one(bi, carry):
        outs, lses = [], []
        for g in range(N_GROUPS):
            hs = slice(g * HEADS_PER_GROUP, (g + 1) * HEADS_PER_GROUP)
            q = q_ref[bi, hs, :]
            lc = jnp.sum(ck_refs[g][bi] * q[None], axis=-1, keepdims=True) + bc_ref[g]
            ln = jnp.sum(kn_ref[bi, hs, :] * q, axis=-1, keepdims=True) + b0_ref[g]
            m = jnp.maximum(jnp.max(lc, axis=0), ln)
            p = jnp.exp(lc - m[None])
            pn = jnp.exp(ln - m)
            s = jnp.sum(p, axis=0) + pn
            outs.append((jnp.sum(p * cv_refs[g][bi], axis=0) + pn * vn_ref[bi, hs, :]) / s)
            lses.append(m + jnp.log(s))
        m = jnp.maximum(jnp.maximum(lses[0], lses[1]), lses[2])
        e = [jnp.exp(l - m) for l in lses]
        acc = e[0] * outs[0] + e[1] * outs[1] + e[2] * outs[2]
        o_ref[bi] = acc / (e[0] + e[1] + e[2])
        return carry

    lax.fori_loop(0, SAMPLE_BB, one, 0)


def _attn_sample(q32, k32, v32, caches, biases):
    heads3 = lambda a: a[N_PROMPT:].reshape(DEC_BATCH, N_GROUPS * HEADS_PER_GROUP, HEAD_DIM)
    new = pl.BlockSpec((SAMPLE_BB, N_GROUPS * HEADS_PER_GROUP, HEAD_DIM), lambda s: (s, 0, 0))
    bc = jnp.stack([b[::-1][:N_KEYS - 1] for b in biases])[..., None]
    b0 = jnp.stack([b[0] for b in biases])[..., None]
    cache_specs, cache_args = [], []
    for which in range(2):
        for g, (window, d) in enumerate(ATTN_GROUPS):
            c = caches[g][which]
            cache_args.append(c.reshape(DEC_BATCH, c.shape[1] // d, d, HEADS_PER_GROUP, HEAD_DIM))
            cache_specs.append(pl.BlockSpec((SAMPLE_BB, N_KEYS - 1, None, HEADS_PER_GROUP, HEAD_DIM),
                                            lambda s: (s, 0, 0, 0, 0)))
    o = pl.pallas_call(
        _attn_s_kernel,
        grid=(DEC_BATCH // SAMPLE_BB,),
        in_specs=[
            new, new, new, *cache_specs,
            pl.BlockSpec((N_GROUPS, N_KEYS - 1, HEADS_PER_GROUP, 1), lambda s: (0, 0, 0, 0)),
            pl.BlockSpec((N_GROUPS, HEADS_PER_GROUP, 1), lambda s: (0, 0, 0)),
        ],
        out_specs=pl.BlockSpec((SAMPLE_BB, HEADS_PER_GROUP, HEAD_DIM), lambda s: (s, 0, 0)),
        out_shape=jax.ShapeDtypeStruct((DEC_BATCH, HEADS_PER_GROUP, HEAD_DIM), F32),
        compiler_params=_cparams("arbitrary"),
        name="attn_s",
    )(heads3(q32), heads3(k32), heads3(v32), *cache_args, bc, b0)
    return o.reshape(DEC_BATCH, GROUP_COLS).astype(BF16)


SCAN_CHUNK = 64


def _softplus(x):
    return jnp.maximum(x, 0.0) + jnp.log1p(jnp.exp(-jnp.abs(x)))


def _gates(conv, wr_ref, br_ref, wi_ref, bi_ref, l_ref):
    cb = conv.astype(BF16)
    r = _sigmoid(jnp.dot(cb, wr_ref[0].astype(BF16), preferred_element_type=F32) + br_ref[...])
    ig = _sigmoid(jnp.dot(cb, wi_ref[0].astype(BF16), preferred_element_type=F32) + bi_ref[...])
    log_a = -LRU_C * r * _softplus(-l_ref[...])
    return log_a, ig


def _decay(log_a):
    a = jnp.exp(log_a)
    return a, jnp.sqrt(-jnp.tanh(log_a) * (a * a + 1.0))


def _rec_p_kernel(xb_ref, gb_ref, cw_ref, cb_ref, wr_ref, br_ref, wi_ref, bi_ref, l_ref,
                  y_ref, h_ref, a_scr, b_scr):
    t = SEQ
    x = xb_ref[...]
    row = lax.broadcasted_iota(jnp.int32, (t, LRU_BLOCK), 0)
    conv = cb_ref[...] + x * cw_ref[CONV_WIDTH - 1:CONV_WIDTH, :]
    for s in range(1, CONV_WIDTH):
        xs = jnp.where(row >= s, pltpu.roll(x, s, 0), 0.0)
        conv = conv + xs * cw_ref[CONV_WIDTH - 1 - s:CONV_WIDTH - s, :]
    log_a, ig = _gates(conv, wr_ref, br_ref, wi_ref, bi_ref, l_ref)
    a, mult = _decay(log_a)
    start = row == 0
    a_scr[...] = jnp.where(start, 0.0, a)
    b_scr[...] = jnp.where(start, 1.0, mult) * ig * conv

    crow = lax.broadcasted_iota(jnp.int32, (SCAN_CHUNK, LRU_BLOCK), 0)

    def chunk(c, h_prev):
        rows = pl.ds(pl.multiple_of(c * SCAN_CHUNK, SCAN_CHUNK), SCAN_CHUNK)
        a_c = a_scr[rows, :]
        b_c = b_scr[rows, :]
        s = 1
        while s < SCAN_CHUNK:
            keep = crow >= s
            a_sh = jnp.where(keep, pltpu.roll(a_c, s, 0), 1.0)
            b_sh = jnp.where(keep, pltpu.roll(b_c, s, 0), 0.0)
            b_c = a_c * b_sh + b_c
            a_c = a_c * a_sh
            s *= 2
        h = a_c * h_prev + b_c
        y_ref[rows, :] = (h * _gelu(gb_ref[rows, :])).astype(BF16)
        return h[SCAN_CHUNK - 1:SCAN_CHUNK, :]

    h_last = lax.fori_loop(0, t // SCAN_CHUNK, chunk, jnp.zeros((1, LRU_BLOCK), F32))
    h_ref[0] = h_last


def _rec_prompt(xg, conv_w, conv_b, w_rg, b_rg, w_ig, b_ig, lru_l):
    nb = LRU_BLOCKS
    vec = pl.BlockSpec((1, LRU_BLOCK), lambda b, n: (0, n))
    wblk = pl.BlockSpec((1, LRU_BLOCK, LRU_BLOCK), lambda b, n: (n, 0, 0))
    y, h = pl.pallas_call(
        _rec_p_kernel,
        grid=(BATCH, nb),
        in_specs=[
            pl.BlockSpec((SEQ, LRU_BLOCK), lambda b, n: (b, n)),
            pl.BlockSpec((SEQ, LRU_BLOCK), lambda b, n: (b, nb + n)),
            pl.BlockSpec((CONV_WIDTH, LRU_BLOCK), lambda b, n: (0, n)),
            vec, wblk, vec, wblk, vec, vec,
        ],
        out_specs=[
            pl.BlockSpec((SEQ, LRU_BLOCK), lambda b, n: (b, n)),
            pl.BlockSpec((1, 1, LRU_BLOCK), lambda b, n: (b, 0, n)),
        ],
        out_shape=[
            jax.ShapeDtypeStruct((N_PROMPT, LRU_WIDTH), BF16),
            jax.ShapeDtypeStruct((BATCH, 1, LRU_WIDTH), F32),
        ],
        scratch_shapes=[pltpu.VMEM((SEQ, LRU_BLOCK), F32), pltpu.VMEM((SEQ, LRU_BLOCK), F32)],
        compiler_params=_cparams("arbitrary", "arbitrary"),
        name="rec_p",
    )(xg, xg, conv_w, conv_b, w_rg, b_rg, w_ig, b_ig, lru_l)
    return y, h.reshape(BATCH, LRU_WIDTH)


def _rec_s_kernel(xb_ref, gb_ref, c0_ref, c1_ref, c2_ref, h0_ref, cw_ref, cb_ref,
                  wr_ref, br_ref, wi_ref, bi_ref, l_ref, y_ref, h_ref):
    conv = (cb_ref[...] + c0_ref[...] * cw_ref[0:1, :] + c1_ref[...] * cw_ref[1:2, :]
            + c2_ref[...] * cw_ref[2:3, :] + xb_ref[...] * cw_ref[3:4, :])
    log_a, ig = _gates(conv, wr_ref, br_ref, wi_ref, bi_ref, l_ref)
    a, mult = _decay(log_a)
    h = a * h0_ref[...] + mult * ig * conv
    h_ref[...] = h
    y_ref[...] = (h * _gelu(gb_ref[...])).astype(BF16)


def _rec_sample(xg, state_conv, h0, conv_w, conv_b, w_rg, b_rg, w_ig, b_ig, lru_l):
    nb = LRU_BLOCKS
    sc = state_conv.reshape(DEC_BATCH, (CONV_WIDTH - 1) * LRU_WIDTH)
    srow = N_PROMPT // DEC_BATCH
    vec = pl.BlockSpec((1, LRU_BLOCK), lambda n: (0, n))
    wblk = pl.BlockSpec((1, LRU_BLOCK, LRU_BLOCK), lambda n: (n, 0, 0))
    tile = lambda f: pl.BlockSpec((DEC_BATCH, LRU_BLOCK), f)
    return pl.pallas_call(
        _rec_s_kernel,
        grid=(nb,),
        in_specs=[
            tile(lambda n: (srow, n)),
            tile(lambda n: (srow, nb + n)),
            tile(lambda n: (0, n)),
            tile(lambda n: (0, nb + n)),
            tile(lambda n: (0, 2 * nb + n)),
            tile(lambda n: (0, n)),
            pl.BlockSpec((CONV_WIDTH, LRU_BLOCK), lambda n: (0, n)),
            vec, wblk, vec, wblk, vec, vec,
        ],
        out_specs=[tile(lambda n: (0, n)), tile(lambda n: (0, n))],
        out_shape=[
            jax.ShapeDtypeStruct((DEC_BATCH, LRU_WIDTH), BF16),
            jax.ShapeDtypeStruct((DEC_BATCH, LRU_WIDTH), F32),
        ],
        compiler_params=_cparams("arbitrary"),
        name="rec_s",
    )(xg, xg, sc, sc, sc, h0, conv_w, conv_b, w_rg, b_rg, w_ig, b_ig, lru_l)


def _mix_kernel(at_ref, rc_ref, ga_ref, gr_ref, wa_ref, wb_ref, o_ref, wa_bf, wb_bf):
    @pl.when(pl.program_id(1) == 0)
    def _():
        wa_bf[...] = wa_ref[...].astype(BF16)
        wb_bf[...] = wb_ref[...].astype(BF16)

    pa = jnp.dot(at_ref[...], wa_bf[...], preferred_element_type=F32)
    pb = jnp.dot(rc_ref[...], wb_bf[...], preferred_element_type=F32)
    o_ref[...] = (ga_ref[...].astype(F32) * pa + gr_ref[...].astype(F32) * pb).astype(BF16)


def _mix(attn, rec, gates, w_pa, w_pb):
    nj = D_MODEL // MM_TN
    return pl.pallas_call(
        _mix_kernel,
        grid=(nj, N_TOK // MM_TM),
        in_specs=[
            pl.BlockSpec((MM_TM, GROUP_COLS), lambda j, i: (i, 0)),
            pl.BlockSpec((MM_TM, LRU_WIDTH), lambda j, i: (i, 0)),
            pl.BlockSpec((MM_TM, MM_TN), lambda j, i: (i, j)),
            pl.BlockSpec((MM_TM, MM_TN), lambda j, i: (i, nj + j)),
            pl.BlockSpec((GROUP_COLS, MM_TN), lambda j, i: (0, j)),
            pl.BlockSpec((LRU_WIDTH, MM_TN), lambda j, i: (0, j)),
        ],
        out_specs=pl.BlockSpec((MM_TM, MM_TN), lambda j, i: (i, j)),
        out_shape=jax.ShapeDtypeStruct((N_TOK, D_MODEL), BF16),
        scratch_shapes=[pltpu.VMEM((GROUP_COLS, MM_TN), BF16), pltpu.VMEM((LRU_WIDTH, MM_TN), BF16)],
        compiler_params=_cparams("arbitrary", "arbitrary"),
        name="mix",
    )(attn, rec, gates, gates, w_pa, w_pb)


def _epi_residual(acc, extra, outs):
    outs[0][...] = extra[0][...] + acc


def _norm2_kernel(x_ref, g_ref, xn_ref, xnt_ref):
    x = x_ref[...]
    ms = jnp.mean(x * x, axis=-1, keepdims=True)
    y = x * lax.rsqrt(ms + EPS) * g_ref[...]
    xn_ref[...] = y.astype(BF16)
    xnt_ref[...] = y.T.astype(BF16)


def _norm2(x1, g):
    return pl.pallas_call(
        _norm2_kernel,
        grid=(N_ROW_BLK,),
        in_specs=[
            pl.BlockSpec((ROW_BLK, D_MODEL), lambda i: (i, 0)),
            pl.BlockSpec((1, D_MODEL), lambda i: (0, 0)),
        ],
        out_specs=[
            pl.BlockSpec((ROW_BLK, D_MODEL), lambda i: (i, 0)),
            pl.BlockSpec((D_MODEL, ROW_BLK), lambda i: (0, i)),
        ],
        out_shape=[
            jax.ShapeDtypeStruct((N_TOK, D_MODEL), BF16),
            jax.ShapeDtypeStruct((D_MODEL, N_TOK), BF16),
        ],
        compiler_params=_cparams("arbitrary"),
        name="norm2",
    )(x1, g)


NO_RANK = float(PEER_KEYS - 1)


def _top_values(x, k, with_rank=False):
    rows = lax.broadcasted_iota(jnp.int32, x.shape, 0).astype(F32)
    rank = jnp.full(x.shape, NO_RANK, F32)
    out = []
    for j in range(k):
        m = jnp.max(x, axis=0, keepdims=True)
        first = jnp.min(jnp.where(x == m, rows, float(x.shape[0])), axis=0, keepdims=True)
        out.append(m)
        hit = rows == first
        x = jnp.where(hit, -jnp.inf, x)
        if with_rank:
            rank = jnp.where(hit, float(j), rank)
    return (out, rank) if with_rank else out


def _peer_prep_kernel(x_ref, wq_ref, k1_ref, k2_ref, e1_ref, n_ref, rank_ref, e2_ref, wq_bf):
    @pl.when(pl.program_id(1) == 0)
    def _():
        wq_bf[...] = wq_ref[...].astype(BF16)

    half = PEER_DKEY // 2
    q = jnp.dot(x_ref[...], wq_bf[...], preferred_element_type=F32).astype(BF16)
    nt = (((1,), (1,)), ((), ()))
    s1 = lax.dot_general(k1_ref[0].astype(BF16), q[:, :half], nt, preferred_element_type=F32)
    s2 = lax.dot_general(k2_ref[0].astype(BF16), q[:, half:], nt, preferred_element_type=F32)
    for c in range(PREP_TM // 128):
        cl = slice(c * 128, (c + 1) * 128)
        x1, x2 = s1[:, cl], s2[:, cl]
        a = _top_values(x1, PEER_TOPK)
        b, rank2 = _top_values(x2, PEER_TOPK, with_rank=True)
        cand = [a[i] + b[j] for i in range(PEER_TOPK) for j in range(PEER_TOPK) if (i + 1) * (j + 1) <= PEER_TOPK]
        cand += [jnp.full_like(a[0], -jnp.inf)] * (-len(cand) % 8)
        top = _top_values(jnp.concatenate(cand, axis=0), PEER_TOPK)
        tau = top[PEER_TOPK - 1]
        z = jnp.ones_like(top[0])
        for v in top[1:]:
            z = z + jnp.exp(v - top[0])
        n = jnp.zeros_like(x1)
        for bk in b:
            n = n + jnp.where(x1 + bk >= tau, 1.0, 0.0)
        e1_ref[0, :, cl] = jnp.exp(x1 - a[0]) / z
        n_ref[0, :, cl] = n
        rank_ref[0, :, cl] = rank2.astype(BF16)
        e2_ref[0, :, cl] = jnp.exp(x2 - b[0]).astype(BF16)


PREP_TM = 640


def _peer_prep(xn2, w_query, k1, k2):
    tm = PREP_TM
    half = PEER_DKEY // 2
    big = pl.BlockSpec((1, PEER_KEYS, tm), lambda h, i: (h, 0, i))
    keys = pl.BlockSpec((1, PEER_KEYS, half), lambda h, i: (h, 0, 0))
    shp = lambda dt: jax.ShapeDtypeStruct((PEER_HEADS, PEER_KEYS, N_TOK), dt)
    return pl.pallas_call(
        _peer_prep_kernel,
        grid=(PEER_HEADS, N_TOK // tm),
        in_specs=[
            pl.BlockSpec((tm, D_MODEL), lambda h, i: (i, 0)),
            pl.BlockSpec((D_MODEL, PEER_DKEY), lambda h, i: (0, h)),
            keys, keys,
        ],
        out_specs=[big, big, big, big],
        out_shape=[shp(F32), shp(F32), shp(BF16), shp(BF16)],
        scratch_shapes=[pltpu.VMEM((D_MODEL, PEER_DKEY), BF16)],
        compiler_params=_cparams("arbitrary", "arbitrary"),
        name="peer_prep",
    )(xn2, w_query, k1, k2)


I1_PER_TILE = PEER_TE // PEER_KEYS
I1_BLK = 8


def _peer_kernel(xt_ref, u_ref, vt_ref, e1_ref, n_ref, rank_ref, e2_ref, o_ref, w_scr):
    e = pl.program_id(1)

    @pl.when(e == 0)
    def _():
        o_ref[...] = jnp.zeros_like(o_ref)

    st = jnp.dot(u_ref[...], xt_ref[...], preferred_element_type=F32)
    off = (e % (I1_BLK // I1_PER_TILE)) * I1_PER_TILE
    zero = jnp.zeros((), BF16)
    for r in range(I1_PER_TILE):
        rows = slice(r * PEER_KEYS, (r + 1) * PEER_KEYS)
        coef = jnp.zeros((PEER_KEYS, PEER_TM), BF16)
        for h in range(PEER_HEADS):
            n = n_ref[h, pl.ds(off + r, 1), :].astype(BF16)
            g1 = e1_ref[h, pl.ds(off + r, 1), :].astype(BF16)
            coef = coef + g1 * jnp.where(rank_ref[h] < n, e2_ref[h], zero)
        w_scr[rows, :] = coef * _gelu(st[rows, :]).astype(BF16)
    o_ref[...] += jnp.dot(vt_ref[...], w_scr[...], preferred_element_type=F32)


def _peer(xn2t, u_bf, vt_bf, e1, n, rank2, e2):
    nt = pl.cdiv(N_TOK, PEER_TM)
    ne = PEER_EXPERTS // PEER_TE
    per = I1_BLK // I1_PER_TILE
    small = pl.BlockSpec((PEER_HEADS, I1_BLK, PEER_TM), lambda t, e: (0, e // per, t))
    big = pl.BlockSpec((PEER_HEADS, PEER_KEYS, PEER_TM), lambda t, e: (0, 0, t))
    return pl.pallas_call(
        _peer_kernel,
        grid=(nt, ne),
        in_specs=[
            pl.BlockSpec((D_MODEL, PEER_TM), lambda t, e: (0, t)),
            pl.BlockSpec((PEER_TE, D_MODEL), lambda t, e: (e, 0)),
            pl.BlockSpec((D_MODEL, PEER_TE), lambda t, e: (0, e)),
            small, small, big, big,
        ],
        out_specs=pl.BlockSpec((D_MODEL, PEER_TM), lambda t, e: (0, t)),
        out_shape=jax.ShapeDtypeStruct((D_MODEL, N_TOK), F32),
        scratch_shapes=[pltpu.VMEM((PEER_TE, PEER_TM), BF16)],
        compiler_params=_cparams("arbitrary", "arbitrary"),
        name="peer",
    )(xn2t, u_bf, vt_bf, e1, n, rank2, e2)


def _final_kernel(x_ref, pt_ref, yp_ref, ys_ref):
    i = pl.program_id(0)
    y = x_ref[...] + pt_ref[...].T

    @pl.when(i < N_PROMPT_BLK)
    def _():
        yp_ref[...] = y

    @pl.when(i == N_PROMPT_BLK)
    def _():
        ys_ref[...] = y


def _final(x1, peer_t):
    return pl.pallas_call(
        _final_kernel,
        grid=(N_ROW_BLK,),
        in_specs=[
            pl.BlockSpec((ROW_BLK, D_MODEL), lambda i: (i, 0)),
            pl.BlockSpec((D_MODEL, ROW_BLK), lambda i: (0, i)),
        ],
        out_specs=[
            pl.BlockSpec((ROW_BLK, D_MODEL), lambda i: (jnp.minimum(i, N_PROMPT_BLK - 1), 0)),
            pl.BlockSpec((ROW_BLK, D_MODEL), lambda i: (0, 0)),
        ],
        out_shape=[
            jax.ShapeDtypeStruct((N_PROMPT, D_MODEL), F32),
            jax.ShapeDtypeStruct((DEC_BATCH, D_MODEL), F32),
        ],
        compiler_params=_cparams("arbitrary"),
        name="final",
    )(x1, peer_t)


def _bucket(dist):
    max_exact = N_BUCKETS // 2
    dd = jnp.maximum(dist, max_exact).astype(F32)
    large = max_exact + (jnp.log(dd / max_exact) / math.log(RP_MAX_DIST / max_exact)
                         * (N_BUCKETS - max_exact)).astype(jnp.int32)
    large = jnp.minimum(large, N_BUCKETS - 1)
    return jnp.where(dist < max_exact, dist, large)


def _group_bias(rel_bias, g, d):
    dist = d * jnp.arange(N_KEYS, dtype=jnp.int32)
    bias = rel_bias[:, g * HEADS_PER_GROUP:(g + 1) * HEADS_PER_GROUP][_bucket(dist)]
    pad = jnp.full((HEADS_PER_GROUP, 2 * ROW_BLK - N_KEYS), NEG, F32)
    ext = jnp.concatenate([bias[::-1].T, pad], axis=1)
    return bias, ext[:, None, :]


def kernel(x_prompt, x_sample, cache_k_w128, cache_v_w128, cache_k_w512, cache_v_w512, cache_k_w2048,
           cache_v_w2048, state_lru_h, state_conv, rel_bias, ln1_g, w_in, q_norm_g, k_norm_g, conv_w,
           conv_b, w_rg, b_rg, w_ig, b_ig, lru_L, w_pa, w_pb, w_o, ln2_g, w_query, sub_keys_1,
           sub_keys_2, peer_u, peer_v):
    caches = ((cache_k_w128, cache_v_w128), (cache_k_w512, cache_v_w512), (cache_k_w2048, cache_v_w2048))
    xp = x_prompt.reshape(N_PROMPT, D_MODEL)
    xs = x_sample.reshape(DEC_BATCH, D_MODEL)
    w = w_in[0]

    xn, x_all = _norm1(xp, xs, ln1_g)
    gspec = [pl.BlockSpec((1, HEAD_DIM), lambda j, i: (0, 0))]
    wide = dict(tm=IN_TM, tn=IN_TN)
    (q32,) = _proj(xn, w, Q_OFF, ATTN_WIDTH, _epi_q, (F32,), (q_norm_g,), gspec, **wide)
    (k32,) = _proj(xn, w, K_OFF, ATTN_WIDTH, _epi_k, (F32,), (k_norm_g,), gspec, **wide)
    (v32,) = _proj(xn, w, V_OFF, ATTN_WIDTH, _epi_f32, (F32,), **wide)
    (xg,) = _proj(xn, w, XB_OFF, 2 * LRU_WIDTH, _epi_f32, (F32,), **wide)
    (gates,) = _proj(xn, w, GA_OFF, 2 * D_MODEL, _epi_sigmoid, (BF16,), **wide)

    biases, exts = zip(*[_group_bias(rel_bias, g, d) for g, (_, d) in enumerate(ATTN_GROUPS)])
    attn_p = _attn_prompt(q32, k32, v32, jnp.concatenate(exts, axis=0))
    attn_s = _attn_sample(q32, k32, v32, [(ck[0], cv[0]) for ck, cv in caches], biases)
    attn = jnp.concatenate([attn_p, attn_s], axis=0)

    rec_p, h_p = _rec_prompt(xg, conv_w[0], conv_b, w_rg[0], b_rg, w_ig[0], b_ig, lru_L)
    rec_s, h_s = _rec_sample(xg, state_conv[0], state_lru_h[0], conv_w[0], conv_b, w_rg[0], b_rg,
                             w_ig[0], b_ig, lru_L)
    rec = jnp.concatenate([rec_p, rec_s], axis=0)

    merged = _mix(attn, rec, gates, w_pa[0], w_pb[0])
    (x1,) = _proj(merged, w_o[0], 0, D_MODEL, _epi_residual, (F32,), (x_all,),
                  [pl.BlockSpec((MM_TM, MM_TN), lambda j, i: (i, j))])

    xn2, xn2t = _norm2(x1, ln2_g)
    e1, nsel, rank2, e2 = _peer_prep(xn2, w_query[0], sub_keys_1[0], sub_keys_2[0])
    peer_t = _peer(xn2t, peer_u[0].astype(BF16), peer_v[0].T.astype(BF16), e1, nsel, rank2, e2)
    y_p, y_s = _final(x1, peer_t)

    outs = [y_p.reshape(BATCH, SEQ, D_MODEL), y_s.reshape(DEC_BATCH, 1, D_MODEL)]
    for g, (window, d) in enumerate(ATTN_GROUPS):
        keep = min(window, SEQ)
        cols = slice(g * GROUP_COLS, (g + 1) * GROUP_COLS)
        for a in (k32, v32):
            tail = a[:N_PROMPT].reshape(BATCH, SEQ, ATTN_WIDTH)[:, SEQ - keep:, cols]
            outs.append(tail.reshape(1, BATCH, keep, HEADS_PER_GROUP, HEAD_DIM))
        for a in (k32, v32):
            outs.append(a[N_PROMPT:, cols].reshape(1, DEC_BATCH, 1, HEADS_PER_GROUP, HEAD_DIM))
    xb_p = xg[:N_PROMPT, :LRU_WIDTH].reshape(BATCH, SEQ, LRU_WIDTH)
    xb_s = xg[N_PROMPT:, :LRU_WIDTH]
    outs += [
        h_p[None], h_s[None],
        xb_p[:, SEQ - (CONV_WIDTH - 1):][None],
        jnp.concatenate([state_conv[0][:, 1:], xb_s[:, None, :]], axis=1)[None],
    ]
    return tuple(outs)
```

```python
import functools
import math

import jax
import jax.numpy as jnp
import numpy as np
from jax import lax
from jax.experimental import pallas as pl
from jax.experimental.pallas import tpu as pltpu

F32 = jnp.float32
BF16 = jnp.bfloat16

D_MODEL = 4096
BATCH = 4
SEQ = 2048
DEC_BATCH = 128
N_PROMPT = BATCH * SEQ
N_TOK = N_PROMPT + DEC_BATCH
HEAD_DIM = 128
HEADS_PER_GROUP = 8
GROUP_COLS = HEADS_PER_GROUP * HEAD_DIM
ATTN_GROUPS = ((128, 1), (512, 4), (2048, 16))
N_GROUPS = 3
ATTN_WIDTH = N_GROUPS * GROUP_COLS
N_KEYS = 129
N_BUCKETS = 32
RP_MAX_DIST = 2048
LRU_WIDTH = 2048
LRU_BLOCKS = 16
LRU_BLOCK = 128
CONV_WIDTH = 4
LRU_C = 8.0
PEER_HEADS = 8
PEER_KEYS = 128
PEER_EXPERTS = PEER_KEYS * PEER_KEYS
PEER_DKEY = 256
PEER_TOPK = 16
EPS = 1e-6

Q_OFF = 0
K_OFF = ATTN_WIDTH
V_OFF = 2 * ATTN_WIDTH
XB_OFF = 3 * ATTN_WIDTH
GA_OFF = XB_OFF + 2 * LRU_WIDTH

ROW_BLK = 128
N_PROMPT_BLK = N_PROMPT // ROW_BLK
N_ROW_BLK = N_TOK // ROW_BLK
MM_TM = 1040
MM_TN = 512
IN_TM = 416
IN_TN = 1024
PEER_TM = 512
PEER_TE = 512
NEG = -1e30
VMEM_LIMIT = 58 * 1024 * 1024


def _cparams(*sem):
    return pltpu.CompilerParams(dimension_semantics=sem, vmem_limit_bytes=VMEM_LIMIT)


def _gelu(x):
    return jax.nn.gelu(x)


def _sigmoid(x):
    return 1.0 / (1.0 + jnp.exp(-x))


def _norm1_kernel(xp_ref, xs_ref, g_ref, xn_ref, xall_ref):
    i = pl.program_id(0)

    def body(x):
        ms = jnp.mean(x * x, axis=-1, keepdims=True)
        xn_ref[...] = (x * lax.rsqrt(ms + EPS) * g_ref[...]).astype(BF16)
        xall_ref[...] = x

    @pl.when(i < N_PROMPT_BLK)
    def _():
        body(xp_ref[...])

    @pl.when(i == N_PROMPT_BLK)
    def _():
        body(xs_ref[...])


def _norm1(xp, xs, g):
    return pl.pallas_call(
        _norm1_kernel,
        grid=(N_ROW_BLK,),
        in_specs=[
            pl.BlockSpec((ROW_BLK, D_MODEL), lambda i: (jnp.minimum(i, N_PROMPT_BLK - 1), 0)),
            pl.BlockSpec((ROW_BLK, D_MODEL), lambda i: (0, 0)),
            pl.BlockSpec((1, D_MODEL), lambda i: (0, 0)),
        ],
        out_specs=[
            pl.BlockSpec((ROW_BLK, D_MODEL), lambda i: (i, 0)),
            pl.BlockSpec((ROW_BLK, D_MODEL), lambda i: (i, 0)),
        ],
        out_shape=[
            jax.ShapeDtypeStruct((N_TOK, D_MODEL), BF16),
            jax.ShapeDtypeStruct((N_TOK, D_MODEL), F32),
        ],
        compiler_params=_cparams("arbitrary"),
        name="norm1",
    )(xp, xs, g)


def _proj_kernel(*refs, n_extra, epilogue):
    x_ref, w_ref = refs[0], refs[1]
    extra = refs[2:2 + n_extra]
    outs = refs[2 + n_extra:-1]
    wbf_ref = refs[-1]

    @pl.when(pl.program_id(1) == 0)
    def _():
        wbf_ref[...] = w_ref[...].astype(BF16)

    acc = jnp.dot(x_ref[...], wbf_ref[...], preferred_element_type=F32)
    epilogue(acc, extra, outs)


def _proj(x, w, col0, ncols, epilogue, out_dtypes, extra=(), extra_specs=(), tm=MM_TM, tn=MM_TN,
          head_major=False):
    m, k = x.shape
    grid = (ncols // tn, m // tm)
    cb0 = col0 // tn
    kern = functools.partial(_proj_kernel, n_extra=len(extra), epilogue=epilogue)
    out_specs = [pl.BlockSpec((tm, tn), lambda j, i: (i, j)) for _ in out_dtypes]
    out_shape = [jax.ShapeDtypeStruct((m, ncols), dt) for dt in out_dtypes]
    if head_major:
        assert tn == GROUP_COLS
        out_specs.append(pl.BlockSpec((None, tm, HEADS_PER_GROUP, HEAD_DIM), lambda j, i: (j, i, 0, 0)))
        out_shape.append(jax.ShapeDtypeStruct((ncols // tn, m, HEADS_PER_GROUP, HEAD_DIM), F32))
    return pl.pallas_call(
        kern,
        grid=grid,
        in_specs=[
            pl.BlockSpec((tm, k), lambda j, i: (i, 0)),
            pl.BlockSpec((k, tn), lambda j, i: (0, cb0 + j)),
            *extra_specs,
        ],
        out_specs=out_specs,
        out_shape=out_shape,
        scratch_shapes=[pltpu.VMEM((k, tn), BF16)],
        compiler_params=_cparams("arbitrary", "arbitrary"),
        name="proj",
    )(x, w, *extra)


def _head_norm(a, g):
    ms = jnp.mean(a * a, axis=-1, keepdims=True)
    return a * lax.rsqrt(ms + EPS) * g


def _epi_q(acc, extra, outs):
    g = extra[0][...]
    for h in range(acc.shape[1] // HEAD_DIM):
        sl = slice(h * HEAD_DIM, (h + 1) * HEAD_DIM)
        outs[0][:, sl] = _head_norm(acc[:, sl], g) * (HEAD_DIM ** -0.5)


def _epi_k(acc, extra, outs):
    g = extra[0][...]
    for h in range(acc.shape[1] // HEAD_DIM):
        sl = slice(h * HEAD_DIM, (h + 1) * HEAD_DIM)
        y = _head_norm(acc[:, sl], g)
        outs[0][:, sl] = y
        outs[1][:, h, :] = y


def _epi_v(acc, extra, outs):
    outs[0][...] = acc
    for h in range(acc.shape[1] // HEAD_DIM):
        outs[1][:, h, :] = acc[:, h * HEAD_DIM:(h + 1) * HEAD_DIM]


def _epi_f32(acc, extra, outs):
    outs[0][...] = acc


def _epi_sigmoid(acc, extra, outs):
    outs[0][...] = _sigmoid(acc).astype(BF16)


def _attn_p_kernel(*refs):
    q_refs, k_refs, v_refs, ext_refs = refs[0:3], refs[3:6], refs[6:9], refs[9:12]
    out_ref = refs[12]
    o_scr, l_scr = refs[13:16], refs[16:19]
    nt = (((1,), (1,)), ((), ()))

    for g, (_, d) in enumerate(ATTN_GROUPS):
        q_ref, k_ref, v_ref = q_refs[g], k_refs[g], v_refs[g]
        band = pltpu.roll(jnp.broadcast_to(ext_refs[g][0], (ROW_BLK, 2 * ROW_BLK)), 0, 1,
                          stride=1, stride_axis=0)
        band_cur = band[:, ROW_BLK:]
        nblk = SEQ // d // ROW_BLK

        def rows(r, i, d=d):
            start = r + (d * ROW_BLK) * i
            return pl.ds(start, ROW_BLK) if d == 1 else pl.ds(start, ROW_BLK, stride=d)

        def block(r, i, first, g=g, q_ref=q_ref, k_ref=k_ref, v_ref=v_ref, band=band,
                  band_cur=band_cur, rows=rows):
            cur = rows(r, i)
            q = q_ref[cur, :].astype(BF16)
            if first:
                kk = k_ref[cur, :].astype(BF16)
                vv = v_ref[cur, :].astype(BF16)
                bias = band_cur
            else:
                prev = rows(r, i - 1)
                kk = jnp.concatenate([k_ref[prev, :], k_ref[cur, :]], axis=0).astype(BF16)
                vv = jnp.concatenate([v_ref[prev, :], v_ref[cur, :]], axis=0).astype(BF16)
                bias = band
            lg = lax.dot_general(q, kk, nt, preferred_element_type=F32) + bias
            m = jnp.max(lg, axis=-1, keepdims=True)
            p = jnp.exp(lg - m)
            s = jnp.sum(p, axis=-1, keepdims=True)
            o_scr[g][cur, :] = jnp.dot(p.astype(BF16), vv, preferred_element_type=F32) / s
            l_scr[g][cur, :] = jnp.broadcast_to(m + jnp.log(s), (ROW_BLK, HEAD_DIM))

        if d == 1:
            block(0, 0, True)
            per_trip = 5

            def trip(t, carry, block=block):
                for u in range(per_trip):
                    block(0, 1 + per_trip * t + u, False)
                return carry

            lax.fori_loop(0, (nblk - 1) // per_trip, trip, 0)
        elif nblk > 1:
            def trip(t, carry, block=block, nblk=nblk):
                for r in (2 * t, 2 * t + 1):
                    block(r, 0, True)
                    for i in range(1, nblk):
                        block(r, i, False)
                return carry

            lax.fori_loop(0, d // 2, trip, 0)
        else:
            per_trip = 8

            def trip(t, carry, block=block):
                for u in range(per_trip):
                    block(per_trip * t + u, 0, True)
                return carry

            lax.fori_loop(0, d // per_trip, trip, 0)

    l0, l1, l2 = l_scr[0][...], l_scr[1][...], l_scr[2][...]
    m = jnp.maximum(jnp.maximum(l0, l1), l2)
    e0, e1, e2 = jnp.exp(l0 - m), jnp.exp(l1 - m), jnp.exp(l2 - m)
    acc = e0 * o_scr[0][...] + e1 * o_scr[1][...] + e2 * o_scr[2][...]
    out_ref[...] = (acc / (e0 + e1 + e2)).astype(BF16)


def _attn_prompt(q32, k32, v32, ext):
    def col(g):
        return pl.BlockSpec((SEQ, HEAD_DIM), lambda b, h: (b, g * HEADS_PER_GROUP + h))

    def erow(g):
        return pl.BlockSpec((1, 1, 2 * ROW_BLK), lambda b, h: (g * HEADS_PER_GROUP + h, 0, 0))

    groups = range(N_GROUPS)
    return pl.pallas_call(
        _attn_p_kernel,
        grid=(BATCH, HEADS_PER_GROUP),
        in_specs=[col(g) for g in groups] * 3 + [erow(g) for g in groups],
        out_specs=pl.BlockSpec((SEQ, HEAD_DIM), lambda b, h: (b, h)),
        out_shape=jax.ShapeDtypeStruct((N_PROMPT, GROUP_COLS), BF16),
        scratch_shapes=[pltpu.VMEM((SEQ, HEAD_DIM), F32) for _ in range(2 * N_GROUPS)],
        compiler_params=_cparams("arbitrary", "arbitrary"),
        name="attn_p",
    )(q32, q32, q32, k32, k32, k32, v32, v32, v32, ext, ext, ext)


SAMPLE_BB = 4


def _attn_s_kernel(*refs):
    q_ref, kn_ref, vn_ref = refs[0:3]
    ck_refs, cv_refs = refs[3:6], refs[6:9]
    bc_ref, b0_ref, o_ref = refs[9], refs[10], refs[11]

    def one(bi, carry):
        outs, lses = [], []
        for g in range(N_GROUPS):
            hs = slice(g * HEADS_PER_GROUP, (g + 1) * HEADS_PER_GROUP)
            q = q_ref[bi, hs, :]
            lc = jnp.sum(ck_refs[g][bi] * q[None], axis=-1, keepdims=True) + bc_ref[g]
            ln = jnp.sum(kn_ref[bi, hs, :] * q, axis=-1, keepdims=True) + b0_ref[g]
            m = jnp.maximum(jnp.max(lc, axis=0), ln)
            p = jnp.exp(lc - m[None])
            pn = jnp.exp(ln - m)
            s = jnp.sum(p, axis=0) + pn
            outs.append((jnp.sum(p * cv_refs[g][bi], axis=0) + pn * vn_ref[bi, hs, :]) / s)
            lses.append(m + jnp.log(s))
        m = jnp.maximum(jnp.maximum(lses[0], lses[1]), lses[2])
        e = [jnp.exp(l - m) for l in lses]
        acc = e[0] * outs[0] + e[1] * outs[1] + e[2] * outs[2]
        o_ref[bi] = acc / (e[0] + e[1] + e[2])
        return carry

    lax.fori_loop(0, SAMPLE_BB, one, 0)


def _attn_sample(q32, k32, v32, caches, biases):
    heads3 = lambda a: a[N_PROMPT:].reshape(DEC_BATCH, N_GROUPS * HEADS_PER_GROUP, HEAD_DIM)
    new = pl.BlockSpec((SAMPLE_BB, N_GROUPS * HEADS_PER_GROUP, HEAD_DIM), lambda s: (s, 0, 0))
    bc = jnp.stack([b[::-1][:N_KEYS - 1] for b in biases])[..., None]
    b0 = jnp.stack([b[0] for b in biases])[..., None]
    cache_specs, cache_args = [], []
    for which in range(2):
        for g, (window, d) in enumerate(ATTN_GROUPS):
            c = caches[g][which]
            cache_args.append(c.reshape(DEC_BATCH, c.shape[1] // d, d, HEADS_PER_GROUP, HEAD_DIM))
            cache_specs.append(pl.BlockSpec((SAMPLE_BB, N_KEYS - 1, None, HEADS_PER_GROUP, HEAD_DIM),
                                            lambda s: (s, 0, 0, 0, 0)))
    o = pl.pallas_call(
        _attn_s_kernel,
        grid=(DEC_BATCH // SAMPLE_BB,),
        in_specs=[
            new, new, new, *cache_specs,
            pl.BlockSpec((N_GROUPS, N_KEYS - 1, HEADS_PER_GROUP, 1), lambda s: (0, 0, 0, 0)),
            pl.BlockSpec((N_GROUPS, HEADS_PER_GROUP, 1), lambda s: (0, 0, 0)),
        ],
        out_specs=pl.BlockSpec((SAMPLE_BB, HEADS_PER_GROUP, HEAD_DIM), lambda s: (s, 0, 0)),
        out_shape=jax.ShapeDtypeStruct((DEC_BATCH, HEADS_PER_GROUP, HEAD_DIM), F32),
        compiler_params=_cparams("arbitrary"),
        name="attn_s",
    )(heads3(q32), heads3(k32), heads3(v32), *cache_args, bc, b0)
    return o.reshape(DEC_BATCH, GROUP_COLS).astype(BF16)


SCAN_CHUNK = 64


def _softplus(x):
    return jnp.maximum(x, 0.0) + jnp.log1p(jnp.exp(-jnp.abs(x)))


def _gates(conv, wr_ref, br_ref, wi_ref, bi_ref, l_ref):
    cb = conv.astype(BF16)
    r = _sigmoid(jnp.dot(cb, wr_ref[0].astype(BF16), preferred_element_type=F32) + br_ref[...])
    ig = _sigmoid(jnp.dot(cb, wi_ref[0].astype(BF16), preferred_element_type=F32) + bi_ref[...])
    log_a = -LRU_C * r * _softplus(-l_ref[...])
    return log_a, ig


def _decay(log_a):
    a = jnp.exp(log_a)
    return a, jnp.sqrt(-jnp.tanh(log_a) * (a * a + 1.0))


def _rec_p_kernel(xb_ref, gb_ref, cw_ref, cb_ref, wr_ref, br_ref, wi_ref, bi_ref, l_ref,
                  y_ref, h_ref, a_scr, b_scr):
    t = SEQ
    x = xb_ref[...]
    row = lax.broadcasted_iota(jnp.int32, (t, LRU_BLOCK), 0)
    conv = cb_ref[...] + x * cw_ref[CONV_WIDTH - 1:CONV_WIDTH, :]
    for s in range(1, CONV_WIDTH):
        xs = jnp.where(row >= s, pltpu.roll(x, s, 0), 0.0)
        conv = conv + xs * cw_ref[CONV_WIDTH - 1 - s:CONV_WIDTH - s, :]
    log_a, ig = _gates(conv, wr_ref, br_ref, wi_ref, bi_ref, l_ref)
    a, mult = _decay(log_a)
    start = row == 0
    a_scr[...] = jnp.where(start, 0.0, a)
    b_scr[...] = jnp.where(start, 1.0, mult) * ig * conv

    crow = lax.broadcasted_iota(jnp.int32, (SCAN_CHUNK, LRU_BLOCK), 0)

    def chunk(c, h_prev):
        rows = pl.ds(pl.multiple_of(c * SCAN_CHUNK, SCAN_CHUNK), SCAN_CHUNK)
        a_c = a_scr[rows, :]
        b_c = b_scr[rows, :]
        s = 1
        while s < SCAN_CHUNK:
            keep = crow >= s
            a_sh = jnp.where(keep, pltpu.roll(a_c, s, 0), 1.0)
            b_sh = jnp.where(keep, pltpu.roll(b_c, s, 0), 0.0)
            b_c = a_c * b_sh + b_c
            a_c = a_c * a_sh
            s *= 2
        h = a_c * h_prev + b_c
        y_ref[rows, :] = (h * _gelu(gb_ref[rows, :])).astype(BF16)
        return h[SCAN_CHUNK - 1:SCAN_CHUNK, :]

    h_last = lax.fori_loop(0, t // SCAN_CHUNK, chunk, jnp.zeros((1, LRU_BLOCK), F32))
    h_ref[0] = h_last


def _rec_prompt(xg, conv_w, conv_b, w_rg, b_rg, w_ig, b_ig, lru_l):
    nb = LRU_BLOCKS
    vec = pl.BlockSpec((1, LRU_BLOCK), lambda b, n: (0, n))
    wblk = pl.BlockSpec((1, LRU_BLOCK, LRU_BLOCK), lambda b, n: (n, 0, 0))
    y, h = pl.pallas_call(
        _rec_p_kernel,
        grid=(BATCH, nb),
        in_specs=[
            pl.BlockSpec((SEQ, LRU_BLOCK), lambda b, n: (b, n)),
            pl.BlockSpec((SEQ, LRU_BLOCK), lambda b, n: (b, nb + n)),
            pl.BlockSpec((CONV_WIDTH, LRU_BLOCK), lambda b, n: (0, n)),
            vec, wblk, vec, wblk, vec, vec,
        ],
        out_specs=[
            pl.BlockSpec((SEQ, LRU_BLOCK), lambda b, n: (b, n)),
            pl.BlockSpec((1, 1, LRU_BLOCK), lambda b, n: (b, 0, n)),
        ],
        out_shape=[
            jax.ShapeDtypeStruct((N_PROMPT, LRU_WIDTH), BF16),
            jax.ShapeDtypeStruct((BATCH, 1, LRU_WIDTH), F32),
        ],
        scratch_shapes=[pltpu.VMEM((SEQ, LRU_BLOCK), F32), pltpu.VMEM((SEQ, LRU_BLOCK), F32)],
        compiler_params=_cparams("arbitrary", "arbitrary"),
        name="rec_p",
    )(xg, xg, conv_w, conv_b, w_rg, b_rg, w_ig, b_ig, lru_l)
    return y, h.reshape(BATCH, LRU_WIDTH)


def _rec_s_kernel(xb_ref, gb_ref, c0_ref, c1_ref, c2_ref, h0_ref, cw_ref, cb_ref,
                  wr_ref, br_ref, wi_ref, bi_ref, l_ref, y_ref, h_ref):
    conv = (cb_ref[...] + c0_ref[...] * cw_ref[0:1, :] + c1_ref[...] * cw_ref[1:2, :]
            + c2_ref[...] * cw_ref[2:3, :] + xb_ref[...] * cw_ref[3:4, :])
    log_a, ig = _gates(conv, wr_ref, br_ref, wi_ref, bi_ref, l_ref)
    a, mult = _decay(log_a)
    h = a * h0_ref[...] + mult * ig * conv
    h_ref[...] = h
    y_ref[...] = (h * _gelu(gb_ref[...])).astype(BF16)


def _rec_sample(xg, state_conv, h0, conv_w, conv_b, w_rg, b_rg, w_ig, b_ig, lru_l):
    nb = LRU_BLOCKS
    sc = state_conv.reshape(DEC_BATCH, (CONV_WIDTH - 1) * LRU_WIDTH)
    srow = N_PROMPT // DEC_BATCH
    vec = pl.BlockSpec((1, LRU_BLOCK), lambda n: (0, n))
    wblk = pl.BlockSpec((1, LRU_BLOCK, LRU_BLOCK), lambda n: (n, 0, 0))
    tile = lambda f: pl.BlockSpec((DEC_BATCH, LRU_BLOCK), f)
    return pl.pallas_call(
        _rec_s_kernel,
        grid=(nb,),
        in_specs=[
            tile(lambda n: (srow, n)),
            tile(lambda n: (srow, nb + n)),
            tile(lambda n: (0, n)),
            tile(lambda n: (0, nb + n)),
            tile(lambda n: (0, 2 * nb + n)),
            tile(lambda n: (0, n)),
            pl.BlockSpec((CONV_WIDTH, LRU_BLOCK), lambda n: (0, n)),
            vec, wblk, vec, wblk, vec, vec,
        ],
        out_specs=[tile(lambda n: (0, n)), tile(lambda n: (0, n))],
        out_shape=[
            jax.ShapeDtypeStruct((DEC_BATCH, LRU_WIDTH), BF16),
            jax.ShapeDtypeStruct((DEC_BATCH, LRU_WIDTH), F32),
        ],
        compiler_params=_cparams("arbitrary"),
        name="rec_s",
    )(xg, xg, sc, sc, sc, h0, conv_w, conv_b, w_rg, b_rg, w_ig, b_ig, lru_l)


def _mix_kernel(at_ref, rc_ref, ga_ref, gr_ref, wa_ref, wb_ref, o_ref, wa_bf, wb_bf):
    @pl.when(pl.program_id(1) == 0)
    def _():
        wa_bf[...] = wa_ref[...].astype(BF16)
        wb_bf[...] = wb_ref[...].astype(BF16)

    pa = jnp.dot(at_ref[...], wa_bf[...], preferred_element_type=F32)
    pb = jnp.dot(rc_ref[...], wb_bf[...], preferred_element_type=F32)
    o_ref[...] = (ga_ref[...].astype(F32) * pa + gr_ref[...].astype(F32) * pb).astype(BF16)


def _mix(attn, rec, gates, w_pa, w_pb):
    nj = D_MODEL // MM_TN
    return pl.pallas_call(
        _mix_kernel,
        grid=(nj, N_TOK // MM_TM),
        in_specs=[
            pl.BlockSpec((MM_TM, GROUP_COLS), lambda j, i: (i, 0)),
            pl.BlockSpec((MM_TM, LRU_WIDTH), lambda j, i: (i, 0)),
            pl.BlockSpec((MM_TM, MM_TN), lambda j, i: (i, j)),
            pl.BlockSpec((MM_TM, MM_TN), lambda j, i: (i, nj + j)),
            pl.BlockSpec((GROUP_COLS, MM_TN), lambda j, i: (0, j)),
            pl.BlockSpec((LRU_WIDTH, MM_TN), lambda j, i: (0, j)),
        ],
        out_specs=pl.BlockSpec((MM_TM, MM_TN), lambda j, i: (i, j)),
        out_shape=jax.ShapeDtypeStruct((N_TOK, D_MODEL), BF16),
        scratch_shapes=[pltpu.VMEM((GROUP_COLS, MM_TN), BF16), pltpu.VMEM((LRU_WIDTH, MM_TN), BF16)],
        compiler_params=_cparams("arbitrary", "arbitrary"),
        name="mix",
    )(attn, rec, gates, gates, w_pa, w_pb)


def _epi_residual(acc, extra, outs):
    outs[0][...] = extra[0][...] + acc


def _norm2_kernel(x_ref, g_ref, xn_ref, xnt_ref):
    x = x_ref[...]
    ms = jnp.mean(x * x, axis=-1, keepdims=True)
    y = x * lax.rsqrt(ms + EPS) * g_ref[...]
    xn_ref[...] = y.astype(BF16)
    xnt_ref[...] = y.T.astype(BF16)


def _norm2(x1, g):
    return pl.pallas_call(
        _norm2_kernel,
        grid=(N_ROW_BLK,),
        in_specs=[
            pl.BlockSpec((ROW_BLK, D_MODEL), lambda i: (i, 0)),
            pl.BlockSpec((1, D_MODEL), lambda i: (0, 0)),
        ],
        out_specs=[
            pl.BlockSpec((ROW_BLK, D_MODEL), lambda i: (i, 0)),
            pl.BlockSpec((D_MODEL, ROW_BLK), lambda i: (0, i)),
        ],
        out_shape=[
            jax.ShapeDtypeStruct((N_TOK, D_MODEL), BF16),
            jax.ShapeDtypeStruct((D_MODEL, N_TOK), BF16),
        ],
        compiler_params=_cparams("arbitrary"),
        name="norm2",
    )(x1, g)


NO_RANK = float(PEER_KEYS - 1)


def _top_values(x, k, with_rank=False):
    rows = lax.broadcasted_iota(jnp.int32, x.shape, 0).astype(F32)
    rank = jnp.full(x.shape, NO_RANK, F32)
    out = []
    for j in range(k):
        m = jnp.max(x, axis=0, keepdims=True)
        first = jnp.min(jnp.where(x == m, rows, float(x.shape[0])), axis=0, keepdims=True)
        out.append(m)
        hit = rows == first
        x = jnp.where(hit, -jnp.inf, x)
        if with_rank:
            rank = jnp.where(hit, float(j), rank)
    return (out, rank) if with_rank else out


def _peer_prep_kernel(x_ref, wq_ref, k1_ref, k2_ref, e1_ref, n_ref, rank_ref, e2_ref, wq_bf):
    @pl.when(pl.program_id(1) == 0)
    def _():
        wq_bf[...] = wq_ref[...].astype(BF16)

    half = PEER_DKEY // 2
    q = jnp.dot(x_ref[...], wq_bf[...], preferred_element_type=F32).astype(BF16)
    nt = (((1,), (1,)), ((), ()))
    s1 = lax.dot_general(k1_ref[0].astype(BF16), q[:, :half], nt, preferred_element_type=F32)
    s2 = lax.dot_general(k2_ref[0].astype(BF16), q[:, half:], nt, preferred_element_type=F32)
    for c in range(PREP_TM // 128):
        cl = slice(c * 128, (c + 1) * 128)
        x1, x2 = s1[:, cl], s2[:, cl]
        a = _top_values(x1, PEER_TOPK)
        b, rank2 = _top_values(x2, PEER_TOPK, with_rank=True)
        cand = [a[i] + b[j] for i in range(PEER_TOPK) for j in range(PEER_TOPK) if (i + 1) * (j + 1) <= PEER_TOPK]
        cand += [jnp.full_like(a[0], -jnp.inf)] * (-len(cand) % 8)
        top = _top_values(jnp.concatenate(cand, axis=0), PEER_TOPK)
        tau = top[PEER_TOPK - 1]
        z = jnp.ones_like(top[0])
        for v in top[1:]:
            z = z + jnp.exp(v - top[0])
        n = jnp.zeros_like(x1)
        for bk in b:
            n = n + jnp.where(x1 + bk >= tau, 1.0, 0.0)
        e1_ref[0, :, cl] = jnp.exp(x1 - a[0]) / z
        n_ref[0, :, cl] = n
        rank_ref[0, :, cl] = rank2.astype(BF16)
        e2_ref[0, :, cl] = jnp.exp(x2 - b[0]).astype(BF16)


PREP_TM = 640


def _peer_prep(xn2, w_query, k1, k2):
    tm = PREP_TM
    half = PEER_DKEY // 2
    big = pl.BlockSpec((1, PEER_KEYS, tm), lambda h, i: (h, 0, i))
    keys = pl.BlockSpec((1, PEER_KEYS, half), lambda h, i: (h, 0, 0))
    shp = lambda dt: jax.ShapeDtypeStruct((PEER_HEADS, PEER_KEYS, N_TOK), dt)
    return pl.pallas_call(
        _peer_prep_kernel,
        grid=(PEER_HEADS, N_TOK // tm),
        in_specs=[
            pl.BlockSpec((tm, D_MODEL), lambda h, i: (i, 0)),
            pl.BlockSpec((D_MODEL, PEER_DKEY), lambda h, i: (0, h)),
            keys, keys,
        ],
        out_specs=[big, big, big, big],
        out_shape=[shp(F32), shp(F32), shp(BF16), shp(BF16)],
        scratch_shapes=[pltpu.VMEM((D_MODEL, PEER_DKEY), BF16)],
        compiler_params=_cparams("arbitrary", "arbitrary"),
        name="peer_prep",
    )(xn2, w_query, k1, k2)


I1_PER_TILE = PEER_TE // PEER_KEYS
I1_BLK = 8


def _peer_kernel(xt_ref, u_ref, vt_ref, e1_ref, n_ref, rank_ref, e2_ref, o_ref, w_scr):
    e = pl.program_id(1)

    @pl.when(e == 0)
    def _():
        o_ref[...] = jnp.zeros_like(o_ref)

    st = jnp.dot(u_ref[...], xt_ref[...], preferred_element_type=F32)
    off = (e % (I1_BLK // I1_PER_TILE)) * I1_PER_TILE
    zero = jnp.zeros((), BF16)
    for r in range(I1_PER_TILE):
        rows = slice(r * PEER_KEYS, (r + 1) * PEER_KEYS)
        coef = jnp.zeros((PEER_KEYS, PEER_TM), BF16)
        for h in range(PEER_HEADS):
            n = n_ref[h, pl.ds(off + r, 1), :].astype(BF16)
            g1 = e1_ref[h, pl.ds(off + r, 1), :].astype(BF16)
            coef = coef + g1 * jnp.where(rank_ref[h] < n, e2_ref[h], zero)
        w_scr[rows, :] = coef * _gelu(st[rows, :]).astype(BF16)
    o_ref[...] += jnp.dot(vt_ref[...], w_scr[...], preferred_element_type=F32)


def _peer(xn2t, u_bf, vt_bf, e1, n, rank2, e2):
    nt = pl.cdiv(N_TOK, PEER_TM)
    ne = PEER_EXPERTS // PEER_TE
    per = I1_BLK // I1_PER_TILE
    small = pl.BlockSpec((PEER_HEADS, I1_BLK, PEER_TM), lambda t, e: (0, e // per, t))
    big = pl.BlockSpec((PEER_HEADS, PEER_KEYS, PEER_TM), lambda t, e: (0, 0, t))
    return pl.pallas_call(
        _peer_kernel,
        grid=(nt, ne),
        in_specs=[
            pl.BlockSpec((D_MODEL, PEER_TM), lambda t, e: (0, t)),
            pl.BlockSpec((PEER_TE, D_MODEL), lambda t, e: (e, 0)),
            pl.BlockSpec((D_MODEL, PEER_TE), lambda t, e: (0, e)),
            small, small, big, big,
        ],
        out_specs=pl.BlockSpec((D_MODEL, PEER_TM), lambda t, e: (0, t)),
        out_shape=jax.ShapeDtypeStruct((D_MODEL, N_TOK), F32),
        scratch_shapes=[pltpu.VMEM((PEER_TE, PEER_TM), BF16)],
        compiler_params=_cparams("arbitrary", "arbitrary"),
        name="peer",
    )(xn2t, u_bf, vt_bf, e1, n, rank2, e2)


def _final_kernel(x_ref, pt_ref, yp_ref, ys_ref):
    i = pl.program_id(0)
    y = x_ref[...] + pt_ref[...].T

    @pl.when(i < N_PROMPT_BLK)
    def _():
        yp_ref[...] = y

    @pl.when(i == N_PROMPT_BLK)
    def _():
        ys_ref[...] = y


def _final(x1, peer_t):
    return pl.pallas_call(
        _final_kernel,
        grid=(N_ROW_BLK,),
        in_specs=[
            pl.BlockSpec((ROW_BLK, D_MODEL), lambda i: (i, 0)),
            pl.BlockSpec((D_MODEL, ROW_BLK), lambda i: (0, i)),
        ],
        out_specs=[
            pl.BlockSpec((ROW_BLK, D_MODEL), lambda i: (jnp.minimum(i, N_PROMPT_BLK - 1), 0)),
            pl.BlockSpec((ROW_BLK, D_MODEL), lambda i: (0, 0)),
        ],
        out_shape=[
            jax.ShapeDtypeStruct((N_PROMPT, D_MODEL), F32),
            jax.ShapeDtypeStruct((DEC_BATCH, D_MODEL), F32),
        ],
        compiler_params=_cparams("arbitrary"),
        name="final",
    )(x1, peer_t)


def _bucket(dist):
    max_exact = N_BUCKETS // 2
    dd = jnp.maximum(dist, max_exact).astype(F32)
    large = max_exact + (jnp.log(dd / max_exact) / math.log(RP_MAX_DIST / max_exact)
                         * (N_BUCKETS - max_exact)).astype(jnp.int32)
    large = jnp.minimum(large, N_BUCKETS - 1)
    return jnp.where(dist < max_exact, dist, large)


def _group_bias(rel_bias, g, d):
    dist = d * jnp.arange(N_KEYS, dtype=jnp.int32)
    bias = rel_bias[:, g * HEADS_PER_GROUP:(g + 1) * HEADS_PER_GROUP][_bucket(dist)]
    pad = jnp.full((HEADS_PER_GROUP, 2 * ROW_BLK - N_KEYS), NEG, F32)
    ext = jnp.concatenate([bias[::-1].T, pad], axis=1)
    return bias, ext[:, None, :]


def kernel(x_prompt, x_sample, cache_k_w128, cache_v_w128, cache_k_w512, cache_v_w512, cache_k_w2048,
           cache_v_w2048, state_lru_h, state_conv, rel_bias, ln1_g, w_in, q_norm_g, k_norm_g, conv_w,
           conv_b, w_rg, b_rg, w_ig, b_ig, lru_L, w_pa, w_pb, w_o, ln2_g, w_query, sub_keys_1,
           sub_keys_2, peer_u, peer_v):
    caches = ((cache_k_w128, cache_v_w128), (cache_k_w512, cache_v_w512), (cache_k_w2048, cache_v_w2048))
    xp = x_prompt.reshape(N_PROMPT, D_MODEL)
    xs = x_sample.reshape(DEC_BATCH, D_MODEL)
    w = w_in[0]

    xn, x_all = _norm1(xp, xs, ln1_g)
    gspec = [pl.BlockSpec((1, HEAD_DIM), lambda j, i: (0, 0))]
    wide = dict(tm=IN_TM, tn=IN_TN)
    (q32,) = _proj(xn, w, Q_OFF, ATTN_WIDTH, _epi_q, (F32,), (q_norm_g,), gspec, **wide)
    k32, k_heads = _proj(xn, w, K_OFF, ATTN_WIDTH, _epi_k, (F32,), (k_norm_g,), gspec, head_major=True, **wide)
    v32, v_heads = _proj(xn, w, V_OFF, ATTN_WIDTH, _epi_v, (F32,), head_major=True, **wide)
    (xg,) = _proj(xn, w, XB_OFF, 2 * LRU_WIDTH, _epi_f32, (F32,), **wide)
    (gates,) = _proj(xn, w, GA_OFF, 2 * D_MODEL, _epi_sigmoid, (BF16,), **wide)

    biases, exts = zip(*[_group_bias(rel_bias, g, d) for g, (_, d) in enumerate(ATTN_GROUPS)])
    attn_p = _attn_prompt(q32, k32, v32, jnp.concatenate(exts, axis=0))
    attn_s = _attn_sample(q32, k32, v32, [(ck[0], cv[0]) for ck, cv in caches], biases)
    attn = jnp.concatenate([attn_p, attn_s], axis=0)

    rec_p, h_p = _rec_prompt(xg, conv_w[0], conv_b, w_rg[0], b_rg, w_ig[0], b_ig, lru_L)
    rec_s, h_s = _rec_sample(xg, state_conv[0], state_lru_h[0], conv_w[0], conv_b, w_rg[0], b_rg,
                             w_ig[0], b_ig, lru_L)
    rec = jnp.concatenate([rec_p, rec_s], axis=0)

    merged = _mix(attn, rec, gates, w_pa[0], w_pb[0])
    (x1,) = _proj(merged, w_o[0], 0, D_MODEL, _epi_residual, (F32,), (x_all,),
                  [pl.BlockSpec((MM_TM, MM_TN), lambda j, i: (i, j))])

    xn2, xn2t = _norm2(x1, ln2_g)
    e1, nsel, rank2, e2 = _peer_prep(xn2, w_query[0], sub_keys_1[0], sub_keys_2[0])
    peer_t = _peer(xn2t, peer_u[0].astype(BF16), peer_v[0].T.astype(BF16), e1, nsel, rank2, e2)
    y_p, y_s = _final(x1, peer_t)

    outs = [y_p.reshape(BATCH, SEQ, D_MODEL), y_s.reshape(DEC_BATCH, 1, D_MODEL)]
    for g, (window, d) in enumerate(ATTN_GROUPS):
        keep = min(window, SEQ)
        for a in (k_heads, v_heads):
            tail = a[g, :N_PROMPT].reshape(BATCH, SEQ, HEADS_PER_GROUP, HEAD_DIM)[:, SEQ - keep:]
            outs.append(tail[None])
        for a in (k_heads, v_heads):
            outs.append(a[g, N_PROMPT:].reshape(1, DEC_BATCH, 1, HEADS_PER_GROUP, HEAD_DIM))
    xb_p = xg[:N_PROMPT, :LRU_WIDTH].reshape(BATCH, SEQ, LRU_WIDTH)
    xb_s = xg[N_PROMPT:, :LRU_WIDTH]
    outs += [
        h_p[None], h_s[None],
        xb_p[:, SEQ - (CONV_WIDTH - 1):][None],
        jnp.concatenate([state_conv[0][:, 1:], xb_s[:, None, :]], axis=1)[None],
    ]
    return tuple(outs)
```

```python
import functools
import math

import jax
import jax.numpy as jnp
import numpy as np
from jax import lax
from jax.experimental import pallas as pl
from jax.experimental.pallas import tpu as pltpu

F32 = jnp.float32
BF16 = jnp.bfloat16

D_MODEL = 4096
BATCH = 4
SEQ = 2048
DEC_BATCH = 128
N_PROMPT = BATCH * SEQ
N_TOK = N_PROMPT + DEC_BATCH
HEAD_DIM = 128
HEADS_PER_GROUP = 8
GROUP_COLS = HEADS_PER_GROUP * HEAD_DIM
ATTN_GROUPS = ((128, 1), (512, 4), (2048, 16))
N_GROUPS = 3
ATTN_WIDTH = N_GROUPS * GROUP_COLS
N_KEYS = 129
N_BUCKETS = 32
RP_MAX_DIST = 2048
LRU_WIDTH = 2048
LRU_BLOCKS = 16
LRU_BLOCK = 128
CONV_WIDTH = 4
LRU_C = 8.0
PEER_HEADS = 8
PEER_KEYS = 128
PEER_EXPERTS = PEER_KEYS * PEER_KEYS
PEER_DKEY = 256
PEER_TOPK = 16
EPS = 1e-6

Q_OFF = 0
K_OFF = ATTN_WIDTH
V_OFF = 2 * ATTN_WIDTH
XB_OFF = 3 * ATTN_WIDTH
GA_OFF = XB_OFF + 2 * LRU_WIDTH

ROW_BLK = 128
N_PROMPT_BLK = N_PROMPT // ROW_BLK
N_ROW_BLK = N_TOK // ROW_BLK
MM_TM = 1040
MM_TN = 512
IN_TM = 416
IN_TN = 1024
PEER_TM = 512
PEER_TE = 512
NEG = -1e30
VMEM_LIMIT = 58 * 1024 * 1024


def _cparams(*sem):
    return pltpu.CompilerParams(dimension_semantics=sem, vmem_limit_bytes=VMEM_LIMIT)


def _gelu(x):
    return jax.nn.gelu(x)


def _sigmoid(x):
    return 1.0 / (1.0 + jnp.exp(-x))


def _norm1_kernel(xp_ref, xs_ref, g_ref, xn_ref, xall_ref):
    i = pl.program_id(0)

    def body(x):
        ms = jnp.mean(x * x, axis=-1, keepdims=True)
        xn_ref[...] = (x * lax.rsqrt(ms + EPS) * g_ref[...]).astype(BF16)
        xall_ref[...] = x

    @pl.when(i < N_PROMPT_BLK)
    def _():
        body(xp_ref[...])

    @pl.when(i == N_PROMPT_BLK)
    def _():
        body(xs_ref[...])


def _norm1(xp, xs, g):
    return pl.pallas_call(
        _norm1_kernel,
        grid=(N_ROW_BLK,),
        in_specs=[
            pl.BlockSpec((ROW_BLK, D_MODEL), lambda i: (jnp.minimum(i, N_PROMPT_BLK - 1), 0)),
            pl.BlockSpec((ROW_BLK, D_MODEL), lambda i: (0, 0)),
            pl.BlockSpec((1, D_MODEL), lambda i: (0, 0)),
        ],
        out_specs=[
            pl.BlockSpec((ROW_BLK, D_MODEL), lambda i: (i, 0)),
            pl.BlockSpec((ROW_BLK, D_MODEL), lambda i: (i, 0)),
        ],
        out_shape=[
            jax.ShapeDtypeStruct((N_TOK, D_MODEL), BF16),
            jax.ShapeDtypeStruct((N_TOK, D_MODEL), F32),
        ],
        compiler_params=_cparams("arbitrary"),
        name="norm1",
    )(xp, xs, g)


def _proj_kernel(*refs, n_extra, epilogue):
    x_ref, w_ref = refs[0], refs[1]
    extra = refs[2:2 + n_extra]
    outs = refs[2 + n_extra:-1]
    wbf_ref = refs[-1]

    @pl.when(pl.program_id(1) == 0)
    def _():
        wbf_ref[...] = w_ref[...].astype(BF16)

    acc = jnp.dot(x_ref[...], wbf_ref[...], preferred_element_type=F32)
    epilogue(acc, extra, outs)


def _proj(x, w, col0, ncols, epilogue, out_dtypes, extra=(), extra_specs=(), tm=MM_TM, tn=MM_TN,
          head_major=False):
    m, k = x.shape
    grid = (ncols // tn, m // tm)
    cb0 = col0 // tn
    kern = functools.partial(_proj_kernel, n_extra=len(extra), epilogue=epilogue)
    out_specs = [pl.BlockSpec((tm, tn), lambda j, i: (i, j)) for _ in out_dtypes]
    out_shape = [jax.ShapeDtypeStruct((m, ncols), dt) for dt in out_dtypes]
    if head_major:
        assert tn == GROUP_COLS
        out_specs.append(pl.BlockSpec((None, tm, HEADS_PER_GROUP, HEAD_DIM), lambda j, i: (j, i, 0, 0)))
        out_shape.append(jax.ShapeDtypeStruct((ncols // tn, m, HEADS_PER_GROUP, HEAD_DIM), F32))
    return pl.pallas_call(
        kern,
        grid=grid,
        in_specs=[
            pl.BlockSpec((tm, k), lambda j, i: (i, 0)),
            pl.BlockSpec((k, tn), lambda j, i: (0, cb0 + j)),
            *extra_specs,
        ],
        out_specs=out_specs,
        out_shape=out_shape,
        scratch_shapes=[pltpu.VMEM((k, tn), BF16)],
        compiler_params=_cparams("arbitrary", "arbitrary"),
        name="proj",
    )(x, w, *extra)


def _head_norm(a, g):
    ms = jnp.mean(a * a, axis=-1, keepdims=True)
    return a * lax.rsqrt(ms + EPS) * g


def _epi_q(acc, extra, outs):
    g = extra[0][...]
    for h in range(acc.shape[1] // HEAD_DIM):
        sl = slice(h * HEAD_DIM, (h + 1) * HEAD_DIM)
        outs[0][:, sl] = _head_norm(acc[:, sl], g) * (HEAD_DIM ** -0.5)


def _epi_k(acc, extra, outs):
    g = extra[0][...]
    for h in range(acc.shape[1] // HEAD_DIM):
        sl = slice(h * HEAD_DIM, (h + 1) * HEAD_DIM)
        y = _head_norm(acc[:, sl], g)
        outs[0][:, sl] = y
        outs[1][:, h, :] = y


def _epi_v(acc, extra, outs):
    outs[0][...] = acc
    for h in range(acc.shape[1] // HEAD_DIM):
        outs[1][:, h, :] = acc[:, h * HEAD_DIM:(h + 1) * HEAD_DIM]


def _epi_f32(acc, extra, outs):
    outs[0][...] = acc


GATE_TM = 832
GATE_TN = 512
TABLE_ROWS = 128


def _gates_tables_kernel(x_ref, w_ref, pu_ref, pv_ref, g_ref, ub_ref, vt_ref, wbf_ref):
    @pl.when(pl.program_id(1) == 0)
    def _():
        wbf_ref[...] = w_ref[...].astype(BF16)

    acc = jnp.dot(x_ref[...], wbf_ref[...], preferred_element_type=F32)
    g_ref[...] = _sigmoid(acc).astype(BF16)
    ub_ref[...] = pu_ref[...].astype(BF16)
    vt_ref[...] = pv_ref[...].T.astype(BF16)


def _gates_and_tables(xn, w, peer_u, peer_v):
    ncols = 2 * D_MODEL
    nm = N_TOK // GATE_TM
    grid = (ncols // GATE_TN, nm)
    n_slabs = PEER_EXPERTS // TABLE_ROWS
    assert grid[0] * grid[1] >= n_slabs
    cb0 = GA_OFF // GATE_TN
    slab = lambda j, i: jnp.minimum(j * nm + i, n_slabs - 1)
    return pl.pallas_call(
        _gates_tables_kernel,
        grid=grid,
        in_specs=[
            pl.BlockSpec((GATE_TM, D_MODEL), lambda j, i: (i, 0)),
            pl.BlockSpec((D_MODEL, GATE_TN), lambda j, i: (0, cb0 + j)),
            pl.BlockSpec((TABLE_ROWS, D_MODEL), lambda j, i: (slab(j, i), 0)),
            pl.BlockSpec((TABLE_ROWS, D_MODEL), lambda j, i: (slab(j, i), 0)),
        ],
        out_specs=[
            pl.BlockSpec((GATE_TM, GATE_TN), lambda j, i: (i, j)),
            pl.BlockSpec((TABLE_ROWS, D_MODEL), lambda j, i: (slab(j, i), 0)),
            pl.BlockSpec((D_MODEL, TABLE_ROWS), lambda j, i: (0, slab(j, i))),
        ],
        out_shape=[
            jax.ShapeDtypeStruct((N_TOK, ncols), BF16),
            jax.ShapeDtypeStruct((PEER_EXPERTS, D_MODEL), BF16),
            jax.ShapeDtypeStruct((D_MODEL, PEER_EXPERTS), BF16),
        ],
        scratch_shapes=[pltpu.VMEM((D_MODEL, GATE_TN), BF16)],
        compiler_params=_cparams("arbitrary", "arbitrary"),
        name="gates_tables",
    )(xn, w, peer_u, peer_v)


def _attn_p_kernel(*refs):
    q_refs, k_refs, v_refs, ext_refs = refs[0:3], refs[3:6], refs[6:9], refs[9:12]
    out_ref = refs[12]
    o_scr, l_scr = refs[13:16], refs[16:19]
    nt = (((1,), (1,)), ((), ()))

    for g, (_, d) in enumerate(ATTN_GROUPS):
        q_ref, k_ref, v_ref = q_refs[g], k_refs[g], v_refs[g]
        band = pltpu.roll(jnp.broadcast_to(ext_refs[g][0], (ROW_BLK, 2 * ROW_BLK)), 0, 1,
                          stride=1, stride_axis=0)
        band_cur = band[:, ROW_BLK:]
        nblk = SEQ // d // ROW_BLK

        def rows(r, i, d=d):
            start = r + (d * ROW_BLK) * i
            return pl.ds(start, ROW_BLK) if d == 1 else pl.ds(start, ROW_BLK, stride=d)

        def block(r, i, first, g=g, q_ref=q_ref, k_ref=k_ref, v_ref=v_ref, band=band,
                  band_cur=band_cur, rows=rows):
            cur = rows(r, i)
            q = q_ref[cur, :].astype(BF16)
            if first:
                kk = k_ref[cur, :].astype(BF16)
                vv = v_ref[cur, :].astype(BF16)
                bias = band_cur
            else:
                prev = rows(r, i - 1)
                kk = jnp.concatenate([k_ref[prev, :], k_ref[cur, :]], axis=0).astype(BF16)
                vv = jnp.concatenate([v_ref[prev, :], v_ref[cur, :]], axis=0).astype(BF16)
                bias = band
            lg = lax.dot_general(q, kk, nt, preferred_element_type=F32) + bias
            m = jnp.max(lg, axis=-1, keepdims=True)
            p = jnp.exp(lg - m)
            s = jnp.sum(p, axis=-1, keepdims=True)
            o_scr[g][cur, :] = jnp.dot(p.astype(BF16), vv, preferred_element_type=F32) / s
            l_scr[g][cur, :] = jnp.broadcast_to(m + jnp.log(s), (ROW_BLK, HEAD_DIM))

        if d == 1:
            block(0, 0, True)
            per_trip = 5

            def trip(t, carry, block=block):
                for u in range(per_trip):
                    block(0, 1 + per_trip * t + u, False)
                return carry

            lax.fori_loop(0, (nblk - 1) // per_trip, trip, 0)
        elif nblk > 1:
            def trip(t, carry, block=block, nblk=nblk):
                for r in (2 * t, 2 * t + 1):
                    block(r, 0, True)
                    for i in range(1, nblk):
                        block(r, i, False)
                return carry

            lax.fori_loop(0, d // 2, trip, 0)
        else:
            per_trip = 8

            def trip(t, carry, block=block):
                for u in range(per_trip):
                    block(per_trip * t + u, 0, True)
                return carry

            lax.fori_loop(0, d // per_trip, trip, 0)

    l0, l1, l2 = l_scr[0][...], l_scr[1][...], l_scr[2][...]
    m = jnp.maximum(jnp.maximum(l0, l1), l2)
    e0, e1, e2 = jnp.exp(l0 - m), jnp.exp(l1 - m), jnp.exp(l2 - m)
    acc = e0 * o_scr[0][...] + e1 * o_scr[1][...] + e2 * o_scr[2][...]
    out_ref[...] = (acc / (e0 + e1 + e2)).astype(BF16)


def _attn_prompt(q32, k32, v32, ext):
    def col(g):
        return pl.BlockSpec((SEQ, HEAD_DIM), lambda b, h: (b, g * HEADS_PER_GROUP + h))

    def erow(g):
        return pl.BlockSpec((1, 1, 2 * ROW_BLK), lambda b, h: (g * HEADS_PER_GROUP + h, 0, 0))

    groups = range(N_GROUPS)
    return pl.pallas_call(
        _attn_p_kernel,
        grid=(BATCH, HEADS_PER_GROUP),
        in_specs=[col(g) for g in groups] * 3 + [erow(g) for g in groups],
        out_specs=pl.BlockSpec((SEQ, HEAD_DIM), lambda b, h: (b, h)),
        out_shape=jax.ShapeDtypeStruct((N_PROMPT, GROUP_COLS), BF16),
        scratch_shapes=[pltpu.VMEM((SEQ, HEAD_DIM), F32) for _ in range(2 * N_GROUPS)],
        compiler_params=_cparams("arbitrary", "arbitrary"),
        name="attn_p",
    )(q32, q32, q32, k32, k32, k32, v32, v32, v32, ext, ext, ext)


SAMPLE_BB = 4


def _attn_s_kernel(*refs):
    q_ref, kn_ref, vn_ref = refs[0:3]
    ck_refs, cv_refs = refs[3:6], refs[6:9]
    bc_ref, b0_ref, o_ref = refs[9], refs[10], refs[11]

    def one(bi, carry):
        outs, lses = [], []
        for g in range(N_GROUPS):
            hs = slice(g * HEADS_PER_GROUP, (g + 1) * HEADS_PER_GROUP)
            q = q_ref[bi, hs, :]
            lc = jnp.sum(ck_refs[g][bi] * q[None], axis=-1, keepdims=True) + bc_ref[g]
            ln = jnp.sum(kn_ref[bi, hs, :] * q, axis=-1, keepdims=True) + b0_ref[g]
            m = jnp.maximum(jnp.max(lc, axis=0), ln)
            p = jnp.exp(lc - m[None])
            pn = jnp.exp(ln - m)
            s = jnp.sum(p, axis=0) + pn
            outs.append((jnp.sum(p * cv_refs[g][bi], axis=0) + pn * vn_ref[bi, hs, :]) / s)
            lses.append(m + jnp.log(s))
        m = jnp.maximum(jnp.maximum(lses[0], lses[1]), lses[2])
        e = [jnp.exp(l - m) for l in lses]
        acc = e[0] * outs[0] + e[1] * outs[1] + e[2] * outs[2]
        o_ref[bi] = acc / (e[0] + e[1] + e[2])
        return carry

    lax.fori_loop(0, SAMPLE_BB, one, 0)


def _attn_sample(q32, k32, v32, caches, biases):
    heads3 = lambda a: a[N_PROMPT:].reshape(DEC_BATCH, N_GROUPS * HEADS_PER_GROUP, HEAD_DIM)
    new = pl.BlockSpec((SAMPLE_BB, N_GROUPS * HEADS_PER_GROUP, HEAD_DIM), lambda s: (s, 0, 0))
    bc = jnp.stack([b[::-1][:N_KEYS - 1] for b in biases])[..., None]
    b0 = jnp.stack([b[0] for b in biases])[..., None]
    cache_specs, cache_args = [], []
    for which in range(2):
        for g, (window, d) in enumerate(ATTN_GROUPS):
            c = caches[g][which]
            cache_args.append(c.reshape(DEC_BATCH, c.shape[1] // d, d, HEADS_PER_GROUP, HEAD_DIM))
            cache_specs.append(pl.BlockSpec((SAMPLE_BB, N_KEYS - 1, None, HEADS_PER_GROUP, HEAD_DIM),
                                            lambda s: (s, 0, 0, 0, 0)))
    o = pl.pallas_call(
        _attn_s_kernel,
        grid=(DEC_BATCH // SAMPLE_BB,),
        in_specs=[
            new, new, new, *cache_specs,
            pl.BlockSpec((N_GROUPS, N_KEYS - 1, HEADS_PER_GROUP, 1), lambda s: (0, 0, 0, 0)),
            pl.BlockSpec((N_GROUPS, HEADS_PER_GROUP, 1), lambda s: (0, 0, 0)),
        ],
        out_specs=pl.BlockSpec((SAMPLE_BB, HEADS_PER_GROUP, HEAD_DIM), lambda s: (s, 0, 0)),
        out_shape=jax.ShapeDtypeStruct((DEC_BATCH, HEADS_PER_GROUP, HEAD_DIM), F32),
        compiler_params=_cparams("arbitrary"),
        name="attn_s",
    )(heads3(q32), heads3(k32), heads3(v32), *cache_args, bc, b0)
    return o.reshape(DEC_BATCH, GROUP_COLS).astype(BF16)


SCAN_CHUNK = 64


def _softplus(x):
    return jnp.maximum(x, 0.0) + jnp.log1p(jnp.exp(-jnp.abs(x)))


def _gates(conv, wr_ref, br_ref, wi_ref, bi_ref, l_ref):
    cb = conv.astype(BF16)
    r = _sigmoid(jnp.dot(cb, wr_ref[0].astype(BF16), preferred_element_type=F32) + br_ref[...])
    ig = _sigmoid(jnp.dot(cb, wi_ref[0].astype(BF16), preferred_element_type=F32) + bi_ref[...])
    log_a = -LRU_C * r * _softplus(-l_ref[...])
    return log_a, ig


def _decay(log_a):
    a = jnp.exp(log_a)
    return a, jnp.sqrt(-jnp.tanh(log_a) * (a * a + 1.0))


def _rec_p_kernel(xb_ref, gb_ref, cw_ref, cb_ref, wr_ref, br_ref, wi_ref, bi_ref, l_ref,
                  y_ref, h_ref, a_scr, b_scr):
    t = SEQ
    x = xb_ref[...]
    row = lax.broadcasted_iota(jnp.int32, (t, LRU_BLOCK), 0)
    conv = cb_ref[...] + x * cw_ref[CONV_WIDTH - 1:CONV_WIDTH, :]
    for s in range(1, CONV_WIDTH):
        xs = jnp.where(row >= s, pltpu.roll(x, s, 0), 0.0)
        conv = conv + xs * cw_ref[CONV_WIDTH - 1 - s:CONV_WIDTH - s, :]
    log_a, ig = _gates(conv, wr_ref, br_ref, wi_ref, bi_ref, l_ref)
    a, mult = _decay(log_a)
    start = row == 0
    a_scr[...] = jnp.where(start, 0.0, a)
    b_scr[...] = jnp.where(start, 1.0, mult) * ig * conv

    crow = lax.broadcasted_iota(jnp.int32, (SCAN_CHUNK, LRU_BLOCK), 0)

    def chunk(c, h_prev):
        rows = pl.ds(pl.multiple_of(c * SCAN_CHUNK, SCAN_CHUNK), SCAN_CHUNK)
        a_c = a_scr[rows, :]
        b_c = b_scr[rows, :]
        s = 1
        while s < SCAN_CHUNK:
            keep = crow >= s
            a_sh = jnp.where(keep, pltpu.roll(a_c, s, 0), 1.0)
            b_sh = jnp.where(keep, pltpu.roll(b_c, s, 0), 0.0)
            b_c = a_c * b_sh + b_c
            a_c = a_c * a_sh
            s *= 2
        h = a_c * h_prev + b_c
        y_ref[rows, :] = (h * _gelu(gb_ref[rows, :])).astype(BF16)
        return h[SCAN_CHUNK - 1:SCAN_CHUNK, :]

    h_last = lax.fori_loop(0, t // SCAN_CHUNK, chunk, jnp.zeros((1, LRU_BLOCK), F32))
    h_ref[0] = h_last


def _rec_prompt(xg, conv_w, conv_b, w_rg, b_rg, w_ig, b_ig, lru_l):
    nb = LRU_BLOCKS
    vec = pl.BlockSpec((1, LRU_BLOCK), lambda b, n: (0, n))
    wblk = pl.BlockSpec((1, LRU_BLOCK, LRU_BLOCK), lambda b, n: (n, 0, 0))
    y, h = pl.pallas_call(
        _rec_p_kernel,
        grid=(BATCH, nb),
        in_specs=[
            pl.BlockSpec((SEQ, LRU_BLOCK), lambda b, n: (b, n)),
            pl.BlockSpec((SEQ, LRU_BLOCK), lambda b, n: (b, nb + n)),
            pl.BlockSpec((CONV_WIDTH, LRU_BLOCK), lambda b, n: (0, n)),
            vec, wblk, vec, wblk, vec, vec,
        ],
        out_specs=[
            pl.BlockSpec((SEQ, LRU_BLOCK), lambda b, n: (b, n)),
            pl.BlockSpec((1, 1, LRU_BLOCK), lambda b, n: (b, 0, n)),
        ],
        out_shape=[
            jax.ShapeDtypeStruct((N_PROMPT, LRU_WIDTH), BF16),
            jax.ShapeDtypeStruct((BATCH, 1, LRU_WIDTH), F32),
        ],
        scratch_shapes=[pltpu.VMEM((SEQ, LRU_BLOCK), F32), pltpu.VMEM((SEQ, LRU_BLOCK), F32)],
        compiler_params=_cparams("arbitrary", "arbitrary"),
        name="rec_p",
    )(xg, xg, conv_w, conv_b, w_rg, b_rg, w_ig, b_ig, lru_l)
    return y, h.reshape(BATCH, LRU_WIDTH)


def _rec_s_kernel(xb_ref, gb_ref, c0_ref, c1_ref, c2_ref, h0_ref, cw_ref, cb_ref,
                  wr_ref, br_ref, wi_ref, bi_ref, l_ref, y_ref, h_ref):
    conv = (cb_ref[...] + c0_ref[...] * cw_ref[0:1, :] + c1_ref[...] * cw_ref[1:2, :]
            + c2_ref[...] * cw_ref[2:3, :] + xb_ref[...] * cw_ref[3:4, :])
    log_a, ig = _gates(conv, wr_ref, br_ref, wi_ref, bi_ref, l_ref)
    a, mult = _decay(log_a)
    h = a * h0_ref[...] + mult * ig * conv
    h_ref[...] = h
    y_ref[...] = (h * _gelu(gb_ref[...])).astype(BF16)


def _rec_sample(xg, state_conv, h0, conv_w, conv_b, w_rg, b_rg, w_ig, b_ig, lru_l):
    nb = LRU_BLOCKS
    sc = state_conv.reshape(DEC_BATCH, (CONV_WIDTH - 1) * LRU_WIDTH)
    srow = N_PROMPT // DEC_BATCH
    vec = pl.BlockSpec((1, LRU_BLOCK), lambda n: (0, n))
    wblk = pl.BlockSpec((1, LRU_BLOCK, LRU_BLOCK), lambda n: (n, 0, 0))
    tile = lambda f: pl.BlockSpec((DEC_BATCH, LRU_BLOCK), f)
    return pl.pallas_call(
        _rec_s_kernel,
        grid=(nb,),
        in_specs=[
            tile(lambda n: (srow, n)),
            tile(lambda n: (srow, nb + n)),
            tile(lambda n: (0, n)),
            tile(lambda n: (0, nb + n)),
            tile(lambda n: (0, 2 * nb + n)),
            tile(lambda n: (0, n)),
            pl.BlockSpec((CONV_WIDTH, LRU_BLOCK), lambda n: (0, n)),
            vec, wblk, vec, wblk, vec, vec,
        ],
        out_specs=[tile(lambda n: (0, n)), tile(lambda n: (0, n))],
        out_shape=[
            jax.ShapeDtypeStruct((DEC_BATCH, LRU_WIDTH), BF16),
            jax.ShapeDtypeStruct((DEC_BATCH, LRU_WIDTH), F32),
        ],
        compiler_params=_cparams("arbitrary"),
        name="rec_s",
    )(xg, xg, sc, sc, sc, h0, conv_w, conv_b, w_rg, b_rg, w_ig, b_ig, lru_l)


def _mix_kernel(at_ref, rc_ref, ga_ref, gr_ref, wa_ref, wb_ref, o_ref, wa_bf, wb_bf):
    @pl.when(pl.program_id(1) == 0)
    def _():
        wa_bf[...] = wa_ref[...].astype(BF16)
        wb_bf[...] = wb_ref[...].astype(BF16)

    pa = jnp.dot(at_ref[...], wa_bf[...], preferred_element_type=F32)
    pb = jnp.dot(rc_ref[...], wb_bf[...], preferred_element_type=F32)
    o_ref[...] = (ga_ref[...].astype(F32) * pa + gr_ref[...].astype(F32) * pb).astype(BF16)


def _mix(attn, rec, gates, w_pa, w_pb):
    nj = D_MODEL // MM_TN
    return pl.pallas_call(
        _mix_kernel,
        grid=(nj, N_TOK // MM_TM),
        in_specs=[
            pl.BlockSpec((MM_TM, GROUP_COLS), lambda j, i: (i, 0)),
            pl.BlockSpec((MM_TM, LRU_WIDTH), lambda j, i: (i, 0)),
            pl.BlockSpec((MM_TM, MM_TN), lambda j, i: (i, j)),
            pl.BlockSpec((MM_TM, MM_TN), lambda j, i: (i, nj + j)),
            pl.BlockSpec((GROUP_COLS, MM_TN), lambda j, i: (0, j)),
            pl.BlockSpec((LRU_WIDTH, MM_TN), lambda j, i: (0, j)),
        ],
        out_specs=pl.BlockSpec((MM_TM, MM_TN), lambda j, i: (i, j)),
        out_shape=jax.ShapeDtypeStruct((N_TOK, D_MODEL), BF16),
        scratch_shapes=[pltpu.VMEM((GROUP_COLS, MM_TN), BF16), pltpu.VMEM((LRU_WIDTH, MM_TN), BF16)],
        compiler_params=_cparams("arbitrary", "arbitrary"),
        name="mix",
    )(attn, rec, gates, gates, w_pa, w_pb)


def _epi_residual(acc, extra, outs):
    outs[0][...] = extra[0][...] + acc


def _norm2_kernel(x_ref, g_ref, xn_ref, xnt_ref):
    x = x_ref[...]
    ms = jnp.mean(x * x, axis=-1, keepdims=True)
    y = x * lax.rsqrt(ms + EPS) * g_ref[...]
    xn_ref[...] = y.astype(BF16)
    xnt_ref[...] = y.T.astype(BF16)


def _norm2(x1, g):
    return pl.pallas_call(
        _norm2_kernel,
        grid=(N_ROW_BLK,),
        in_specs=[
            pl.BlockSpec((ROW_BLK, D_MODEL), lambda i: (i, 0)),
            pl.BlockSpec((1, D_MODEL), lambda i: (0, 0)),
        ],
        out_specs=[
            pl.BlockSpec((ROW_BLK, D_MODEL), lambda i: (i, 0)),
            pl.BlockSpec((D_MODEL, ROW_BLK), lambda i: (0, i)),
        ],
        out_shape=[
            jax.ShapeDtypeStruct((N_TOK, D_MODEL), BF16),
            jax.ShapeDtypeStruct((D_MODEL, N_TOK), BF16),
        ],
        compiler_params=_cparams("arbitrary"),
        name="norm2",
    )(x1, g)


NO_RANK = float(PEER_KEYS - 1)


def _top_values(x, k, with_rank=False):
    rows = lax.broadcasted_iota(jnp.int32, x.shape, 0).astype(F32)
    rank = jnp.full(x.shape, NO_RANK, F32)
    out = []
    for j in range(k):
        m = jnp.max(x, axis=0, keepdims=True)
        first = jnp.min(jnp.where(x == m, rows, float(x.shape[0])), axis=0, keepdims=True)
        out.append(m)
        hit = rows == first
        x = jnp.where(hit, -jnp.inf, x)
        if with_rank:
            rank = jnp.where(hit, float(j), rank)
    return (out, rank) if with_rank else out


def _peer_prep_kernel(x_ref, wq_ref, k1_ref, k2_ref, e1_ref, n_ref, rank_ref, e2_ref, wq_bf):
    @pl.when(pl.program_id(1) == 0)
    def _():
        wq_bf[...] = wq_ref[...].astype(BF16)

    half = PEER_DKEY // 2
    q = jnp.dot(x_ref[...], wq_bf[...], preferred_element_type=F32).astype(BF16)
    nt = (((1,), (1,)), ((), ()))
    s1 = lax.dot_general(k1_ref[0].astype(BF16), q[:, :half], nt, preferred_element_type=F32)
    s2 = lax.dot_general(k2_ref[0].astype(BF16), q[:, half:], nt, preferred_element_type=F32)
    for c in range(PREP_TM // 128):
        cl = slice(c * 128, (c + 1) * 128)
        x1, x2 = s1[:, cl], s2[:, cl]
        a = _top_values(x1, PEER_TOPK)
        b, rank2 = _top_values(x2, PEER_TOPK, with_rank=True)
        cand = [a[i] + b[j] for i in range(PEER_TOPK) for j in range(PEER_TOPK) if (i + 1) * (j + 1) <= PEER_TOPK]
        cand += [jnp.full_like(a[0], -jnp.inf)] * (-len(cand) % 8)
        top = _top_values(jnp.concatenate(cand, axis=0), PEER_TOPK)
        tau = top[PEER_TOPK - 1]
        z = jnp.ones_like(top[0])
        for v in top[1:]:
            z = z + jnp.exp(v - top[0])
        n = jnp.zeros_like(x1)
        for k, bk in enumerate(b):
            n = jnp.where(x1 + bk >= tau, float(k + 1), n)
        e1_ref[0, :, cl] = jnp.exp(x1 - a[0]) / z
        n_ref[0, :, cl] = n
        rank_ref[0, :, cl] = rank2.astype(BF16)
        e2_ref[0, :, cl] = jnp.exp(x2 - b[0]).astype(BF16)


PREP_TM = 640


def _peer_prep(xn2, w_query, k1, k2):
    tm = PREP_TM
    half = PEER_DKEY // 2
    big = pl.BlockSpec((1, PEER_KEYS, tm), lambda h, i: (h, 0, i))
    keys = pl.BlockSpec((1, PEER_KEYS, half), lambda h, i: (h, 0, 0))
    shp = lambda dt: jax.ShapeDtypeStruct((PEER_HEADS, PEER_KEYS, N_TOK), dt)
    return pl.pallas_call(
        _peer_prep_kernel,
        grid=(PEER_HEADS, N_TOK // tm),
        in_specs=[
            pl.BlockSpec((tm, D_MODEL), lambda h, i: (i, 0)),
            pl.BlockSpec((D_MODEL, PEER_DKEY), lambda h, i: (0, h)),
            keys, keys,
        ],
        out_specs=[big, big, big, big],
        out_shape=[shp(F32), shp(F32), shp(BF16), shp(BF16)],
        scratch_shapes=[pltpu.VMEM((D_MODEL, PEER_DKEY), BF16)],
        compiler_params=_cparams("arbitrary", "arbitrary"),
        name="peer_prep",
    )(xn2, w_query, k1, k2)


I1_PER_TILE = PEER_TE // PEER_KEYS
I1_BLK = 8


def _peer_kernel(xt_ref, u_ref, vt_ref, e1_ref, n_ref, rank_ref, e2_ref, o_ref, w_scr):
    e = pl.program_id(1)

    @pl.when(e == 0)
    def _():
        o_ref[...] = jnp.zeros_like(o_ref)

    st = jnp.dot(u_ref[...], xt_ref[...], preferred_element_type=F32)
    off = (e % (I1_BLK // I1_PER_TILE)) * I1_PER_TILE
    zero = jnp.zeros((), BF16)
    for r in range(I1_PER_TILE):
        rows = slice(r * PEER_KEYS, (r + 1) * PEER_KEYS)
        coef = jnp.zeros((PEER_KEYS, PEER_TM), BF16)
        for h in range(PEER_HEADS):
            n = n_ref[h, pl.ds(off + r, 1), :].astype(BF16)
            g1 = e1_ref[h, pl.ds(off + r, 1), :].astype(BF16)
            coef = coef + g1 * jnp.where(rank_ref[h] < n, e2_ref[h], zero)
        w_scr[rows, :] = coef * _gelu(st[rows, :]).astype(BF16)
    o_ref[...] += jnp.dot(vt_ref[...], w_scr[...], preferred_element_type=F32)


def _peer(xn2t, u_bf, vt_bf, e1, n, rank2, e2):
    nt = pl.cdiv(N_TOK, PEER_TM)
    ne = PEER_EXPERTS // PEER_TE
    per = I1_BLK // I1_PER_TILE
    small = pl.BlockSpec((PEER_HEADS, I1_BLK, PEER_TM), lambda t, e: (0, e // per, t))
    big = pl.BlockSpec((PEER_HEADS, PEER_KEYS, PEER_TM), lambda t, e: (0, 0, t))
    return pl.pallas_call(
        _peer_kernel,
        grid=(nt, ne),
        in_specs=[
            pl.BlockSpec((D_MODEL, PEER_TM), lambda t, e: (0, t)),
            pl.BlockSpec((PEER_TE, D_MODEL), lambda t, e: (e, 0)),
            pl.BlockSpec((D_MODEL, PEER_TE), lambda t, e: (0, e)),
            small, small, big, big,
        ],
        out_specs=pl.BlockSpec((D_MODEL, PEER_TM), lambda t, e: (0, t)),
        out_shape=jax.ShapeDtypeStruct((D_MODEL, N_TOK), F32),
        scratch_shapes=[pltpu.VMEM((PEER_TE, PEER_TM), BF16)],
        compiler_params=_cparams("arbitrary", "arbitrary"),
        name="peer",
    )(xn2t, u_bf, vt_bf, e1, n, rank2, e2)


def _final_kernel(x_ref, pt_ref, yp_ref, ys_ref):
    i = pl.program_id(0)
    y = x_ref[...] + pt_ref[...].T

    @pl.when(i < N_PROMPT_BLK)
    def _():
        yp_ref[...] = y

    @pl.when(i == N_PROMPT_BLK)
    def _():
        ys_ref[...] = y


def _final(x1, peer_t):
    return pl.pallas_call(
        _final_kernel,
        grid=(N_ROW_BLK,),
        in_specs=[
            pl.BlockSpec((ROW_BLK, D_MODEL), lambda i: (i, 0)),
            pl.BlockSpec((D_MODEL, ROW_BLK), lambda i: (0, i)),
        ],
        out_specs=[
            pl.BlockSpec((ROW_BLK, D_MODEL), lambda i: (jnp.minimum(i, N_PROMPT_BLK - 1), 0)),
            pl.BlockSpec((ROW_BLK, D_MODEL), lambda i: (0, 0)),
        ],
        out_shape=[
            jax.ShapeDtypeStruct((N_PROMPT, D_MODEL), F32),
            jax.ShapeDtypeStruct((DEC_BATCH, D_MODEL), F32),
        ],
        compiler_params=_cparams("arbitrary"),
        name="final",
    )(x1, peer_t)


def _bucket(dist):
    max_exact = N_BUCKETS // 2
    dd = jnp.maximum(dist, max_exact).astype(F32)
    large = max_exact + (jnp.log(dd / max_exact) / math.log(RP_MAX_DIST / max_exact)
                         * (N_BUCKETS - max_exact)).astype(jnp.int32)
    large = jnp.minimum(large, N_BUCKETS - 1)
    return jnp.where(dist < max_exact, dist, large)


def _group_bias(rel_bias, g, d):
    dist = d * jnp.arange(N_KEYS, dtype=jnp.int32)
    bias = rel_bias[:, g * HEADS_PER_GROUP:(g + 1) * HEADS_PER_GROUP][_bucket(dist)]
    pad = jnp.full((HEADS_PER_GROUP, 2 * ROW_BLK - N_KEYS), NEG, F32)
    ext = jnp.concatenate([bias[::-1].T, pad], axis=1)
    return bias, ext[:, None, :]


def kernel(x_prompt, x_sample, cache_k_w128, cache_v_w128, cache_k_w512, cache_v_w512, cache_k_w2048,
           cache_v_w2048, state_lru_h, state_conv, rel_bias, ln1_g, w_in, q_norm_g, k_norm_g, conv_w,
           conv_b, w_rg, b_rg, w_ig, b_ig, lru_L, w_pa, w_pb, w_o, ln2_g, w_query, sub_keys_1,
           sub_keys_2, peer_u, peer_v):
    caches = ((cache_k_w128, cache_v_w128), (cache_k_w512, cache_v_w512), (cache_k_w2048, cache_v_w2048))
    xp = x_prompt.reshape(N_PROMPT, D_MODEL)
    xs = x_sample.reshape(DEC_BATCH, D_MODEL)
    w = w_in[0]

    xn, x_all = _norm1(xp, xs, ln1_g)
    gspec = [pl.BlockSpec((1, HEAD_DIM), lambda j, i: (0, 0))]
    wide = dict(tm=IN_TM, tn=IN_TN)
    (q32,) = _proj(xn, w, Q_OFF, ATTN_WIDTH, _epi_q, (F32,), (q_norm_g,), gspec, **wide)
    k32, k_heads = _proj(xn, w, K_OFF, ATTN_WIDTH, _epi_k, (F32,), (k_norm_g,), gspec, head_major=True, **wide)
    v32, v_heads = _proj(xn, w, V_OFF, ATTN_WIDTH, _epi_v, (F32,), head_major=True, **wide)
    (xg,) = _proj(xn, w, XB_OFF, 2 * LRU_WIDTH, _epi_f32, (F32,), **wide)
    gates, u_bf, vt_bf = _gates_and_tables(xn, w, peer_u[0], peer_v[0])

    biases, exts = zip(*[_group_bias(rel_bias, g, d) for g, (_, d) in enumerate(ATTN_GROUPS)])
    attn_p = _attn_prompt(q32, k32, v32, jnp.concatenate(exts, axis=0))
    attn_s = _attn_sample(q32, k32, v32, [(ck[0], cv[0]) for ck, cv in caches], biases)
    attn = jnp.concatenate([attn_p, attn_s], axis=0)

    rec_p, h_p = _rec_prompt(xg, conv_w[0], conv_b, w_rg[0], b_rg, w_ig[0], b_ig, lru_L)
    rec_s, h_s = _rec_sample(xg, state_conv[0], state_lru_h[0], conv_w[0], conv_b, w_rg[0], b_rg,
                             w_ig[0], b_ig, lru_L)
    rec = jnp.concatenate([rec_p, rec_s], axis=0)

    merged = _mix(attn, rec, gates, w_pa[0], w_pb[0])
    (x1,) = _proj(merged, w_o[0], 0, D_MODEL, _epi_residual, (F32,), (x_all,),
                  [pl.BlockSpec((MM_TM, MM_TN), lambda j, i: (i, j))])

    xn2, xn2t = _norm2(x1, ln2_g)
    e1, nsel, rank2, e2 = _peer_prep(xn2, w_query[0], sub_keys_1[0], sub_keys_2[0])
    peer_t = _peer(xn2t, u_bf, vt_bf, e1, nsel, rank2, e2)
    y_p, y_s = _final(x1, peer_t)

    outs = [y_p.reshape(BATCH, SEQ, D_MODEL), y_s.reshape(DEC_BATCH, 1, D_MODEL)]
    for g, (window, d) in enumerate(ATTN_GROUPS):
        keep = min(window, SEQ)
        for a in (k_heads, v_heads):
            tail = a[g, :N_PROMPT].reshape(BATCH, SEQ, HEADS_PER_GROUP, HEAD_DIM)[:, SEQ - keep:]
            outs.append(tail[None])
        for a in (k_heads, v_heads):
            outs.append(a[g, N_PROMPT:].reshape(1, DEC_BATCH, 1, HEADS_PER_GROUP, HEAD_DIM))
    xb_p = xg[:N_PROMPT, :LRU_WIDTH].reshape(BATCH, SEQ, LRU_WIDTH)
    xb_s = xg[N_PROMPT:, :LRU_WIDTH]
    outs += [
        h_p[None], h_s[None],
        xb_p[:, SEQ - (CONV_WIDTH - 1):][None],
        jnp.concatenate([state_conv[0][:, 1:], xb_s[:, None, :]], axis=1)[None],
    ]
    return tuple(outs)
```

```python
import functools
import math

import jax
import jax.numpy as jnp
import numpy as np
from jax import lax
from jax.experimental import pallas as pl
from jax.experimental.pallas import tpu as pltpu

F32 = jnp.float32
BF16 = jnp.bfloat16

D_MODEL = 4096
BATCH = 4
SEQ = 2048
DEC_BATCH = 128
N_PROMPT = BATCH * SEQ
N_TOK = N_PROMPT + DEC_BATCH
HEAD_DIM = 128
HEADS_PER_GROUP = 8
GROUP_COLS = HEADS_PER_GROUP * HEAD_DIM
ATTN_GROUPS = ((128, 1), (512, 4), (2048, 16))
N_GROUPS = 3
ATTN_WIDTH = N_GROUPS * GROUP_COLS
N_KEYS = 129
N_BUCKETS = 32
RP_MAX_DIST = 2048
LRU_WIDTH = 2048
LRU_BLOCKS = 16
LRU_BLOCK = 128
CONV_WIDTH = 4
LRU_C = 8.0
PEER_HEADS = 8
PEER_KEYS = 128
PEER_EXPERTS = PEER_KEYS * PEER_KEYS
PEER_DKEY = 256
PEER_TOPK = 16
EPS = 1e-6

Q_OFF = 0
K_OFF = ATTN_WIDTH
V_OFF = 2 * ATTN_WIDTH
XB_OFF = 3 * ATTN_WIDTH
GA_OFF = XB_OFF + 2 * LRU_WIDTH

ROW_BLK = 128
N_PROMPT_BLK = N_PROMPT // ROW_BLK
N_ROW_BLK = N_TOK // ROW_BLK
MM_TM = 1040
MM_TN = 512
IN_TM = 416
IN_TN = 1024
PEER_TM = 512
PEER_TE = 512
NEG = -1e30
VMEM_LIMIT = 58 * 1024 * 1024


def _cparams(*sem):
    return pltpu.CompilerParams(dimension_semantics=sem, vmem_limit_bytes=VMEM_LIMIT)


def _gelu(x):
    return jax.nn.gelu(x)


def _sigmoid(x):
    return 1.0 / (1.0 + jnp.exp(-x))


def _norm1_kernel(xp_ref, xs_ref, g_ref, xn_ref, xall_ref):
    i = pl.program_id(0)

    def body(x):
        ms = jnp.mean(x * x, axis=-1, keepdims=True)
        xn_ref[...] = (x * lax.rsqrt(ms + EPS) * g_ref[...]).astype(BF16)
        xall_ref[...] = x

    @pl.when(i < N_PROMPT_BLK)
    def _():
        body(xp_ref[...])

    @pl.when(i == N_PROMPT_BLK)
    def _():
        body(xs_ref[...])


def _norm1(xp, xs, g):
    return pl.pallas_call(
        _norm1_kernel,
        grid=(N_ROW_BLK,),
        in_specs=[
            pl.BlockSpec((ROW_BLK, D_MODEL), lambda i: (jnp.minimum(i, N_PROMPT_BLK - 1), 0)),
            pl.BlockSpec((ROW_BLK, D_MODEL), lambda i: (0, 0)),
            pl.BlockSpec((1, D_MODEL), lambda i: (0, 0)),
        ],
        out_specs=[
            pl.BlockSpec((ROW_BLK, D_MODEL), lambda i: (i, 0)),
            pl.BlockSpec((ROW_BLK, D_MODEL), lambda i: (i, 0)),
        ],
        out_shape=[
            jax.ShapeDtypeStruct((N_TOK, D_MODEL), BF16),
            jax.ShapeDtypeStruct((N_TOK, D_MODEL), F32),
        ],
        compiler_params=_cparams("arbitrary"),
        name="norm1",
    )(xp, xs, g)


def _proj_kernel(*refs, n_extra, epilogue):
    x_ref, w_ref = refs[0], refs[1]
    extra = refs[2:2 + n_extra]
    outs = refs[2 + n_extra:-1]
    wbf_ref = refs[-1]

    @pl.when(pl.program_id(1) == 0)
    def _():
        wbf_ref[...] = w_ref[...].astype(BF16)

    acc = jnp.dot(x_ref[...], wbf_ref[...], preferred_element_type=F32)
    epilogue(acc, extra, outs)


def _proj(x, w, col0, ncols, epilogue, out_dtypes, extra=(), extra_specs=(), tm=MM_TM, tn=MM_TN,
          head_major=False):
    m, k = x.shape
    grid = (ncols // tn, m // tm)
    cb0 = col0 // tn
    kern = functools.partial(_proj_kernel, n_extra=len(extra), epilogue=epilogue)
    out_specs = [pl.BlockSpec((tm, tn), lambda j, i: (i, j)) for _ in out_dtypes]
    out_shape = [jax.ShapeDtypeStruct((m, ncols), dt) for dt in out_dtypes]
    if head_major:
        assert tn == GROUP_COLS
        out_specs.append(pl.BlockSpec((None, tm, HEADS_PER_GROUP, HEAD_DIM), lambda j, i: (j, i, 0, 0)))
        out_shape.append(jax.ShapeDtypeStruct((ncols // tn, m, HEADS_PER_GROUP, HEAD_DIM), F32))
    return pl.pallas_call(
        kern,
        grid=grid,
        in_specs=[
            pl.BlockSpec((tm, k), lambda j, i: (i, 0)),
            pl.BlockSpec((k, tn), lambda j, i: (0, cb0 + j)),
            *extra_specs,
        ],
        out_specs=out_specs,
        out_shape=out_shape,
        scratch_shapes=[pltpu.VMEM((k, tn), BF16)],
        compiler_params=_cparams("arbitrary", "arbitrary"),
        name="proj",
    )(x, w, *extra)


def _head_norm(a, g):
    ms = jnp.mean(a * a, axis=-1, keepdims=True)
    return a * lax.rsqrt(ms + EPS) * g


def _epi_q(acc, extra, outs):
    g = extra[0][...]
    for h in range(acc.shape[1] // HEAD_DIM):
        sl = slice(h * HEAD_DIM, (h + 1) * HEAD_DIM)
        outs[0][:, sl] = _head_norm(acc[:, sl], g) * (HEAD_DIM ** -0.5)


def _epi_k(acc, extra, outs):
    g = extra[0][...]
    for h in range(acc.shape[1] // HEAD_DIM):
        sl = slice(h * HEAD_DIM, (h + 1) * HEAD_DIM)
        y = _head_norm(acc[:, sl], g)
        outs[0][:, sl] = y
        outs[1][:, h, :] = y


def _epi_v(acc, extra, outs):
    outs[0][...] = acc
    for h in range(acc.shape[1] // HEAD_DIM):
        outs[1][:, h, :] = acc[:, h * HEAD_DIM:(h + 1) * HEAD_DIM]


def _epi_f32(acc, extra, outs):
    outs[0][...] = acc


GATE_TM = 832
GATE_TN = 512
TABLE_ROWS = 128


def _gates_tables_kernel(x_ref, w_ref, pu_ref, pv_ref, g_ref, ub_ref, vt_ref, wbf_ref):
    @pl.when(pl.program_id(1) == 0)
    def _():
        wbf_ref[...] = w_ref[...].astype(BF16)

    acc = jnp.dot(x_ref[...], wbf_ref[...], preferred_element_type=F32)
    g_ref[...] = _sigmoid(acc).astype(BF16)
    ub_ref[...] = pu_ref[...].astype(BF16)
    vt_ref[...] = pv_ref[...].T.astype(BF16)


def _gates_and_tables(xn, w, peer_u, peer_v):
    ncols = 2 * D_MODEL
    nm = N_TOK // GATE_TM
    grid = (ncols // GATE_TN, nm)
    n_slabs = PEER_EXPERTS // TABLE_ROWS
    assert grid[0] * grid[1] >= n_slabs
    cb0 = GA_OFF // GATE_TN
    slab = lambda j, i: jnp.minimum(j * nm + i, n_slabs - 1)
    return pl.pallas_call(
        _gates_tables_kernel,
        grid=grid,
        in_specs=[
            pl.BlockSpec((GATE_TM, D_MODEL), lambda j, i: (i, 0)),
            pl.BlockSpec((D_MODEL, GATE_TN), lambda j, i: (0, cb0 + j)),
            pl.BlockSpec((TABLE_ROWS, D_MODEL), lambda j, i: (slab(j, i), 0)),
            pl.BlockSpec((TABLE_ROWS, D_MODEL), lambda j, i: (slab(j, i), 0)),
        ],
        out_specs=[
            pl.BlockSpec((GATE_TM, GATE_TN), lambda j, i: (i, j)),
            pl.BlockSpec((TABLE_ROWS, D_MODEL), lambda j, i: (slab(j, i), 0)),
            pl.BlockSpec((D_MODEL, TABLE_ROWS), lambda j, i: (0, slab(j, i))),
        ],
        out_shape=[
            jax.ShapeDtypeStruct((N_TOK, ncols), BF16),
            jax.ShapeDtypeStruct((PEER_EXPERTS, D_MODEL), BF16),
            jax.ShapeDtypeStruct((D_MODEL, PEER_EXPERTS), BF16),
        ],
        scratch_shapes=[pltpu.VMEM((D_MODEL, GATE_TN), BF16)],
        compiler_params=_cparams("arbitrary", "arbitrary"),
        name="gates_tables",
    )(xn, w, peer_u, peer_v)


def _attn_p_kernel(*refs):
    q_refs, k_refs, v_refs, ext_refs = refs[0:3], refs[3:6], refs[6:9], refs[9:12]
    out_ref = refs[12]
    o_scr, l_scr = refs[13:16], refs[16:19]
    nt = (((1,), (1,)), ((), ()))

    for g, (_, d) in enumerate(ATTN_GROUPS):
        q_ref, k_ref, v_ref = q_refs[g], k_refs[g], v_refs[g]
        band = pltpu.roll(jnp.broadcast_to(ext_refs[g][0], (ROW_BLK, 2 * ROW_BLK)), 0, 1,
                          stride=1, stride_axis=0)
        band_cur = band[:, ROW_BLK:]
        nblk = SEQ // d // ROW_BLK

        def rows(r, i, d=d):
            start = r + (d * ROW_BLK) * i
            return pl.ds(start, ROW_BLK) if d == 1 else pl.ds(start, ROW_BLK, stride=d)

        def block(r, i, first, g=g, q_ref=q_ref, k_ref=k_ref, v_ref=v_ref, band=band,
                  band_cur=band_cur, rows=rows):
            cur = rows(r, i)
            q = q_ref[cur, :].astype(BF16)
            if first:
                kk = k_ref[cur, :].astype(BF16)
                vv = v_ref[cur, :].astype(BF16)
                bias = band_cur
            else:
                prev = rows(r, i - 1)
                kk = jnp.concatenate([k_ref[prev, :], k_ref[cur, :]], axis=0).astype(BF16)
                vv = jnp.concatenate([v_ref[prev, :], v_ref[cur, :]], axis=0).astype(BF16)
                bias = band
            lg = lax.dot_general(q, kk, nt, preferred_element_type=F32) + bias
            m = jnp.max(lg, axis=-1, keepdims=True)
            p = jnp.exp(lg - m)
            s = jnp.sum(p, axis=-1, keepdims=True)
            o_scr[g][cur, :] = jnp.dot(p.astype(BF16), vv, preferred_element_type=F32) / s
            l_scr[g][cur, :] = jnp.broadcast_to(m + jnp.log(s), (ROW_BLK, HEAD_DIM))

        if d == 1:
            block(0, 0, True)
            per_trip = 5

            def trip(t, carry, block=block):
                for u in range(per_trip):
                    block(0, 1 + per_trip * t + u, False)
                return carry

            lax.fori_loop(0, (nblk - 1) // per_trip, trip, 0)
        elif nblk > 1:
            def trip(t, carry, block=block, nblk=nblk):
                for r in (2 * t, 2 * t + 1):
                    block(r, 0, True)
                    for i in range(1, nblk):
                        block(r, i, False)
                return carry

            lax.fori_loop(0, d // 2, trip, 0)
        else:
            per_trip = 8

            def trip(t, carry, block=block):
                for u in range(per_trip):
                    block(per_trip * t + u, 0, True)
                return carry

            lax.fori_loop(0, d // per_trip, trip, 0)

    l0, l1, l2 = l_scr[0][...], l_scr[1][...], l_scr[2][...]
    m = jnp.maximum(jnp.maximum(l0, l1), l2)
    e0, e1, e2 = jnp.exp(l0 - m), jnp.exp(l1 - m), jnp.exp(l2 - m)
    acc = e0 * o_scr[0][...] + e1 * o_scr[1][...] + e2 * o_scr[2][...]
    out_ref[...] = (acc / (e0 + e1 + e2)).astype(BF16)


def _attn_prompt(q32, k32, v32, ext):
    def col(g):
        return pl.BlockSpec((SEQ, HEAD_DIM), lambda b, h: (b, g * HEADS_PER_GROUP + h))

    def erow(g):
        return pl.BlockSpec((1, 1, 2 * ROW_BLK), lambda b, h: (g * HEADS_PER_GROUP + h, 0, 0))

    groups = range(N_GROUPS)
    return pl.pallas_call(
        _attn_p_kernel,
        grid=(BATCH, HEADS_PER_GROUP),
        in_specs=[col(g) for g in groups] * 3 + [erow(g) for g in groups],
        out_specs=pl.BlockSpec((SEQ, HEAD_DIM), lambda b, h: (b, h)),
        out_shape=jax.ShapeDtypeStruct((N_PROMPT, GROUP_COLS), BF16),
        scratch_shapes=[pltpu.VMEM((SEQ, HEAD_DIM), F32) for _ in range(2 * N_GROUPS)],
        compiler_params=_cparams("arbitrary", "arbitrary"),
        name="attn_p",
    )(q32, q32, q32, k32, k32, k32, v32, v32, v32, ext, ext, ext)


SAMPLE_BB = 4


def _attn_s_kernel(*refs):
    q_ref, kn_ref, vn_ref = refs[0:3]
    ck_refs, cv_refs = refs[3:6], refs[6:9]
    bc_ref, b0_ref, o_ref = refs[9], refs[10], refs[11]

    def one(bi, carry):
        outs, lses = [], []
        for g in range(N_GROUPS):
            hs = slice(g * HEADS_PER_GROUP, (g + 1) * HEADS_PER_GROUP)
            q = q_ref[bi, hs, :]
            lc = jnp.sum(ck_refs[g][bi] * q[None], axis=-1, keepdims=True) + bc_ref[g]
            ln = jnp.sum(kn_ref[bi, hs, :] * q, axis=-1, keepdims=True) + b0_ref[g]
            m = jnp.maximum(jnp.max(lc, axis=0), ln)
            p = jnp.exp(lc - m[None])
            pn = jnp.exp(ln - m)
            s = jnp.sum(p, axis=0) + pn
            outs.append((jnp.sum(p * cv_refs[g][bi], axis=0) + pn * vn_ref[bi, hs, :]) / s)
            lses.append(m + jnp.log(s))
        m = jnp.maximum(jnp.maximum(lses[0], lses[1]), lses[2])
        e = [jnp.exp(l - m) for l in lses]
        acc = e[0] * outs[0] + e[1] * outs[1] + e[2] * outs[2]
        o_ref[bi] = acc / (e[0] + e[1] + e[2])
        return carry

    lax.fori_loop(0, SAMPLE_BB, one, 0)


def _attn_sample(q32, k32, v32, caches, biases):
    heads3 = lambda a: a[N_PROMPT:].reshape(DEC_BATCH, N_GROUPS * HEADS_PER_GROUP, HEAD_DIM)
    new = pl.BlockSpec((SAMPLE_BB, N_GROUPS * HEADS_PER_GROUP, HEAD_DIM), lambda s: (s, 0, 0))
    bc = jnp.stack([b[::-1][:N_KEYS - 1] for b in biases])[..., None]
    b0 = jnp.stack([b[0] for b in biases])[..., None]
    cache_specs, cache_args = [], []
    for which in range(2):
        for g, (window, d) in enumerate(ATTN_GROUPS):
            c = caches[g][which]
            cache_args.append(c.reshape(DEC_BATCH, c.shape[1] // d, d, HEADS_PER_GROUP, HEAD_DIM))
            cache_specs.append(pl.BlockSpec((SAMPLE_BB, N_KEYS - 1, None, HEADS_PER_GROUP, HEAD_DIM),
                                            lambda s: (s, 0, 0, 0, 0)))
    o = pl.pallas_call(
        _attn_s_kernel,
        grid=(DEC_BATCH // SAMPLE_BB,),
        in_specs=[
            new, new, new, *cache_specs,
            pl.BlockSpec((N_GROUPS, N_KEYS - 1, HEADS_PER_GROUP, 1), lambda s: (0, 0, 0, 0)),
            pl.BlockSpec((N_GROUPS, HEADS_PER_GROUP, 1), lambda s: (0, 0, 0)),
        ],
        out_specs=pl.BlockSpec((SAMPLE_BB, HEADS_PER_GROUP, HEAD_DIM), lambda s: (s, 0, 0)),
        out_shape=jax.ShapeDtypeStruct((DEC_BATCH, HEADS_PER_GROUP, HEAD_DIM), F32),
        compiler_params=_cparams("arbitrary"),
        name="attn_s",
    )(heads3(q32), heads3(k32), heads3(v32), *cache_args, bc, b0)
    return o.reshape(DEC_BATCH, GROUP_COLS).astype(BF16)


SCAN_CHUNK = 64


def _softplus(x):
    return jnp.maximum(x, 0.0) + jnp.log1p(jnp.exp(-jnp.abs(x)))


def _gates(conv, wr_ref, br_ref, wi_ref, bi_ref, l_ref):
    cb = conv.astype(BF16)
    r = _sigmoid(jnp.dot(cb, wr_ref[0].astype(BF16), preferred_element_type=F32) + br_ref[...])
    ig = _sigmoid(jnp.dot(cb, wi_ref[0].astype(BF16), preferred_element_type=F32) + bi_ref[...])
    log_a = -LRU_C * r * _softplus(-l_ref[...])
    return log_a, ig


def _decay(log_a):
    a = jnp.exp(log_a)
    return a, jnp.sqrt(-jnp.tanh(log_a) * (a * a + 1.0))


def _rec_p_kernel(xb_ref, gb_ref, cw_ref, cb_ref, wr_ref, br_ref, wi_ref, bi_ref, l_ref,
                  y_ref, h_ref, a_scr, b_scr):
    t = SEQ
    x = xb_ref[...]
    row = lax.broadcasted_iota(jnp.int32, (t, LRU_BLOCK), 0)
    conv = cb_ref[...] + x * cw_ref[CONV_WIDTH - 1:CONV_WIDTH, :]
    for s in range(1, CONV_WIDTH):
        xs = jnp.where(row >= s, pltpu.roll(x, s, 0), 0.0)
        conv = conv + xs * cw_ref[CONV_WIDTH - 1 - s:CONV_WIDTH - s, :]
    log_a, ig = _gates(conv, wr_ref, br_ref, wi_ref, bi_ref, l_ref)
    a, mult = _decay(log_a)
    start = row == 0
    a_scr[...] = jnp.where(start, 0.0, a)
    b_scr[...] = jnp.where(start, 1.0, mult) * ig * conv

    crow = lax.broadcasted_iota(jnp.int32, (SCAN_CHUNK, LRU_BLOCK), 0)

    def chunk(c, h_prev):
        rows = pl.ds(pl.multiple_of(c * SCAN_CHUNK, SCAN_CHUNK), SCAN_CHUNK)
        a_c = a_scr[rows, :]
        b_c = b_scr[rows, :]
        s = 1
        while s < SCAN_CHUNK:
            keep = crow >= s
            a_sh = jnp.where(keep, pltpu.roll(a_c, s, 0), 1.0)
            b_sh = jnp.where(keep, pltpu.roll(b_c, s, 0), 0.0)
            b_c = a_c * b_sh + b_c
            a_c = a_c * a_sh
            s *= 2
        h = a_c * h_prev + b_c
        y_ref[rows, :] = (h * _gelu(gb_ref[rows, :])).astype(BF16)
        return h[SCAN_CHUNK - 1:SCAN_CHUNK, :]

    h_last = lax.fori_loop(0, t // SCAN_CHUNK, chunk, jnp.zeros((1, LRU_BLOCK), F32))
    h_ref[0] = h_last


def _rec_prompt(xg, conv_w, conv_b, w_rg, b_rg, w_ig, b_ig, lru_l):
    nb = LRU_BLOCKS
    vec = pl.BlockSpec((1, LRU_BLOCK), lambda b, n: (0, n))
    wblk = pl.BlockSpec((1, LRU_BLOCK, LRU_BLOCK), lambda b, n: (n, 0, 0))
    y, h = pl.pallas_call(
        _rec_p_kernel,
        grid=(BATCH, nb),
        in_specs=[
            pl.BlockSpec((SEQ, LRU_BLOCK), lambda b, n: (b, n)),
            pl.BlockSpec((SEQ, LRU_BLOCK), lambda b, n: (b, nb + n)),
            pl.BlockSpec((CONV_WIDTH, LRU_BLOCK), lambda b, n: (0, n)),
            vec, wblk, vec, wblk, vec, vec,
        ],
        out_specs=[
            pl.BlockSpec((SEQ, LRU_BLOCK), lambda b, n: (b, n)),
            pl.BlockSpec((1, 1, LRU_BLOCK), lambda b, n: (b, 0, n)),
        ],
        out_shape=[
            jax.ShapeDtypeStruct((N_PROMPT, LRU_WIDTH), BF16),
            jax.ShapeDtypeStruct((BATCH, 1, LRU_WIDTH), F32),
        ],
        scratch_shapes=[pltpu.VMEM((SEQ, LRU_BLOCK), F32), pltpu.VMEM((SEQ, LRU_BLOCK), F32)],
        compiler_params=_cparams("arbitrary", "arbitrary"),
        name="rec_p",
    )(xg, xg, conv_w, conv_b, w_rg, b_rg, w_ig, b_ig, lru_l)
    return y, h.reshape(BATCH, LRU_WIDTH)


def _rec_s_kernel(xb_ref, gb_ref, c0_ref, c1_ref, c2_ref, h0_ref, cw_ref, cb_ref,
                  wr_ref, br_ref, wi_ref, bi_ref, l_ref, y_ref, h_ref):
    conv = (cb_ref[...] + c0_ref[...] * cw_ref[0:1, :] + c1_ref[...] * cw_ref[1:2, :]
            + c2_ref[...] * cw_ref[2:3, :] + xb_ref[...] * cw_ref[3:4, :])
    log_a, ig = _gates(conv, wr_ref, br_ref, wi_ref, bi_ref, l_ref)
    a, mult = _decay(log_a)
    h = a * h0_ref[...] + mult * ig * conv
    h_ref[...] = h
    y_ref[...] = (h * _gelu(gb_ref[...])).astype(BF16)


def _rec_sample(xg, state_conv, h0, conv_w, conv_b, w_rg, b_rg, w_ig, b_ig, lru_l):
    nb = LRU_BLOCKS
    sc = state_conv.reshape(DEC_BATCH, (CONV_WIDTH - 1) * LRU_WIDTH)
    srow = N_PROMPT // DEC_BATCH
    vec = pl.BlockSpec((1, LRU_BLOCK), lambda n: (0, n))
    wblk = pl.BlockSpec((1, LRU_BLOCK, LRU_BLOCK), lambda n: (n, 0, 0))
    tile = lambda f: pl.BlockSpec((DEC_BATCH, LRU_BLOCK), f)
    return pl.pallas_call(
        _rec_s_kernel,
        grid=(nb,),
        in_specs=[
            tile(lambda n: (srow, n)),
            tile(lambda n: (srow, nb + n)),
            tile(lambda n: (0, n)),
            tile(lambda n: (0, nb + n)),
            tile(lambda n: (0, 2 * nb + n)),
            tile(lambda n: (0, n)),
            pl.BlockSpec((CONV_WIDTH, LRU_BLOCK), lambda n: (0, n)),
            vec, wblk, vec, wblk, vec, vec,
        ],
        out_specs=[tile(lambda n: (0, n)), tile(lambda n: (0, n))],
        out_shape=[
            jax.ShapeDtypeStruct((DEC_BATCH, LRU_WIDTH), BF16),
            jax.ShapeDtypeStruct((DEC_BATCH, LRU_WIDTH), F32),
        ],
        compiler_params=_cparams("arbitrary"),
        name="rec_s",
    )(xg, xg, sc, sc, sc, h0, conv_w, conv_b, w_rg, b_rg, w_ig, b_ig, lru_l)


def _mix_kernel(at_ref, rc_ref, ga_ref, gr_ref, wa_ref, wb_ref, o_ref, wa_bf, wb_bf):
    @pl.when(pl.program_id(1) == 0)
    def _():
        wa_bf[...] = wa_ref[...].astype(BF16)
        wb_bf[...] = wb_ref[...].astype(BF16)

    pa = jnp.dot(at_ref[...], wa_bf[...], preferred_element_type=F32)
    pb = jnp.dot(rc_ref[...], wb_bf[...], preferred_element_type=F32)
    o_ref[...] = (ga_ref[...].astype(F32) * pa + gr_ref[...].astype(F32) * pb).astype(BF16)


def _mix(attn, rec, gates, w_pa, w_pb):
    nj = D_MODEL // MM_TN
    return pl.pallas_call(
        _mix_kernel,
        grid=(nj, N_TOK // MM_TM),
        in_specs=[
            pl.BlockSpec((MM_TM, GROUP_COLS), lambda j, i: (i, 0)),
            pl.BlockSpec((MM_TM, LRU_WIDTH), lambda j, i: (i, 0)),
            pl.BlockSpec((MM_TM, MM_TN), lambda j, i: (i, j)),
            pl.BlockSpec((MM_TM, MM_TN), lambda j, i: (i, nj + j)),
            pl.BlockSpec((GROUP_COLS, MM_TN), lambda j, i: (0, j)),
            pl.BlockSpec((LRU_WIDTH, MM_TN), lambda j, i: (0, j)),
        ],
        out_specs=pl.BlockSpec((MM_TM, MM_TN), lambda j, i: (i, j)),
        out_shape=jax.ShapeDtypeStruct((N_TOK, D_MODEL), BF16),
        scratch_shapes=[pltpu.VMEM((GROUP_COLS, MM_TN), BF16), pltpu.VMEM((LRU_WIDTH, MM_TN), BF16)],
        compiler_params=_cparams("arbitrary", "arbitrary"),
        name="mix",
    )(attn, rec, gates, gates, w_pa, w_pb)


CACHE_ROWS = 128


def _wo_caches_kernel(*refs):
    m_ref, w_ref, x_ref = refs[0:3]
    src = refs[3:9]
    o_ref = refs[9]
    dst = refs[10:16]
    wbf_ref = refs[16]

    @pl.when(pl.program_id(1) == 0)
    def _():
        wbf_ref[...] = w_ref[...].astype(BF16)

    o_ref[...] = x_ref[...] + jnp.dot(m_ref[...], wbf_ref[...], preferred_element_type=F32)
    for s, d in zip(src, dst):
        d[...] = s[...]


def _wo_and_caches(merged, w_o, x_all, k_heads, v_heads):
    nm = N_TOK // MM_TM
    grid = (D_MODEL // MM_TN, nm)
    blk = (None, CACHE_ROWS, HEADS_PER_GROUP, HEAD_DIM)
    per_batch = SEQ // CACHE_ROWS
    in_specs, out_specs, out_shape, args = [], [], [], []
    for g, (window, _) in enumerate(ATTN_GROUPS):
        keep = min(window, SEQ) // CACHE_ROWS
        n_slabs = BATCH * keep
        assert grid[0] * grid[1] >= n_slabs

        def slab(j, i, n_slabs=n_slabs):
            return jnp.minimum(j * nm + i, n_slabs - 1)

        def src_idx(j, i, g=g, keep=keep, slab=slab):
            s = slab(j, i)
            return (g, (s // keep) * per_batch + (per_batch - keep) + s % keep, 0, 0)

        for a in (k_heads, v_heads):
            args.append(a)
            in_specs.append(pl.BlockSpec(blk, src_idx))
            out_specs.append(pl.BlockSpec(blk[1:], lambda j, i, slab=slab: (slab(j, i), 0, 0)))
            out_shape.append(jax.ShapeDtypeStruct((n_slabs * CACHE_ROWS, HEADS_PER_GROUP, HEAD_DIM), F32))
    tile = pl.BlockSpec((MM_TM, MM_TN), lambda j, i: (i, j))
    outs = pl.pallas_call(
        _wo_caches_kernel,
        grid=grid,
        in_specs=[
            pl.BlockSpec((MM_TM, D_MODEL), lambda j, i: (i, 0)),
            pl.BlockSpec((D_MODEL, MM_TN), lambda j, i: (0, j)),
            tile, *in_specs,
        ],
        out_specs=[tile, *out_specs],
        out_shape=[jax.ShapeDtypeStruct((N_TOK, D_MODEL), F32), *out_shape],
        scratch_shapes=[pltpu.VMEM((D_MODEL, MM_TN), BF16)],
        compiler_params=_cparams("arbitrary", "arbitrary"),
        name="wo_caches",
    )(merged, w_o, x_all, *args)
    return outs[0], outs[1:]


def _norm2_kernel(x_ref, g_ref, xn_ref, xnt_ref):
    x = x_ref[...]
    ms = jnp.mean(x * x, axis=-1, keepdims=True)
    y = x * lax.rsqrt(ms + EPS) * g_ref[...]
    xn_ref[...] = y.astype(BF16)
    xnt_ref[...] = y.T.astype(BF16)


def _norm2(x1, g):
    return pl.pallas_call(
        _norm2_kernel,
        grid=(N_ROW_BLK,),
        in_specs=[
            pl.BlockSpec((ROW_BLK, D_MODEL), lambda i: (i, 0)),
            pl.BlockSpec((1, D_MODEL), lambda i: (0, 0)),
        ],
        out_specs=[
            pl.BlockSpec((ROW_BLK, D_MODEL), lambda i: (i, 0)),
            pl.BlockSpec((D_MODEL, ROW_BLK), lambda i: (0, i)),
        ],
        out_shape=[
            jax.ShapeDtypeStruct((N_TOK, D_MODEL), BF16),
            jax.ShapeDtypeStruct((D_MODEL, N_TOK), BF16),
        ],
        compiler_params=_cparams("arbitrary"),
        name="norm2",
    )(x1, g)


NO_RANK = float(PEER_KEYS - 1)


def _top_values(x, k, with_rank=False):
    rows = lax.broadcasted_iota(jnp.int32, x.shape, 0).astype(F32)
    rank = jnp.full(x.shape, NO_RANK, F32)
    out = []
    for j in range(k):
        m = jnp.max(x, axis=0, keepdims=True)
        first = jnp.min(jnp.where(x == m, rows, float(x.shape[0])), axis=0, keepdims=True)
        out.append(m)
        hit = rows == first
        x = jnp.where(hit, -jnp.inf, x)
        if with_rank:
            rank = jnp.where(hit, float(j), rank)
    return (out, rank) if with_rank else out


def _peer_prep_kernel(x_ref, wq_ref, k1_ref, k2_ref, e1_ref, n_ref, rank_ref, e2_ref, wq_bf):
    @pl.when(pl.program_id(1) == 0)
    def _():
        wq_bf[...] = wq_ref[...].astype(BF16)

    half = PEER_DKEY // 2
    q = jnp.dot(x_ref[...], wq_bf[...], preferred_element_type=F32).astype(BF16)
    nt = (((1,), (1,)), ((), ()))
    s1 = lax.dot_general(k1_ref[0].astype(BF16), q[:, :half], nt, preferred_element_type=F32)
    s2 = lax.dot_general(k2_ref[0].astype(BF16), q[:, half:], nt, preferred_element_type=F32)
    for c in range(PREP_TM // 128):
        cl = slice(c * 128, (c + 1) * 128)
        x1, x2 = s1[:, cl], s2[:, cl]
        a = _top_values(x1, PEER_TOPK)
        b, rank2 = _top_values(x2, PEER_TOPK, with_rank=True)
        cand = [a[i] + b[j] for i in range(PEER_TOPK) for j in range(PEER_TOPK) if (i + 1) * (j + 1) <= PEER_TOPK]
        cand += [jnp.full_like(a[0], -jnp.inf)] * (-len(cand) % 8)
        top = _top_values(jnp.concatenate(cand, axis=0), PEER_TOPK)
        tau = top[PEER_TOPK - 1]
        z = jnp.ones_like(top[0])
        for v in top[1:]:
            z = z + jnp.exp(v - top[0])
        n = jnp.zeros_like(x1)
        for k, bk in enumerate(b):
            n = jnp.where(x1 + bk >= tau, float(k + 1), n)
        e1_ref[0, :, cl] = jnp.exp(x1 - a[0]) / z
        n_ref[0, :, cl] = n
        rank_ref[0, :, cl] = rank2.astype(BF16)
        e2_ref[0, :, cl] = jnp.exp(x2 - b[0]).astype(BF16)


PREP_TM = 640


def _peer_prep(xn2, w_query, k1, k2):
    tm = PREP_TM
    half = PEER_DKEY // 2
    big = pl.BlockSpec((1, PEER_KEYS, tm), lambda h, i: (h, 0, i))
    keys = pl.BlockSpec((1, PEER_KEYS, half), lambda h, i: (h, 0, 0))
    shp = lambda dt: jax.ShapeDtypeStruct((PEER_HEADS, PEER_KEYS, N_TOK), dt)
    return pl.pallas_call(
        _peer_prep_kernel,
        grid=(PEER_HEADS, N_TOK // tm),
        in_specs=[
            pl.BlockSpec((tm, D_MODEL), lambda h, i: (i, 0)),
            pl.BlockSpec((D_MODEL, PEER_DKEY), lambda h, i: (0, h)),
            keys, keys,
        ],
        out_specs=[big, big, big, big],
        out_shape=[shp(F32), shp(F32), shp(BF16), shp(BF16)],
        scratch_shapes=[pltpu.VMEM((D_MODEL, PEER_DKEY), BF16)],
        compiler_params=_cparams("arbitrary", "arbitrary"),
        name="peer_prep",
    )(xn2, w_query, k1, k2)


I1_PER_TILE = PEER_TE // PEER_KEYS
I1_BLK = 8


def _peer_kernel(xt_ref, u_ref, vt_ref, e1_ref, n_ref, rank_ref, e2_ref, o_ref, w_scr):
    e = pl.program_id(1)

    @pl.when(e == 0)
    def _():
        o_ref[...] = jnp.zeros_like(o_ref)

    st = jnp.dot(u_ref[...], xt_ref[...], preferred_element_type=F32)
    off = (e % (I1_BLK // I1_PER_TILE)) * I1_PER_TILE
    zero = jnp.zeros((), BF16)
    for r in range(I1_PER_TILE):
        rows = slice(r * PEER_KEYS, (r + 1) * PEER_KEYS)
        coef = jnp.zeros((PEER_KEYS, PEER_TM), BF16)
        for h in range(PEER_HEADS):
            n = n_ref[h, pl.ds(off + r, 1), :].astype(BF16)
            g1 = e1_ref[h, pl.ds(off + r, 1), :].astype(BF16)
            coef = coef + g1 * jnp.where(rank_ref[h] < n, e2_ref[h], zero)
        w_scr[rows, :] = coef * _gelu(st[rows, :]).astype(BF16)
    o_ref[...] += jnp.dot(vt_ref[...], w_scr[...], preferred_element_type=F32)


def _peer(xn2t, u_bf, vt_bf, e1, n, rank2, e2):
    nt = pl.cdiv(N_TOK, PEER_TM)
    ne = PEER_EXPERTS // PEER_TE
    per = I1_BLK // I1_PER_TILE
    small = pl.BlockSpec((PEER_HEADS, I1_BLK, PEER_TM), lambda t, e: (0, e // per, t))
    big = pl.BlockSpec((PEER_HEADS, PEER_KEYS, PEER_TM), lambda t, e: (0, 0, t))
    return pl.pallas_call(
        _peer_kernel,
        grid=(nt, ne),
        in_specs=[
            pl.BlockSpec((D_MODEL, PEER_TM), lambda t, e: (0, t)),
            pl.BlockSpec((PEER_TE, D_MODEL), lambda t, e: (e, 0)),
            pl.BlockSpec((D_MODEL, PEER_TE), lambda t, e: (0, e)),
            small, small, big, big,
        ],
        out_specs=pl.BlockSpec((D_MODEL, PEER_TM), lambda t, e: (0, t)),
        out_shape=jax.ShapeDtypeStruct((D_MODEL, N_TOK), F32),
        scratch_shapes=[pltpu.VMEM((PEER_TE, PEER_TM), BF16)],
        compiler_params=_cparams("arbitrary", "arbitrary"),
        name="peer",
    )(xn2t, u_bf, vt_bf, e1, n, rank2, e2)


def _final_kernel(x_ref, pt_ref, yp_ref, ys_ref):
    i = pl.program_id(0)
    y = x_ref[...] + pt_ref[...].T

    @pl.when(i < N_PROMPT_BLK)
    def _():
        yp_ref[...] = y

    @pl.when(i == N_PROMPT_BLK)
    def _():
        ys_ref[...] = y


def _final(x1, peer_t):
    return pl.pallas_call(
        _final_kernel,
        grid=(N_ROW_BLK,),
        in_specs=[
            pl.BlockSpec((ROW_BLK, D_MODEL), lambda i: (i, 0)),
            pl.BlockSpec((D_MODEL, ROW_BLK), lambda i: (0, i)),
        ],
        out_specs=[
            pl.BlockSpec((ROW_BLK, D_MODEL), lambda i: (jnp.minimum(i, N_PROMPT_BLK - 1), 0)),
            pl.BlockSpec((ROW_BLK, D_MODEL), lambda i: (0, 0)),
        ],
        out_shape=[
            jax.ShapeDtypeStruct((N_PROMPT, D_MODEL), F32),
            jax.ShapeDtypeStruct((DEC_BATCH, D_MODEL), F32),
        ],
        compiler_params=_cparams("arbitrary"),
        name="final",
    )(x1, peer_t)


def _bucket(dist):
    max_exact = N_BUCKETS // 2
    dd = jnp.maximum(dist, max_exact).astype(F32)
    large = max_exact + (jnp.log(dd / max_exact) / math.log(RP_MAX_DIST / max_exact)
                         * (N_BUCKETS - max_exact)).astype(jnp.int32)
    large = jnp.minimum(large, N_BUCKETS - 1)
    return jnp.where(dist < max_exact, dist, large)


def _group_bias(rel_bias, g, d):
    dist = d * jnp.arange(N_KEYS, dtype=jnp.int32)
    bias = rel_bias[:, g * HEADS_PER_GROUP:(g + 1) * HEADS_PER_GROUP][_bucket(dist)]
    pad = jnp.full((HEADS_PER_GROUP, 2 * ROW_BLK - N_KEYS), NEG, F32)
    ext = jnp.concatenate([bias[::-1].T, pad], axis=1)
    return bias, ext[:, None, :]


def kernel(x_prompt, x_sample, cache_k_w128, cache_v_w128, cache_k_w512, cache_v_w512, cache_k_w2048,
           cache_v_w2048, state_lru_h, state_conv, rel_bias, ln1_g, w_in, q_norm_g, k_norm_g, conv_w,
           conv_b, w_rg, b_rg, w_ig, b_ig, lru_L, w_pa, w_pb, w_o, ln2_g, w_query, sub_keys_1,
           sub_keys_2, peer_u, peer_v):
    caches = ((cache_k_w128, cache_v_w128), (cache_k_w512, cache_v_w512), (cache_k_w2048, cache_v_w2048))
    xp = x_prompt.reshape(N_PROMPT, D_MODEL)
    xs = x_sample.reshape(DEC_BATCH, D_MODEL)
    w = w_in[0]

    xn, x_all = _norm1(xp, xs, ln1_g)
    gspec = [pl.BlockSpec((1, HEAD_DIM), lambda j, i: (0, 0))]
    wide = dict(tm=IN_TM, tn=IN_TN)
    (q32,) = _proj(xn, w, Q_OFF, ATTN_WIDTH, _epi_q, (F32,), (q_norm_g,), gspec, **wide)
    k32, k_heads = _proj(xn, w, K_OFF, ATTN_WIDTH, _epi_k, (F32,), (k_norm_g,), gspec, head_major=True, **wide)
    v32, v_heads = _proj(xn, w, V_OFF, ATTN_WIDTH, _epi_v, (F32,), head_major=True, **wide)
    (xg,) = _proj(xn, w, XB_OFF, 2 * LRU_WIDTH, _epi_f32, (F32,), **wide)
    gates, u_bf, vt_bf = _gates_and_tables(xn, w, peer_u[0], peer_v[0])

    biases, exts = zip(*[_group_bias(rel_bias, g, d) for g, (_, d) in enumerate(ATTN_GROUPS)])
    attn_p = _attn_prompt(q32, k32, v32, jnp.concatenate(exts, axis=0))
    attn_s = _attn_sample(q32, k32, v32, [(ck[0], cv[0]) for ck, cv in caches], biases)
    attn = jnp.concatenate([attn_p, attn_s], axis=0)

    rec_p, h_p = _rec_prompt(xg, conv_w[0], conv_b, w_rg[0], b_rg, w_ig[0], b_ig, lru_L)
    rec_s, h_s = _rec_sample(xg, state_conv[0], state_lru_h[0], conv_w[0], conv_b, w_rg[0], b_rg,
                             w_ig[0], b_ig, lru_L)
    rec = jnp.concatenate([rec_p, rec_s], axis=0)

    merged = _mix(attn, rec, gates, w_pa[0], w_pb[0])
    x1, prompt_caches = _wo_and_caches(merged, w_o[0], x_all, k_heads, v_heads)

    xn2, xn2t = _norm2(x1, ln2_g)
    e1, nsel, rank2, e2 = _peer_prep(xn2, w_query[0], sub_keys_1[0], sub_keys_2[0])
    peer_t = _peer(xn2t, u_bf, vt_bf, e1, nsel, rank2, e2)
    y_p, y_s = _final(x1, peer_t)

    outs = [y_p.reshape(BATCH, SEQ, D_MODEL), y_s.reshape(DEC_BATCH, 1, D_MODEL)]
    for g, (window, d) in enumerate(ATTN_GROUPS):
        keep = min(window, SEQ)
        for a in prompt_caches[2 * g:2 * g + 2]:
            outs.append(a.reshape(1, BATCH, keep, HEADS_PER_GROUP, HEAD_DIM))
        for a in (k_heads, v_heads):
            outs.append(a[g, N_PROMPT:].reshape(1, DEC_BATCH, 1, HEADS_PER_GROUP, HEAD_DIM))
    xb_p = xg[:N_PROMPT, :LRU_WIDTH].reshape(BATCH, SEQ, LRU_WIDTH)
    xb_s = xg[N_PROMPT:, :LRU_WIDTH]
    outs += [
        h_p[None], h_s[None],
        xb_p[:, SEQ - (CONV_WIDTH - 1):][None],
        jnp.concatenate([state_conv[0][:, 1:], xb_s[:, None, :]], axis=1)[None],
    ]
    return tuple(outs)
```

```python
import functools
import math

import jax
import jax.numpy as jnp
import numpy as np
from jax import lax
from jax.experimental import pallas as pl
from jax.experimental.pallas import tpu as pltpu

F32 = jnp.float32
BF16 = jnp.bfloat16

D_MODEL = 4096
BATCH = 4
SEQ = 2048
DEC_BATCH = 128
N_PROMPT = BATCH * SEQ
N_TOK = N_PROMPT + DEC_BATCH
HEAD_DIM = 128
HEADS_PER_GROUP = 8
GROUP_COLS = HEADS_PER_GROUP * HEAD_DIM
ATTN_GROUPS = ((128, 1), (512, 4), (2048, 16))
N_GROUPS = 3
ATTN_WIDTH = N_GROUPS * GROUP_COLS
N_KEYS = 129
N_BUCKETS = 32
RP_MAX_DIST = 2048
LRU_WIDTH = 2048
LRU_BLOCKS = 16
LRU_BLOCK = 128
CONV_WIDTH = 4
LRU_C = 8.0
PEER_HEADS = 8
PEER_KEYS = 128
PEER_EXPERTS = PEER_KEYS * PEER_KEYS
PEER_DKEY = 256
PEER_TOPK = 16
EPS = 1e-6

Q_OFF = 0
K_OFF = ATTN_WIDTH
V_OFF = 2 * ATTN_WIDTH
XB_OFF = 3 * ATTN_WIDTH
GA_OFF = XB_OFF + 2 * LRU_WIDTH

ROW_BLK = 128
N_PROMPT_BLK = N_PROMPT // ROW_BLK
N_ROW_BLK = N_TOK // ROW_BLK
MM_TM = 1040
MM_TN = 512
IN_TM = 416
IN_TN = 1024
PEER_TM = 512
PEER_TE = 512
NEG = -1e30
VMEM_LIMIT = 58 * 1024 * 1024


def _cparams(*sem):
    return pltpu.CompilerParams(dimension_semantics=sem, vmem_limit_bytes=VMEM_LIMIT)


def _gelu(x):
    return jax.nn.gelu(x)


def _sigmoid(x):
    return 1.0 / (1.0 + jnp.exp(-x))


def _norm1_kernel(xp_ref, xs_ref, g_ref, xn_ref, xall_ref):
    i = pl.program_id(0)

    def body(x):
        ms = jnp.mean(x * x, axis=-1, keepdims=True)
        xn_ref[...] = (x * lax.rsqrt(ms + EPS) * g_ref[...]).astype(BF16)
        xall_ref[...] = x

    @pl.when(i < N_PROMPT_BLK)
    def _():
        body(xp_ref[...])

    @pl.when(i == N_PROMPT_BLK)
    def _():
        body(xs_ref[...])


def _norm1(xp, xs, g):
    return pl.pallas_call(
        _norm1_kernel,
        grid=(N_ROW_BLK,),
        in_specs=[
            pl.BlockSpec((ROW_BLK, D_MODEL), lambda i: (jnp.minimum(i, N_PROMPT_BLK - 1), 0)),
            pl.BlockSpec((ROW_BLK, D_MODEL), lambda i: (0, 0)),
            pl.BlockSpec((1, D_MODEL), lambda i: (0, 0)),
        ],
        out_specs=[
            pl.BlockSpec((ROW_BLK, D_MODEL), lambda i: (i, 0)),
            pl.BlockSpec((ROW_BLK, D_MODEL), lambda i: (i, 0)),
        ],
        out_shape=[
            jax.ShapeDtypeStruct((N_TOK, D_MODEL), BF16),
            jax.ShapeDtypeStruct((N_TOK, D_MODEL), F32),
        ],
        compiler_params=_cparams("arbitrary"),
        name="norm1",
    )(xp, xs, g)


def _proj_kernel(*refs, n_extra, epilogue):
    x_ref, w_ref = refs[0], refs[1]
    extra = refs[2:2 + n_extra]
    outs = refs[2 + n_extra:-1]
    wbf_ref = refs[-1]

    @pl.when(pl.program_id(1) == 0)
    def _():
        wbf_ref[...] = w_ref[...].astype(BF16)

    acc = jnp.dot(x_ref[...], wbf_ref[...], preferred_element_type=F32)
    epilogue(acc, extra, outs)


def _proj(x, w, col0, ncols, epilogue, out_dtypes, extra=(), extra_specs=(), tm=MM_TM, tn=MM_TN,
          head_major=False):
    m, k = x.shape
    grid = (ncols // tn, m // tm)
    cb0 = col0 // tn
    kern = functools.partial(_proj_kernel, n_extra=len(extra), epilogue=epilogue)
    out_specs = [pl.BlockSpec((tm, tn), lambda j, i: (i, j)) for _ in out_dtypes]
    out_shape = [jax.ShapeDtypeStruct((m, ncols), dt) for dt in out_dtypes]
    if head_major:
        assert tn == GROUP_COLS
        out_specs.append(pl.BlockSpec((None, tm, HEADS_PER_GROUP, HEAD_DIM), lambda j, i: (j, i, 0, 0)))
        out_shape.append(jax.ShapeDtypeStruct((ncols // tn, m, HEADS_PER_GROUP, HEAD_DIM), F32))
    return pl.pallas_call(
        kern,
        grid=grid,
        in_specs=[
            pl.BlockSpec((tm, k), lambda j, i: (i, 0)),
            pl.BlockSpec((k, tn), lambda j, i: (0, cb0 + j)),
            *extra_specs,
        ],
        out_specs=out_specs,
        out_shape=out_shape,
        scratch_shapes=[pltpu.VMEM((k, tn), BF16)],
        compiler_params=_cparams("arbitrary", "arbitrary"),
        name="proj",
    )(x, w, *extra)


def _head_norm(a, g):
    ms = jnp.mean(a * a, axis=-1, keepdims=True)
    return a * lax.rsqrt(ms + EPS) * g


def _epi_q(acc, extra, outs):
    g = extra[0][...]
    for h in range(acc.shape[1] // HEAD_DIM):
        sl = slice(h * HEAD_DIM, (h + 1) * HEAD_DIM)
        outs[0][:, sl] = _head_norm(acc[:, sl], g) * (HEAD_DIM ** -0.5)


def _epi_k(acc, extra, outs):
    g = extra[0][...]
    for h in range(acc.shape[1] // HEAD_DIM):
        sl = slice(h * HEAD_DIM, (h + 1) * HEAD_DIM)
        y = _head_norm(acc[:, sl], g)
        outs[0][:, sl] = y
        outs[1][:, h, :] = y


def _epi_v(acc, extra, outs):
    outs[0][...] = acc
    for h in range(acc.shape[1] // HEAD_DIM):
        outs[1][:, h, :] = acc[:, h * HEAD_DIM:(h + 1) * HEAD_DIM]


def _epi_f32(acc, extra, outs):
    outs[0][...] = acc


GATE_TM = 832
GATE_TN = 512
TABLE_ROWS = 128


def _gates_tables_kernel(x_ref, w_ref, pu_ref, pv_ref, g_ref, ub_ref, vt_ref, wbf_ref):
    @pl.when(pl.program_id(1) == 0)
    def _():
        wbf_ref[...] = w_ref[...].astype(BF16)

    acc = jnp.dot(x_ref[...], wbf_ref[...], preferred_element_type=F32)
    g_ref[...] = _sigmoid(acc).astype(BF16)
    ub_ref[...] = pu_ref[...].astype(BF16)
    vt_ref[...] = pv_ref[...].T.astype(BF16)


def _gates_and_tables(xn, w, peer_u, peer_v):
    ncols = 2 * D_MODEL
    nm = N_TOK // GATE_TM
    grid = (ncols // GATE_TN, nm)
    n_slabs = PEER_EXPERTS // TABLE_ROWS
    assert grid[0] * grid[1] >= n_slabs
    cb0 = GA_OFF // GATE_TN
    slab = lambda j, i: jnp.minimum(j * nm + i, n_slabs - 1)
    return pl.pallas_call(
        _gates_tables_kernel,
        grid=grid,
        in_specs=[
            pl.BlockSpec((GATE_TM, D_MODEL), lambda j, i: (i, 0)),
            pl.BlockSpec((D_MODEL, GATE_TN), lambda j, i: (0, cb0 + j)),
            pl.BlockSpec((TABLE_ROWS, D_MODEL), lambda j, i: (slab(j, i), 0)),
            pl.BlockSpec((TABLE_ROWS, D_MODEL), lambda j, i: (slab(j, i), 0)),
        ],
        out_specs=[
            pl.BlockSpec((GATE_TM, GATE_TN), lambda j, i: (i, j)),
            pl.BlockSpec((TABLE_ROWS, D_MODEL), lambda j, i: (slab(j, i), 0)),
            pl.BlockSpec((D_MODEL, TABLE_ROWS), lambda j, i: (0, slab(j, i))),
        ],
        out_shape=[
            jax.ShapeDtypeStruct((N_TOK, ncols), BF16),
            jax.ShapeDtypeStruct((PEER_EXPERTS, D_MODEL), BF16),
            jax.ShapeDtypeStruct((D_MODEL, PEER_EXPERTS), BF16),
        ],
        scratch_shapes=[pltpu.VMEM((D_MODEL, GATE_TN), BF16)],
        compiler_params=_cparams("arbitrary", "arbitrary"),
        name="gates_tables",
    )(xn, w, peer_u, peer_v)


def _attn_p_kernel(*refs):
    q_refs, k_refs, v_refs, ext_refs = refs[0:3], refs[3:6], refs[6:9], refs[9:12]
    out_ref = refs[12]
    o_scr, l_scr = refs[13:16], refs[16:19]
    nt = (((1,), (1,)), ((), ()))

    for g, (_, d) in enumerate(ATTN_GROUPS):
        q_ref, k_ref, v_ref = q_refs[g], k_refs[g], v_refs[g]
        band = pltpu.roll(jnp.broadcast_to(ext_refs[g][0], (ROW_BLK, 2 * ROW_BLK)), 0, 1,
                          stride=1, stride_axis=0)
        band_cur = band[:, ROW_BLK:]
        nblk = SEQ // d // ROW_BLK

        def rows(r, i, d=d):
            start = r + (d * ROW_BLK) * i
            return pl.ds(start, ROW_BLK) if d == 1 else pl.ds(start, ROW_BLK, stride=d)

        def blocks(todo, g=g, q_ref=q_ref, k_ref=k_ref, v_ref=v_ref, band=band, band_cur=band_cur, rows=rows):
            staged = []
            for r, i, first in todo:
                cur = rows(r, i)
                q = q_ref[cur, :].astype(BF16)
                if first:
                    kk = k_ref[cur, :].astype(BF16)
                    vv = v_ref[cur, :].astype(BF16)
                    bias = band_cur
                else:
                    prev = rows(r, i - 1)
                    kk = jnp.concatenate([k_ref[prev, :], k_ref[cur, :]], axis=0).astype(BF16)
                    vv = jnp.concatenate([v_ref[prev, :], v_ref[cur, :]], axis=0).astype(BF16)
                    bias = band
                staged.append((cur, vv, lax.dot_general(q, kk, nt, preferred_element_type=F32) + bias))
            soft = []
            for cur, vv, lg in staged:
                m = jnp.max(lg, axis=-1, keepdims=True)
                p = jnp.exp(lg - m)
                s = jnp.sum(p, axis=-1, keepdims=True)
                soft.append((cur, vv, p.astype(BF16), s, m))
            for cur, vv, p, s, m in soft:
                o_scr[g][cur, :] = jnp.dot(p, vv, preferred_element_type=F32) / s
                l_scr[g][cur, :] = jnp.broadcast_to(m + jnp.log(s), (ROW_BLK, HEAD_DIM))

        if d == 1:
            blocks([(0, 0, True)])
            per_trip = 5

            def trip(t, carry, blocks=blocks):
                blocks([(0, 1 + per_trip * t + u, False) for u in range(per_trip)])
                return carry

            lax.fori_loop(0, (nblk - 1) // per_trip, trip, 0)
        elif nblk > 1:
            def trip(t, carry, blocks=blocks, nblk=nblk):
                blocks([(r, i, i == 0) for r in (2 * t, 2 * t + 1) for i in range(nblk)])
                return carry

            lax.fori_loop(0, d // 2, trip, 0)
        else:
            per_trip = 8

            def trip(t, carry, blocks=blocks):
                blocks([(per_trip * t + u, 0, True) for u in range(per_trip)])
                return carry

            lax.fori_loop(0, d // per_trip, trip, 0)

    l0, l1, l2 = l_scr[0][...], l_scr[1][...], l_scr[2][...]
    m = jnp.maximum(jnp.maximum(l0, l1), l2)
    e0, e1, e2 = jnp.exp(l0 - m), jnp.exp(l1 - m), jnp.exp(l2 - m)
    acc = e0 * o_scr[0][...] + e1 * o_scr[1][...] + e2 * o_scr[2][...]
    out_ref[...] = (acc / (e0 + e1 + e2)).astype(BF16)


def _attn_prompt(q32, k32, v32, ext):
    def col(g):
        return pl.BlockSpec((SEQ, HEAD_DIM), lambda b, h: (b, g * HEADS_PER_GROUP + h))

    def erow(g):
        return pl.BlockSpec((1, 1, 2 * ROW_BLK), lambda b, h: (g * HEADS_PER_GROUP + h, 0, 0))

    groups = range(N_GROUPS)
    return pl.pallas_call(
        _attn_p_kernel,
        grid=(BATCH, HEADS_PER_GROUP),
        in_specs=[col(g) for g in groups] * 3 + [erow(g) for g in groups],
        out_specs=pl.BlockSpec((SEQ, HEAD_DIM), lambda b, h: (b, h)),
        out_shape=jax.ShapeDtypeStruct((N_PROMPT, GROUP_COLS), BF16),
        scratch_shapes=[pltpu.VMEM((SEQ, HEAD_DIM), F32) for _ in range(2 * N_GROUPS)],
        compiler_params=_cparams("arbitrary", "arbitrary"),
        name="attn_p",
    )(q32, q32, q32, k32, k32, k32, v32, v32, v32, ext, ext, ext)


SAMPLE_BB = 4


def _attn_s_kernel(*refs):
    q_ref, kn_ref, vn_ref = refs[0:3]
    ck_refs, cv_refs = refs[3:6], refs[6:9]
    bc_ref, b0_ref, o_ref = refs[9], refs[10], refs[11]

    def one(bi, carry):
        outs, lses = [], []
        for g in range(N_GROUPS):
            hs = slice(g * HEADS_PER_GROUP, (g + 1) * HEADS_PER_GROUP)
            q = q_ref[bi, hs, :]
            lc = jnp.sum(ck_refs[g][bi] * q[None], axis=-1, keepdims=True) + bc_ref[g]
            ln = jnp.sum(kn_ref[bi, hs, :] * q, axis=-1, keepdims=True) + b0_ref[g]
            m = jnp.maximum(jnp.max(lc, axis=0), ln)
            p = jnp.exp(lc - m[None])
            pn = jnp.exp(ln - m)
            s = jnp.sum(p, axis=0) + pn
            outs.append((jnp.sum(p * cv_refs[g][bi], axis=0) + pn * vn_ref[bi, hs, :]) / s)
            lses.append(m + jnp.log(s))
        m = jnp.maximum(jnp.maximum(lses[0], lses[1]), lses[2])
        e = [jnp.exp(l - m) for l in lses]
        acc = e[0] * outs[0] + e[1] * outs[1] + e[2] * outs[2]
        o_ref[bi] = acc / (e[0] + e[1] + e[2])
        return carry

    lax.fori_loop(0, SAMPLE_BB, one, 0)


def _attn_sample(q32, k32, v32, caches, biases):
    heads3 = lambda a: a[N_PROMPT:].reshape(DEC_BATCH, N_GROUPS * HEADS_PER_GROUP, HEAD_DIM)
    new = pl.BlockSpec((SAMPLE_BB, N_GROUPS * HEADS_PER_GROUP, HEAD_DIM), lambda s: (s, 0, 0))
    bc = jnp.stack([b[::-1][:N_KEYS - 1] for b in biases])[..., None]
    b0 = jnp.stack([b[0] for b in biases])[..., None]
    cache_specs, cache_args = [], []
    for which in range(2):
        for g, (window, d) in enumerate(ATTN_GROUPS):
            c = caches[g][which]
            cache_args.append(c.reshape(DEC_BATCH, c.shape[1] // d, d, HEADS_PER_GROUP, HEAD_DIM))
            cache_specs.append(pl.BlockSpec((SAMPLE_BB, N_KEYS - 1, None, HEADS_PER_GROUP, HEAD_DIM),
                                            lambda s: (s, 0, 0, 0, 0)))
    o = pl.pallas_call(
        _attn_s_kernel,
        grid=(DEC_BATCH // SAMPLE_BB,),
        in_specs=[
            new, new, new, *cache_specs,
            pl.BlockSpec((N_GROUPS, N_KEYS - 1, HEADS_PER_GROUP, 1), lambda s: (0, 0, 0, 0)),
            pl.BlockSpec((N_GROUPS, HEADS_PER_GROUP, 1), lambda s: (0, 0, 0)),
        ],
        out_specs=pl.BlockSpec((SAMPLE_BB, HEADS_PER_GROUP, HEAD_DIM), lambda s: (s, 0, 0)),
        out_shape=jax.ShapeDtypeStruct((DEC_BATCH, HEADS_PER_GROUP, HEAD_DIM), F32),
        compiler_params=_cparams("arbitrary"),
        name="attn_s",
    )(heads3(q32), heads3(k32), heads3(v32), *cache_args, bc, b0)
    return o.reshape(DEC_BATCH, GROUP_COLS).astype(BF16)


SCAN_CHUNK = 64


def _softplus(x):
    return jnp.maximum(x, 0.0) + jnp.log1p(jnp.exp(-jnp.abs(x)))


def _gates(conv, wr_ref, br_ref, wi_ref, bi_ref, l_ref):
    cb = conv.astype(BF16)
    r = _sigmoid(jnp.dot(cb, wr_ref[0].astype(BF16), preferred_element_type=F32) + br_ref[...])
    ig = _sigmoid(jnp.dot(cb, wi_ref[0].astype(BF16), preferred_element_type=F32) + bi_ref[...])
    log_a = -LRU_C * r * _softplus(-l_ref[...])
    return log_a, ig


def _decay(log_a):
    a = jnp.exp(log_a)
    return a, jnp.sqrt(-jnp.tanh(log_a) * (a * a + 1.0))


def _rec_p_kernel(xb_ref, gb_ref, cw_ref, cb_ref, wr_ref, br_ref, wi_ref, bi_ref, l_ref,
                  y_ref, h_ref, a_scr, b_scr):
    t = SEQ
    x = xb_ref[...]
    row = lax.broadcasted_iota(jnp.int32, (t, LRU_BLOCK), 0)
    conv = cb_ref[...] + x * cw_ref[CONV_WIDTH - 1:CONV_WIDTH, :]
    for s in range(1, CONV_WIDTH):
        xs = jnp.where(row >= s, pltpu.roll(x, s, 0), 0.0)
        conv = conv + xs * cw_ref[CONV_WIDTH - 1 - s:CONV_WIDTH - s, :]
    log_a, ig = _gates(conv, wr_ref, br_ref, wi_ref, bi_ref, l_ref)
    a, mult = _decay(log_a)
    start = row == 0
    a_scr[...] = jnp.where(start, 0.0, a)
    b_scr[...] = jnp.where(start, 1.0, mult) * ig * conv

    crow = lax.broadcasted_iota(jnp.int32, (SCAN_CHUNK, LRU_BLOCK), 0)

    def chunk(c, h_prev):
        rows = pl.ds(pl.multiple_of(c * SCAN_CHUNK, SCAN_CHUNK), SCAN_CHUNK)
        a_c = a_scr[rows, :]
        b_c = b_scr[rows, :]
        s = 1
        while s < SCAN_CHUNK:
            keep = crow >= s
            a_sh = jnp.where(keep, pltpu.roll(a_c, s, 0), 1.0)
            b_sh = jnp.where(keep, pltpu.roll(b_c, s, 0), 0.0)
            b_c = a_c * b_sh + b_c
            a_c = a_c * a_sh
            s *= 2
        h = a_c * h_prev + b_c
        y_ref[rows, :] = (h * _gelu(gb_ref[rows, :])).astype(BF16)
        return h[SCAN_CHUNK - 1:SCAN_CHUNK, :]

    h_last = lax.fori_loop(0, t // SCAN_CHUNK, chunk, jnp.zeros((1, LRU_BLOCK), F32))
    h_ref[0] = h_last


def _rec_prompt(xg, conv_w, conv_b, w_rg, b_rg, w_ig, b_ig, lru_l):
    nb = LRU_BLOCKS
    vec = pl.BlockSpec((1, LRU_BLOCK), lambda b, n: (0, n))
    wblk = pl.BlockSpec((1, LRU_BLOCK, LRU_BLOCK), lambda b, n: (n, 0, 0))
    y, h = pl.pallas_call(
        _rec_p_kernel,
        grid=(BATCH, nb),
        in_specs=[
            pl.BlockSpec((SEQ, LRU_BLOCK), lambda b, n: (b, n)),
            pl.BlockSpec((SEQ, LRU_BLOCK), lambda b, n: (b, nb + n)),
            pl.BlockSpec((CONV_WIDTH, LRU_BLOCK), lambda b, n: (0, n)),
            vec, wblk, vec, wblk, vec, vec,
        ],
        out_specs=[
            pl.BlockSpec((SEQ, LRU_BLOCK), lambda b, n: (b, n)),
            pl.BlockSpec((1, 1, LRU_BLOCK), lambda b, n: (b, 0, n)),
        ],
        out_shape=[
            jax.ShapeDtypeStruct((N_PROMPT, LRU_WIDTH), BF16),
            jax.ShapeDtypeStruct((BATCH, 1, LRU_WIDTH), F32),
        ],
        scratch_shapes=[pltpu.VMEM((SEQ, LRU_BLOCK), F32), pltpu.VMEM((SEQ, LRU_BLOCK), F32)],
        compiler_params=_cparams("arbitrary", "arbitrary"),
        name="rec_p",
    )(xg, xg, conv_w, conv_b, w_rg, b_rg, w_ig, b_ig, lru_l)
    return y, h.reshape(BATCH, LRU_WIDTH)


def _rec_s_kernel(xb_ref, gb_ref, c0_ref, c1_ref, c2_ref, h0_ref, cw_ref, cb_ref,
                  wr_ref, br_ref, wi_ref, bi_ref, l_ref, y_ref, h_ref):
    conv = (cb_ref[...] + c0_ref[...] * cw_ref[0:1, :] + c1_ref[...] * cw_ref[1:2, :]
            + c2_ref[...] * cw_ref[2:3, :] + xb_ref[...] * cw_ref[3:4, :])
    log_a, ig = _gates(conv, wr_ref, br_ref, wi_ref, bi_ref, l_ref)
    a, mult = _decay(log_a)
    h = a * h0_ref[...] + mult * ig * conv
    h_ref[...] = h
    y_ref[...] = (h * _gelu(gb_ref[...])).astype(BF16)


def _rec_sample(xg, state_conv, h0, conv_w, conv_b, w_rg, b_rg, w_ig, b_ig, lru_l):
    nb = LRU_BLOCKS
    sc = state_conv.reshape(DEC_BATCH, (CONV_WIDTH - 1) * LRU_WIDTH)
    srow = N_PROMPT // DEC_BATCH
    vec = pl.BlockSpec((1, LRU_BLOCK), lambda n: (0, n))
    wblk = pl.BlockSpec((1, LRU_BLOCK, LRU_BLOCK), lambda n: (n, 0, 0))
    tile = lambda f: pl.BlockSpec((DEC_BATCH, LRU_BLOCK), f)
    return pl.pallas_call(
        _rec_s_kernel,
        grid=(nb,),
        in_specs=[
            tile(lambda n: (srow, n)),
            tile(lambda n: (srow, nb + n)),
            tile(lambda n: (0, n)),
            tile(lambda n: (0, nb + n)),
            tile(lambda n: (0, 2 * nb + n)),
            tile(lambda n: (0, n)),
            pl.BlockSpec((CONV_WIDTH, LRU_BLOCK), lambda n: (0, n)),
            vec, wblk, vec, wblk, vec, vec,
        ],
        out_specs=[tile(lambda n: (0, n)), tile(lambda n: (0, n))],
        out_shape=[
            jax.ShapeDtypeStruct((DEC_BATCH, LRU_WIDTH), BF16),
            jax.ShapeDtypeStruct((DEC_BATCH, LRU_WIDTH), F32),
        ],
        compiler_params=_cparams("arbitrary"),
        name="rec_s",
    )(xg, xg, sc, sc, sc, h0, conv_w, conv_b, w_rg, b_rg, w_ig, b_ig, lru_l)


def _mix_kernel(at_ref, rc_ref, ga_ref, gr_ref, wa_ref, wb_ref, o_ref, wa_bf, wb_bf):
    @pl.when(pl.program_id(1) == 0)
    def _():
        wa_bf[...] = wa_ref[...].astype(BF16)
        wb_bf[...] = wb_ref[...].astype(BF16)

    pa = jnp.dot(at_ref[...], wa_bf[...], preferred_element_type=F32)
    pb = jnp.dot(rc_ref[...], wb_bf[...], preferred_element_type=F32)
    o_ref[...] = (ga_ref[...].astype(F32) * pa + gr_ref[...].astype(F32) * pb).astype(BF16)


def _mix(attn, rec, gates, w_pa, w_pb):
    nj = D_MODEL // MM_TN
    return pl.pallas_call(
        _mix_kernel,
        grid=(nj, N_TOK // MM_TM),
        in_specs=[
            pl.BlockSpec((MM_TM, GROUP_COLS), lambda j, i: (i, 0)),
            pl.BlockSpec((MM_TM, LRU_WIDTH), lambda j, i: (i, 0)),
            pl.BlockSpec((MM_TM, MM_TN), lambda j, i: (i, j)),
            pl.BlockSpec((MM_TM, MM_TN), lambda j, i: (i, nj + j)),
            pl.BlockSpec((GROUP_COLS, MM_TN), lambda j, i: (0, j)),
            pl.BlockSpec((LRU_WIDTH, MM_TN), lambda j, i: (0, j)),
        ],
        out_specs=pl.BlockSpec((MM_TM, MM_TN), lambda j, i: (i, j)),
        out_shape=jax.ShapeDtypeStruct((N_TOK, D_MODEL), BF16),
        scratch_shapes=[pltpu.VMEM((GROUP_COLS, MM_TN), BF16), pltpu.VMEM((LRU_WIDTH, MM_TN), BF16)],
        compiler_params=_cparams("arbitrary", "arbitrary"),
        name="mix",
    )(attn, rec, gates, gates, w_pa, w_pb)


CACHE_ROWS = 128


def _wo_caches_kernel(*refs):
    m_ref, w_ref, x_ref = refs[0:3]
    src = refs[3:9]
    o_ref = refs[9]
    dst = refs[10:16]
    wbf_ref = refs[16]

    @pl.when(pl.program_id(1) == 0)
    def _():
        wbf_ref[...] = w_ref[...].astype(BF16)

    o_ref[...] = x_ref[...] + jnp.dot(m_ref[...], wbf_ref[...], preferred_element_type=F32)
    for s, d in zip(src, dst):
        d[...] = s[...]


def _wo_and_caches(merged, w_o, x_all, k_heads, v_heads):
    nm = N_TOK // MM_TM
    grid = (D_MODEL // MM_TN, nm)
    blk = (None, CACHE_ROWS, HEADS_PER_GROUP, HEAD_DIM)
    per_batch = SEQ // CACHE_ROWS
    in_specs, out_specs, out_shape, args = [], [], [], []
    for g, (window, _) in enumerate(ATTN_GROUPS):
        keep = min(window, SEQ) // CACHE_ROWS
        n_slabs = BATCH * keep
        assert grid[0] * grid[1] >= n_slabs

        def slab(j, i, n_slabs=n_slabs):
            return jnp.minimum(j * nm + i, n_slabs - 1)

        def src_idx(j, i, g=g, keep=keep, slab=slab):
            s = slab(j, i)
            return (g, (s // keep) * per_batch + (per_batch - keep) + s % keep, 0, 0)

        for a in (k_heads, v_heads):
            args.append(a)
            in_specs.append(pl.BlockSpec(blk, src_idx))
            out_specs.append(pl.BlockSpec(blk[1:], lambda j, i, slab=slab: (slab(j, i), 0, 0)))
            out_shape.append(jax.ShapeDtypeStruct((n_slabs * CACHE_ROWS, HEADS_PER_GROUP, HEAD_DIM), F32))
    tile = pl.BlockSpec((MM_TM, MM_TN), lambda j, i: (i, j))
    outs = pl.pallas_call(
        _wo_caches_kernel,
        grid=grid,
        in_specs=[
            pl.BlockSpec((MM_TM, D_MODEL), lambda j, i: (i, 0)),
            pl.BlockSpec((D_MODEL, MM_TN), lambda j, i: (0, j)),
            tile, *in_specs,
        ],
        out_specs=[tile, *out_specs],
        out_shape=[jax.ShapeDtypeStruct((N_TOK, D_MODEL), F32), *out_shape],
        scratch_shapes=[pltpu.VMEM((D_MODEL, MM_TN), BF16)],
        compiler_params=_cparams("arbitrary", "arbitrary"),
        name="wo_caches",
    )(merged, w_o, x_all, *args)
    return outs[0], outs[1:]


def _norm2_kernel(x_ref, g_ref, xn_ref, xnt_ref):
    x = x_ref[...]
    ms = jnp.mean(x * x, axis=-1, keepdims=True)
    y = x * lax.rsqrt(ms + EPS) * g_ref[...]
    xn_ref[...] = y.astype(BF16)
    xnt_ref[...] = y.T.astype(BF16)


def _norm2(x1, g):
    return pl.pallas_call(
        _norm2_kernel,
        grid=(N_ROW_BLK,),
        in_specs=[
            pl.BlockSpec((ROW_BLK, D_MODEL), lambda i: (i, 0)),
            pl.BlockSpec((1, D_MODEL), lambda i: (0, 0)),
        ],
        out_specs=[
            pl.BlockSpec((ROW_BLK, D_MODEL), lambda i: (i, 0)),
            pl.BlockSpec((D_MODEL, ROW_BLK), lambda i: (0, i)),
        ],
        out_shape=[
            jax.ShapeDtypeStruct((N_TOK, D_MODEL), BF16),
            jax.ShapeDtypeStruct((D_MODEL, N_TOK), BF16),
        ],
        compiler_params=_cparams("arbitrary"),
        name="norm2",
    )(x1, g)


NO_RANK = float(PEER_KEYS - 1)


def _top_values(x, k, with_rank=False):
    rows = lax.broadcasted_iota(jnp.int32, x.shape, 0).astype(F32)
    rank = jnp.full(x.shape, NO_RANK, F32)
    out = []
    for j in range(k):
        m = jnp.max(x, axis=0, keepdims=True)
        first = jnp.min(jnp.where(x == m, rows, float(x.shape[0])), axis=0, keepdims=True)
        out.append(m)
        hit = rows == first
        x = jnp.where(hit, -jnp.inf, x)
        if with_rank:
            rank = jnp.where(hit, float(j), rank)
    return (out, rank) if with_rank else out


def _peer_prep_kernel(x_ref, wq_ref, k1_ref, k2_ref, e1_ref, n_ref, rank_ref, e2_ref, wq_bf):
    @pl.when(pl.program_id(1) == 0)
    def _():
        wq_bf[...] = wq_ref[...].astype(BF16)

    half = PEER_DKEY // 2
    q = jnp.dot(x_ref[...], wq_bf[...], preferred_element_type=F32).astype(BF16)
    nt = (((1,), (1,)), ((), ()))
    s1 = lax.dot_general(k1_ref[0].astype(BF16), q[:, :half], nt, preferred_element_type=F32)
    s2 = lax.dot_general(k2_ref[0].astype(BF16), q[:, half:], nt, preferred_element_type=F32)
    for c in range(PREP_TM // 128):
        cl = slice(c * 128, (c + 1) * 128)
        x1, x2 = s1[:, cl], s2[:, cl]
        a = _top_values(x1, PEER_TOPK)
        b, rank2 = _top_values(x2, PEER_TOPK, with_rank=True)
        cand = [a[i] + b[j] for i in range(PEER_TOPK) for j in range(PEER_TOPK) if (i + 1) * (j + 1) <= PEER_TOPK]
        cand += [jnp.full_like(a[0], -jnp.inf)] * (-len(cand) % 8)
        top = _top_values(jnp.concatenate(cand, axis=0), PEER_TOPK)
        tau = top[PEER_TOPK - 1]
        z = jnp.ones_like(top[0])
        for v in top[1:]:
            z = z + jnp.exp(v - top[0])
        n = jnp.zeros_like(x1)
        for k, bk in enumerate(b):
            n = jnp.where(x1 + bk >= tau, float(k + 1), n)
        e1_ref[0, :, cl] = jnp.exp(x1 - a[0]) / z
        n_ref[0, :, cl] = n
        rank_ref[0, :, cl] = rank2.astype(BF16)
        e2_ref[0, :, cl] = jnp.exp(x2 - b[0]).astype(BF16)


PREP_TM = 640


def _peer_prep(xn2, w_query, k1, k2):
    tm = PREP_TM
    half = PEER_DKEY // 2
    big = pl.BlockSpec((1, PEER_KEYS, tm), lambda h, i: (h, 0, i))
    keys = pl.BlockSpec((1, PEER_KEYS, half), lambda h, i: (h, 0, 0))
    shp = lambda dt: jax.ShapeDtypeStruct((PEER_HEADS, PEER_KEYS, N_TOK), dt)
    return pl.pallas_call(
        _peer_prep_kernel,
        grid=(PEER_HEADS, N_TOK // tm),
        in_specs=[
            pl.BlockSpec((tm, D_MODEL), lambda h, i: (i, 0)),
            pl.BlockSpec((D_MODEL, PEER_DKEY), lambda h, i: (0, h)),
            keys, keys,
        ],
        out_specs=[big, big, big, big],
        out_shape=[shp(F32), shp(F32), shp(BF16), shp(BF16)],
        scratch_shapes=[pltpu.VMEM((D_MODEL, PEER_DKEY), BF16)],
        compiler_params=_cparams("arbitrary", "arbitrary"),
        name="peer_prep",
    )(xn2, w_query, k1, k2)


I1_PER_TILE = PEER_TE // PEER_KEYS
I1_BLK = 8


def _peer_kernel(xt_ref, u_ref, vt_ref, e1_ref, n_ref, rank_ref, e2_ref, o_ref, w_scr):
    e = pl.program_id(1)

    @pl.when(e == 0)
    def _():
        o_ref[...] = jnp.zeros_like(o_ref)

    st = jnp.dot(u_ref[...], xt_ref[...], preferred_element_type=F32)
    off = (e % (I1_BLK // I1_PER_TILE)) * I1_PER_TILE
    zero = jnp.zeros((), BF16)
    for r in range(I1_PER_TILE):
        rows = slice(r * PEER_KEYS, (r + 1) * PEER_KEYS)
        coef = jnp.zeros((PEER_KEYS, PEER_TM), BF16)
        for h in range(PEER_HEADS):
            n = n_ref[h, pl.ds(off + r, 1), :].astype(BF16)
            g1 = e1_ref[h, pl.ds(off + r, 1), :].astype(BF16)
            coef = coef + g1 * jnp.where(rank_ref[h] < n, e2_ref[h], zero)
        w_scr[rows, :] = coef * _gelu(st[rows, :]).astype(BF16)
    o_ref[...] += jnp.dot(vt_ref[...], w_scr[...], preferred_element_type=F32)


def _peer(xn2t, u_bf, vt_bf, e1, n, rank2, e2):
    nt = pl.cdiv(N_TOK, PEER_TM)
    ne = PEER_EXPERTS // PEER_TE
    per = I1_BLK // I1_PER_TILE
    small = pl.BlockSpec((PEER_HEADS, I1_BLK, PEER_TM), lambda t, e: (0, e // per, t))
    big = pl.BlockSpec((PEER_HEADS, PEER_KEYS, PEER_TM), lambda t, e: (0, 0, t))
    return pl.pallas_call(
        _peer_kernel,
        grid=(nt, ne),
        in_specs=[
            pl.BlockSpec((D_MODEL, PEER_TM), lambda t, e: (0, t)),
            pl.BlockSpec((PEER_TE, D_MODEL), lambda t, e: (e, 0)),
            pl.BlockSpec((D_MODEL, PEER_TE), lambda t, e: (0, e)),
            small, small, big, big,
        ],
        out_specs=pl.BlockSpec((D_MODEL, PEER_TM), lambda t, e: (0, t)),
        out_shape=jax.ShapeDtypeStruct((D_MODEL, N_TOK), F32),
        scratch_shapes=[pltpu.VMEM((PEER_TE, PEER_TM), BF16)],
        compiler_params=_cparams("arbitrary", "arbitrary"),
        name="peer",
    )(xn2t, u_bf, vt_bf, e1, n, rank2, e2)


def _final_kernel(x_ref, pt_ref, yp_ref, ys_ref):
    i = pl.program_id(0)
    y = x_ref[...] + pt_ref[...].T

    @pl.when(i < N_PROMPT_BLK)
    def _():
        yp_ref[...] = y

    @pl.when(i == N_PROMPT_BLK)
    def _():
        ys_ref[...] = y


def _final(x1, peer_t):
    return pl.pallas_call(
        _final_kernel,
        grid=(N_ROW_BLK,),
        in_specs=[
            pl.BlockSpec((ROW_BLK, D_MODEL), lambda i: (i, 0)),
            pl.BlockSpec((D_MODEL, ROW_BLK), lambda i: (0, i)),
        ],
        out_specs=[
            pl.BlockSpec((ROW_BLK, D_MODEL), lambda i: (jnp.minimum(i, N_PROMPT_BLK - 1), 0)),
            pl.BlockSpec((ROW_BLK, D_MODEL), lambda i: (0, 0)),
        ],
        out_shape=[
            jax.ShapeDtypeStruct((N_PROMPT, D_MODEL), F32),
            jax.ShapeDtypeStruct((DEC_BATCH, D_MODEL), F32),
        ],
        compiler_params=_cparams("arbitrary"),
        name="final",
    )(x1, peer_t)


def _bucket(dist):
    max_exact = N_BUCKETS // 2
    dd = np.maximum(dist, max_exact).astype(np.float32)
    large = max_exact + (np.log(dd / np.float32(max_exact)) / np.float32(math.log(RP_MAX_DIST / max_exact))
                         * np.float32(N_BUCKETS - max_exact)).astype(np.int32)
    large = np.minimum(large, N_BUCKETS - 1)
    return np.where(dist < max_exact, dist, large)


def _group_bias(rel_bias, g, d):
    onehot = np.eye(N_BUCKETS, dtype=np.float32)[_bucket(d * np.arange(N_KEYS))]
    bias = jnp.dot(onehot, rel_bias[:, g * HEADS_PER_GROUP:(g + 1) * HEADS_PER_GROUP],
                   precision=lax.Precision.HIGHEST)
    pad = jnp.full((HEADS_PER_GROUP, 2 * ROW_BLK - N_KEYS), NEG, F32)
    ext = jnp.concatenate([bias[::-1].T, pad], axis=1)
    return bias, ext[:, None, :]


def kernel(x_prompt, x_sample, cache_k_w128, cache_v_w128, cache_k_w512, cache_v_w512, cache_k_w2048,
           cache_v_w2048, state_lru_h, state_conv, rel_bias, ln1_g, w_in, q_norm_g, k_norm_g, conv_w,
           conv_b, w_rg, b_rg, w_ig, b_ig, lru_L, w_pa, w_pb, w_o, ln2_g, w_query, sub_keys_1,
           sub_keys_2, peer_u, peer_v):
    caches = ((cache_k_w128, cache_v_w128), (cache_k_w512, cache_v_w512), (cache_k_w2048, cache_v_w2048))
    xp = x_prompt.reshape(N_PROMPT, D_MODEL)
    xs = x_sample.reshape(DEC_BATCH, D_MODEL)
    w = w_in[0]

    xn, x_all = _norm1(xp, xs, ln1_g)
    gspec = [pl.BlockSpec((1, HEAD_DIM), lambda j, i: (0, 0))]
    wide = dict(tm=IN_TM, tn=IN_TN)
    (q32,) = _proj(xn, w, Q_OFF, ATTN_WIDTH, _epi_q, (F32,), (q_norm_g,), gspec, **wide)
    k32, k_heads = _proj(xn, w, K_OFF, ATTN_WIDTH, _epi_k, (F32,), (k_norm_g,), gspec, head_major=True, **wide)
    v32, v_heads = _proj(xn, w, V_OFF, ATTN_WIDTH, _epi_v, (F32,), head_major=True, **wide)
    (xg,) = _proj(xn, w, XB_OFF, 2 * LRU_WIDTH, _epi_f32, (F32,), **wide)
    gates, u_bf, vt_bf = _gates_and_tables(xn, w, peer_u[0], peer_v[0])

    biases, exts = zip(*[_group_bias(rel_bias, g, d) for g, (_, d) in enumerate(ATTN_GROUPS)])
    attn_p = _attn_prompt(q32, k32, v32, jnp.concatenate(exts, axis=0))
    attn_s = _attn_sample(q32, k32, v32, [(ck[0], cv[0]) for ck, cv in caches], biases)
    attn = jnp.concatenate([attn_p, attn_s], axis=0)

    rec_p, h_p = _rec_prompt(xg, conv_w[0], conv_b, w_rg[0], b_rg, w_ig[0], b_ig, lru_L)
    rec_s, h_s = _rec_sample(xg, state_conv[0], state_lru_h[0], conv_w[0], conv_b, w_rg[0], b_rg,
                             w_ig[0], b_ig, lru_L)
    rec = jnp.concatenate([rec_p, rec_s], axis=0)

    merged = _mix(attn, rec, gates, w_pa[0], w_pb[0])
    x1, prompt_caches = _wo_and_caches(merged, w_o[0], x_all, k_heads, v_heads)

    xn2, xn2t = _norm2(x1, ln2_g)
    e1, nsel, rank2, e2 = _peer_prep(xn2, w_query[0], sub_keys_1[0], sub_keys_2[0])
    peer_t = _peer(xn2t, u_bf, vt_bf, e1, nsel, rank2, e2)
    y_p, y_s = _final(x1, peer_t)

    outs = [y_p.reshape(BATCH, SEQ, D_MODEL), y_s.reshape(DEC_BATCH, 1, D_MODEL)]
    for g, (window, d) in enumerate(ATTN_GROUPS):
        keep = min(window, SEQ)
        for a in prompt_caches[2 * g:2 * g + 2]:
            outs.append(a.reshape(1, BATCH, keep, HEADS_PER_GROUP, HEAD_DIM))
        for a in (k_heads, v_heads):
            outs.append(a[g, N_PROMPT:].reshape(1, DEC_BATCH, 1, HEADS_PER_GROUP, HEAD_DIM))
    tail = CONV_WIDTH - 1
    xb_tail = jnp.stack([xg[(b + 1) * SEQ - tail:(b + 1) * SEQ, :LRU_WIDTH] for b in range(BATCH)])
    xb_s = xg[N_PROMPT:, :LRU_WIDTH]
    outs += [
        h_p[None], h_s[None],
        xb_tail[None],
        jnp.concatenate([state_conv[0][:, 1:], xb_s[:, None, :]], axis=1)[None],
    ]
    return tuple(outs)
```

```python
import functools
import math

import jax
import jax.numpy as jnp
import numpy as np
from jax import lax
from jax.experimental import pallas as pl
from jax.experimental.pallas import tpu as pltpu

F32 = jnp.float32
BF16 = jnp.bfloat16

D_MODEL = 4096
BATCH = 4
SEQ = 2048
DEC_BATCH = 128
N_PROMPT = BATCH * SEQ
N_TOK = N_PROMPT + DEC_BATCH
HEAD_DIM = 128
HEADS_PER_GROUP = 8
GROUP_COLS = HEADS_PER_GROUP * HEAD_DIM
ATTN_GROUPS = ((128, 1), (512, 4), (2048, 16))
N_GROUPS = 3
ATTN_WIDTH = N_GROUPS * GROUP_COLS
N_KEYS = 129
N_BUCKETS = 32
RP_MAX_DIST = 2048
LRU_WIDTH = 2048
LRU_BLOCKS = 16
LRU_BLOCK = 128
CONV_WIDTH = 4
LRU_C = 8.0
PEER_HEADS = 8
PEER_KEYS = 128
PEER_EXPERTS = PEER_KEYS * PEER_KEYS
PEER_DKEY = 256
PEER_TOPK = 16
EPS = 1e-6

Q_OFF = 0
K_OFF = ATTN_WIDTH
V_OFF = 2 * ATTN_WIDTH
XB_OFF = 3 * ATTN_WIDTH
GA_OFF = XB_OFF + 2 * LRU_WIDTH

ROW_BLK = 128
N_PROMPT_BLK = N_PROMPT // ROW_BLK
N_ROW_BLK = N_TOK // ROW_BLK
MM_TM = 1040
MM_TN = 512
IN_TM = 416
IN_TN = 1024
PEER_TM = 512
PEER_TE = 512
NEG = -1e30
VMEM_LIMIT = 58 * 1024 * 1024


def _cparams(*sem):
    return pltpu.CompilerParams(dimension_semantics=sem, vmem_limit_bytes=VMEM_LIMIT)


def _gelu(x):
    return jax.nn.gelu(x)


def _sigmoid(x):
    return 1.0 / (1.0 + jnp.exp(-x))


def _norm1_kernel(xp_ref, xs_ref, g_ref, xn_ref, xall_ref):
    i = pl.program_id(0)

    def body(x):
        ms = jnp.mean(x * x, axis=-1, keepdims=True)
        xn_ref[...] = (x * lax.rsqrt(ms + EPS) * g_ref[...]).astype(BF16)
        xall_ref[...] = x

    @pl.when(i < N_PROMPT_BLK)
    def _():
        body(xp_ref[...])

    @pl.when(i == N_PROMPT_BLK)
    def _():
        body(xs_ref[...])


def _norm1(xp, xs, g):
    return pl.pallas_call(
        _norm1_kernel,
        grid=(N_ROW_BLK,),
        in_specs=[
            pl.BlockSpec((ROW_BLK, D_MODEL), lambda i: (jnp.minimum(i, N_PROMPT_BLK - 1), 0)),
            pl.BlockSpec((ROW_BLK, D_MODEL), lambda i: (0, 0)),
            pl.BlockSpec((1, D_MODEL), lambda i: (0, 0)),
        ],
        out_specs=[
            pl.BlockSpec((ROW_BLK, D_MODEL), lambda i: (i, 0)),
            pl.BlockSpec((ROW_BLK, D_MODEL), lambda i: (i, 0)),
        ],
        out_shape=[
            jax.ShapeDtypeStruct((N_TOK, D_MODEL), BF16),
            jax.ShapeDtypeStruct((N_TOK, D_MODEL), F32),
        ],
        compiler_params=_cparams("arbitrary"),
        name="norm1",
    )(xp, xs, g)


def _proj_kernel(*refs, n_extra, epilogue):
    x_ref, w_ref = refs[0], refs[1]
    extra = refs[2:2 + n_extra]
    outs = refs[2 + n_extra:-1]
    wbf_ref = refs[-1]

    @pl.when(pl.program_id(1) == 0)
    def _():
        wbf_ref[...] = w_ref[...].astype(BF16)

    acc = jnp.dot(x_ref[...], wbf_ref[...], preferred_element_type=F32)
    epilogue(acc, extra, outs)


def _proj(x, w, col0, ncols, epilogue, out_dtypes, extra=(), extra_specs=(), tm=MM_TM, tn=MM_TN,
          head_major=False):
    m, k = x.shape
    grid = (ncols // tn, m // tm)
    cb0 = col0 // tn
    kern = functools.partial(_proj_kernel, n_extra=len(extra), epilogue=epilogue)
    out_specs = [pl.BlockSpec((tm, tn), lambda j, i: (i, j)) for _ in out_dtypes]
    out_shape = [jax.ShapeDtypeStruct((m, ncols), dt) for dt in out_dtypes]
    if head_major:
        assert tn == GROUP_COLS
        out_specs.append(pl.BlockSpec((None, tm, HEADS_PER_GROUP, HEAD_DIM), lambda j, i: (j, i, 0, 0)))
        out_shape.append(jax.ShapeDtypeStruct((ncols // tn, m, HEADS_PER_GROUP, HEAD_DIM), F32))
    return pl.pallas_call(
        kern,
        grid=grid,
        in_specs=[
            pl.BlockSpec((tm, k), lambda j, i: (i, 0)),
            pl.BlockSpec((k, tn), lambda j, i: (0, cb0 + j)),
            *extra_specs,
        ],
        out_specs=out_specs,
        out_shape=out_shape,
        scratch_shapes=[pltpu.VMEM((k, tn), BF16)],
        compiler_params=_cparams("arbitrary", "arbitrary"),
        name="proj",
    )(x, w, *extra)


def _head_norm(a, g):
    ms = jnp.mean(a * a, axis=-1, keepdims=True)
    return a * lax.rsqrt(ms + EPS) * g


def _epi_q(acc, extra, outs):
    g = extra[0][...]
    for h in range(acc.shape[1] // HEAD_DIM):
        sl = slice(h * HEAD_DIM, (h + 1) * HEAD_DIM)
        outs[0][:, sl] = _head_norm(acc[:, sl], g) * (HEAD_DIM ** -0.5)


def _epi_k(acc, extra, outs):
    g = extra[0][...]
    for h in range(acc.shape[1] // HEAD_DIM):
        sl = slice(h * HEAD_DIM, (h + 1) * HEAD_DIM)
        y = _head_norm(acc[:, sl], g)
        outs[0][:, sl] = y
        outs[1][:, h, :] = y


def _epi_v(acc, extra, outs):
    outs[0][...] = acc
    for h in range(acc.shape[1] // HEAD_DIM):
        outs[1][:, h, :] = acc[:, h * HEAD_DIM:(h + 1) * HEAD_DIM]


def _epi_f32(acc, extra, outs):
    outs[0][...] = acc


GATE_TM = 832
GATE_TN = 512
TABLE_ROWS = 128


def _gates_tables_kernel(x_ref, w_ref, pu_ref, pv_ref, g_ref, ub_ref, vt_ref, wbf_ref):
    @pl.when(pl.program_id(1) == 0)
    def _():
        wbf_ref[...] = w_ref[...].astype(BF16)

    acc = jnp.dot(x_ref[...], wbf_ref[...], preferred_element_type=F32)
    g_ref[...] = _sigmoid(acc).astype(BF16)
    ub_ref[...] = pu_ref[...].astype(BF16)
    vt_ref[...] = pv_ref[...].T.astype(BF16)


def _gates_and_tables(xn, w, peer_u, peer_v):
    ncols = 2 * D_MODEL
    nm = N_TOK // GATE_TM
    grid = (ncols // GATE_TN, nm)
    n_slabs = PEER_EXPERTS // TABLE_ROWS
    assert grid[0] * grid[1] >= n_slabs
    cb0 = GA_OFF // GATE_TN
    slab = lambda j, i: jnp.minimum(j * nm + i, n_slabs - 1)
    return pl.pallas_call(
        _gates_tables_kernel,
        grid=grid,
        in_specs=[
            pl.BlockSpec((GATE_TM, D_MODEL), lambda j, i: (i, 0)),
            pl.BlockSpec((D_MODEL, GATE_TN), lambda j, i: (0, cb0 + j)),
            pl.BlockSpec((TABLE_ROWS, D_MODEL), lambda j, i: (slab(j, i), 0)),
            pl.BlockSpec((TABLE_ROWS, D_MODEL), lambda j, i: (slab(j, i), 0)),
        ],
        out_specs=[
            pl.BlockSpec((GATE_TM, GATE_TN), lambda j, i: (i, j)),
            pl.BlockSpec((TABLE_ROWS, D_MODEL), lambda j, i: (slab(j, i), 0)),
            pl.BlockSpec((D_MODEL, TABLE_ROWS), lambda j, i: (0, slab(j, i))),
        ],
        out_shape=[
            jax.ShapeDtypeStruct((N_TOK, ncols), BF16),
            jax.ShapeDtypeStruct((PEER_EXPERTS, D_MODEL), BF16),
            jax.ShapeDtypeStruct((D_MODEL, PEER_EXPERTS), BF16),
        ],
        scratch_shapes=[pltpu.VMEM((D_MODEL, GATE_TN), BF16)],
        compiler_params=_cparams("arbitrary", "arbitrary"),
        name="gates_tables",
    )(xn, w, peer_u, peer_v)


def _attn_p_kernel(*refs):
    q_refs, k_refs, v_refs, ext_refs = refs[0:3], refs[3:6], refs[6:9], refs[9:12]
    out_ref = refs[12]
    o_scr, l_scr = refs[13:16], refs[16:19]
    nt = (((1,), (1,)), ((), ()))

    for g, (_, d) in enumerate(ATTN_GROUPS):
        q_ref, k_ref, v_ref = q_refs[g], k_refs[g], v_refs[g]
        band = pltpu.roll(jnp.broadcast_to(ext_refs[g][0], (ROW_BLK, 2 * ROW_BLK)), 0, 1,
                          stride=1, stride_axis=0)
        band_cur = band[:, ROW_BLK:]
        nblk = SEQ // d // ROW_BLK

        def rows(r, i, d=d):
            start = r + (d * ROW_BLK) * i
            return pl.ds(start, ROW_BLK) if d == 1 else pl.ds(start, ROW_BLK, stride=d)

        def blocks(todo, g=g, q_ref=q_ref, k_ref=k_ref, v_ref=v_ref, band=band, band_cur=band_cur, rows=rows):
            staged = []
            for r, i, first in todo:
                cur = rows(r, i)
                q = q_ref[cur, :].astype(BF16)
                if first:
                    kk = k_ref[cur, :].astype(BF16)
                    vv = v_ref[cur, :].astype(BF16)
                    bias = band_cur
                else:
                    prev = rows(r, i - 1)
                    kk = jnp.concatenate([k_ref[prev, :], k_ref[cur, :]], axis=0).astype(BF16)
                    vv = jnp.concatenate([v_ref[prev, :], v_ref[cur, :]], axis=0).astype(BF16)
                    bias = band
                staged.append((cur, vv, lax.dot_general(q, kk, nt, preferred_element_type=F32) + bias))
            soft = []
            for cur, vv, lg in staged:
                m = jnp.max(lg, axis=-1, keepdims=True)
                p = jnp.exp(lg - m)
                s = jnp.sum(p, axis=-1, keepdims=True)
                soft.append((cur, vv, p.astype(BF16), s, m))
            for cur, vv, p, s, m in soft:
                o_scr[g][cur, :] = jnp.dot(p, vv, preferred_element_type=F32) / s
                l_scr[g][cur, :] = jnp.broadcast_to(m + jnp.log(s), (ROW_BLK, HEAD_DIM))

        if d == 1:
            blocks([(0, 0, True)])
            per_trip = 5

            def trip(t, carry, blocks=blocks):
                blocks([(0, 1 + per_trip * t + u, False) for u in range(per_trip)])
                return carry

            lax.fori_loop(0, (nblk - 1) // per_trip, trip, 0)
        elif nblk > 1:
            def trip(t, carry, blocks=blocks, nblk=nblk):
                blocks([(r, i, i == 0) for r in (2 * t, 2 * t + 1) for i in range(nblk)])
                return carry

            lax.fori_loop(0, d // 2, trip, 0)
        else:
            per_trip = 8

            def trip(t, carry, blocks=blocks):
                blocks([(per_trip * t + u, 0, True) for u in range(per_trip)])
                return carry

            lax.fori_loop(0, d // per_trip, trip, 0)

    l0, l1, l2 = l_scr[0][...], l_scr[1][...], l_scr[2][...]
    m = jnp.maximum(jnp.maximum(l0, l1), l2)
    e0, e1, e2 = jnp.exp(l0 - m), jnp.exp(l1 - m), jnp.exp(l2 - m)
    acc = e0 * o_scr[0][...] + e1 * o_scr[1][...] + e2 * o_scr[2][...]
    out_ref[...] = (acc / (e0 + e1 + e2)).astype(BF16)


def _attn_prompt(q32, k32, v32, ext):
    def col(g):
        return pl.BlockSpec((SEQ, HEAD_DIM), lambda b, h: (b, g * HEADS_PER_GROUP + h))

    def erow(g):
        return pl.BlockSpec((1, 1, 2 * ROW_BLK), lambda b, h: (g * HEADS_PER_GROUP + h, 0, 0))

    groups = range(N_GROUPS)
    return pl.pallas_call(
        _attn_p_kernel,
        grid=(BATCH, HEADS_PER_GROUP),
        in_specs=[col(g) for g in groups] * 3 + [erow(g) for g in groups],
        out_specs=pl.BlockSpec((SEQ, HEAD_DIM), lambda b, h: (b, h)),
        out_shape=jax.ShapeDtypeStruct((N_PROMPT, GROUP_COLS), BF16),
        scratch_shapes=[pltpu.VMEM((SEQ, HEAD_DIM), F32) for _ in range(2 * N_GROUPS)],
        compiler_params=_cparams("arbitrary", "arbitrary"),
        name="attn_p",
    )(q32, q32, q32, k32, k32, k32, v32, v32, v32, ext, ext, ext)


SAMPLE_BB = 4


def _attn_s_kernel(*refs):
    q_ref, kn_ref, vn_ref = refs[0:3]
    ck_refs, cv_refs = refs[3:6], refs[6:9]
    bc_ref, b0_ref, o_ref = refs[9], refs[10], refs[11]

    def one(bi, carry):
        outs, lses = [], []
        for g in range(N_GROUPS):
            hs = slice(g * HEADS_PER_GROUP, (g + 1) * HEADS_PER_GROUP)
            q = q_ref[bi, hs, :]
            lc = jnp.sum(ck_refs[g][bi] * q[None], axis=-1, keepdims=True) + bc_ref[g]
            ln = jnp.sum(kn_ref[bi, hs, :] * q, axis=-1, keepdims=True) + b0_ref[g]
            m = jnp.maximum(jnp.max(lc, axis=0), ln)
            p = jnp.exp(lc - m[None])
            pn = jnp.exp(ln - m)
            s = jnp.sum(p, axis=0) + pn
            outs.append((jnp.sum(p * cv_refs[g][bi], axis=0) + pn * vn_ref[bi, hs, :]) / s)
            lses.append(m + jnp.log(s))
        m = jnp.maximum(jnp.maximum(lses[0], lses[1]), lses[2])
        e = [jnp.exp(l - m) for l in lses]
        acc = e[0] * outs[0] + e[1] * outs[1] + e[2] * outs[2]
        o_ref[bi] = acc / (e[0] + e[1] + e[2])
        return carry

    lax.fori_loop(0, SAMPLE_BB, one, 0)


def _attn_sample(q32, k32, v32, caches, biases):
    heads3 = lambda a: a[N_PROMPT:].reshape(DEC_BATCH, N_GROUPS * HEADS_PER_GROUP, HEAD_DIM)
    new = pl.BlockSpec((SAMPLE_BB, N_GROUPS * HEADS_PER_GROUP, HEAD_DIM), lambda s: (s, 0, 0))
    bc = jnp.stack([b[::-1][:N_KEYS - 1] for b in biases])[..., None]
    b0 = jnp.stack([b[0] for b in biases])[..., None]
    cache_specs, cache_args = [], []
    for which in range(2):
        for g, (window, d) in enumerate(ATTN_GROUPS):
            c = caches[g][which]
            cache_args.append(c.reshape(DEC_BATCH, c.shape[1] // d, d, HEADS_PER_GROUP, HEAD_DIM))
            cache_specs.append(pl.BlockSpec((SAMPLE_BB, N_KEYS - 1, None, HEADS_PER_GROUP, HEAD_DIM),
                                            lambda s: (s, 0, 0, 0, 0)))
    o = pl.pallas_call(
        _attn_s_kernel,
        grid=(DEC_BATCH // SAMPLE_BB,),
        in_specs=[
            new, new, new, *cache_specs,
            pl.BlockSpec((N_GROUPS, N_KEYS - 1, HEADS_PER_GROUP, 1), lambda s: (0, 0, 0, 0)),
            pl.BlockSpec((N_GROUPS, HEADS_PER_GROUP, 1), lambda s: (0, 0, 0)),
        ],
        out_specs=pl.BlockSpec((SAMPLE_BB, HEADS_PER_GROUP, HEAD_DIM), lambda s: (s, 0, 0)),
        out_shape=jax.ShapeDtypeStruct((DEC_BATCH, HEADS_PER_GROUP, HEAD_DIM), F32),
        compiler_params=_cparams("arbitrary"),
        name="attn_s",
    )(heads3(q32), heads3(k32), heads3(v32), *cache_args, bc, b0)
    return o.reshape(DEC_BATCH, GROUP_COLS).astype(BF16)


SCAN_CHUNK = 64


def _softplus(x):
    return jnp.maximum(x, 0.0) + jnp.log1p(jnp.exp(-jnp.abs(x)))


def _gates(conv, wr_ref, br_ref, wi_ref, bi_ref, l_ref):
    cb = conv.astype(BF16)
    r = _sigmoid(jnp.dot(cb, wr_ref[0].astype(BF16), preferred_element_type=F32) + br_ref[...])
    ig = _sigmoid(jnp.dot(cb, wi_ref[0].astype(BF16), preferred_element_type=F32) + bi_ref[...])
    log_a = -LRU_C * r * _softplus(-l_ref[...])
    return log_a, ig


def _decay(log_a):
    a = jnp.exp(log_a)
    return a, jnp.sqrt(-jnp.tanh(log_a) * (a * a + 1.0))


def _rec_p_kernel(xb_ref, gb_ref, cw_ref, cb_ref, wr_ref, br_ref, wi_ref, bi_ref, l_ref,
                  y_ref, h_ref, a_scr, b_scr):
    t = SEQ
    x = xb_ref[...]
    row = lax.broadcasted_iota(jnp.int32, (t, LRU_BLOCK), 0)
    conv = cb_ref[...] + x * cw_ref[CONV_WIDTH - 1:CONV_WIDTH, :]
    for s in range(1, CONV_WIDTH):
        xs = jnp.where(row >= s, pltpu.roll(x, s, 0), 0.0)
        conv = conv + xs * cw_ref[CONV_WIDTH - 1 - s:CONV_WIDTH - s, :]
    log_a, ig = _gates(conv, wr_ref, br_ref, wi_ref, bi_ref, l_ref)
    a, mult = _decay(log_a)
    start = row == 0
    a_scr[...] = jnp.where(start, 0.0, a)
    b_scr[...] = jnp.where(start, 1.0, mult) * ig * conv

    crow = lax.broadcasted_iota(jnp.int32, (SCAN_CHUNK, LRU_BLOCK), 0)

    def chunk(c, h_prev):
        rows = pl.ds(pl.multiple_of(c * SCAN_CHUNK, SCAN_CHUNK), SCAN_CHUNK)
        a_c = a_scr[rows, :]
        b_c = b_scr[rows, :]
        s = 1
        while s < SCAN_CHUNK:
            keep = crow >= s
            a_sh = jnp.where(keep, pltpu.roll(a_c, s, 0), 1.0)
            b_sh = jnp.where(keep, pltpu.roll(b_c, s, 0), 0.0)
            b_c = a_c * b_sh + b_c
            a_c = a_c * a_sh
            s *= 2
        h = a_c * h_prev + b_c
        y_ref[rows, :] = (h * _gelu(gb_ref[rows, :])).astype(BF16)
        return h[SCAN_CHUNK - 1:SCAN_CHUNK, :]

    h_last = lax.fori_loop(0, t // SCAN_CHUNK, chunk, jnp.zeros((1, LRU_BLOCK), F32))
    h_ref[0] = h_last


def _rec_prompt(xg, conv_w, conv_b, w_rg, b_rg, w_ig, b_ig, lru_l):
    nb = LRU_BLOCKS
    vec = pl.BlockSpec((1, LRU_BLOCK), lambda b, n: (0, n))
    wblk = pl.BlockSpec((1, LRU_BLOCK, LRU_BLOCK), lambda b, n: (n, 0, 0))
    y, h = pl.pallas_call(
        _rec_p_kernel,
        grid=(BATCH, nb),
        in_specs=[
            pl.BlockSpec((SEQ, LRU_BLOCK), lambda b, n: (b, n)),
            pl.BlockSpec((SEQ, LRU_BLOCK), lambda b, n: (b, nb + n)),
            pl.BlockSpec((CONV_WIDTH, LRU_BLOCK), lambda b, n: (0, n)),
            vec, wblk, vec, wblk, vec, vec,
        ],
        out_specs=[
            pl.BlockSpec((SEQ, LRU_BLOCK), lambda b, n: (b, n)),
            pl.BlockSpec((1, 1, LRU_BLOCK), lambda b, n: (b, 0, n)),
        ],
        out_shape=[
            jax.ShapeDtypeStruct((N_PROMPT, LRU_WIDTH), BF16),
            jax.ShapeDtypeStruct((BATCH, 1, LRU_WIDTH), F32),
        ],
        scratch_shapes=[pltpu.VMEM((SEQ, LRU_BLOCK), F32), pltpu.VMEM((SEQ, LRU_BLOCK), F32)],
        compiler_params=_cparams("arbitrary", "arbitrary"),
        name="rec_p",
    )(xg, xg, conv_w, conv_b, w_rg, b_rg, w_ig, b_ig, lru_l)
    return y, h.reshape(BATCH, LRU_WIDTH)


def _rec_s_kernel(xb_ref, gb_ref, c0_ref, c1_ref, c2_ref, h0_ref, cw_ref, cb_ref,
                  wr_ref, br_ref, wi_ref, bi_ref, l_ref, y_ref, h_ref):
    conv = (cb_ref[...] + c0_ref[...] * cw_ref[0:1, :] + c1_ref[...] * cw_ref[1:2, :]
            + c2_ref[...] * cw_ref[2:3, :] + xb_ref[...] * cw_ref[3:4, :])
    log_a, ig = _gates(conv, wr_ref, br_ref, wi_ref, bi_ref, l_ref)
    a, mult = _decay(log_a)
    h = a * h0_ref[...] + mult * ig * conv
    h_ref[...] = h
    y_ref[...] = (h * _gelu(gb_ref[...])).astype(BF16)


def _rec_sample(xg, state_conv, h0, conv_w, conv_b, w_rg, b_rg, w_ig, b_ig, lru_l):
    nb = LRU_BLOCKS
    sc = state_conv.reshape(DEC_BATCH, (CONV_WIDTH - 1) * LRU_WIDTH)
    srow = N_PROMPT // DEC_BATCH
    vec = pl.BlockSpec((1, LRU_BLOCK), lambda n: (0, n))
    wblk = pl.BlockSpec((1, LRU_BLOCK, LRU_BLOCK), lambda n: (n, 0, 0))
    tile = lambda f: pl.BlockSpec((DEC_BATCH, LRU_BLOCK), f)
    return pl.pallas_call(
        _rec_s_kernel,
        grid=(nb,),
        in_specs=[
            tile(lambda n: (srow, n)),
            tile(lambda n: (srow, nb + n)),
            tile(lambda n: (0, n)),
            tile(lambda n: (0, nb + n)),
            tile(lambda n: (0, 2 * nb + n)),
            tile(lambda n: (0, n)),
            pl.BlockSpec((CONV_WIDTH, LRU_BLOCK), lambda n: (0, n)),
            vec, wblk, vec, wblk, vec, vec,
        ],
        out_specs=[tile(lambda n: (0, n)), tile(lambda n: (0, n))],
        out_shape=[
            jax.ShapeDtypeStruct((DEC_BATCH, LRU_WIDTH), BF16),
            jax.ShapeDtypeStruct((DEC_BATCH, LRU_WIDTH), F32),
        ],
        compiler_params=_cparams("arbitrary"),
        name="rec_s",
    )(xg, xg, sc, sc, sc, h0, conv_w, conv_b, w_rg, b_rg, w_ig, b_ig, lru_l)


def _mix_kernel(at_ref, rc_ref, ga_ref, gr_ref, wa_ref, wb_ref, o_ref, wa_bf, wb_bf):
    @pl.when(pl.program_id(1) == 0)
    def _():
        wa_bf[...] = wa_ref[...].astype(BF16)
        wb_bf[...] = wb_ref[...].astype(BF16)

    pa = jnp.dot(at_ref[...], wa_bf[...], preferred_element_type=F32)
    pb = jnp.dot(rc_ref[...], wb_bf[...], preferred_element_type=F32)
    o_ref[...] = (ga_ref[...].astype(F32) * pa + gr_ref[...].astype(F32) * pb).astype(BF16)


def _mix(attn, rec, gates, w_pa, w_pb):
    nj = D_MODEL // MM_TN
    return pl.pallas_call(
        _mix_kernel,
        grid=(nj, N_TOK // MM_TM),
        in_specs=[
            pl.BlockSpec((MM_TM, GROUP_COLS), lambda j, i: (i, 0)),
            pl.BlockSpec((MM_TM, LRU_WIDTH), lambda j, i: (i, 0)),
            pl.BlockSpec((MM_TM, MM_TN), lambda j, i: (i, j)),
            pl.BlockSpec((MM_TM, MM_TN), lambda j, i: (i, nj + j)),
            pl.BlockSpec((GROUP_COLS, MM_TN), lambda j, i: (0, j)),
            pl.BlockSpec((LRU_WIDTH, MM_TN), lambda j, i: (0, j)),
        ],
        out_specs=pl.BlockSpec((MM_TM, MM_TN), lambda j, i: (i, j)),
        out_shape=jax.ShapeDtypeStruct((N_TOK, D_MODEL), BF16),
        scratch_shapes=[pltpu.VMEM((GROUP_COLS, MM_TN), BF16), pltpu.VMEM((LRU_WIDTH, MM_TN), BF16)],
        compiler_params=_cparams("arbitrary", "arbitrary"),
        name="mix",
    )(attn, rec, gates, gates, w_pa, w_pb)


CACHE_ROWS = 128


def _wo_caches_kernel(*refs):
    m_ref, w_ref, x_ref = refs[0:3]
    src = refs[3:9]
    o_ref = refs[9]
    dst = refs[10:16]
    wbf_ref = refs[16]

    @pl.when(pl.program_id(1) == 0)
    def _():
        wbf_ref[...] = w_ref[...].astype(BF16)

    o_ref[...] = x_ref[...] + jnp.dot(m_ref[...], wbf_ref[...], preferred_element_type=F32)
    for s, d in zip(src, dst):
        d[...] = s[...]


def _wo_and_caches(merged, w_o, x_all, k_heads, v_heads):
    nm = N_TOK // MM_TM
    grid = (D_MODEL // MM_TN, nm)
    blk = (None, CACHE_ROWS, HEADS_PER_GROUP, HEAD_DIM)
    per_batch = SEQ // CACHE_ROWS
    in_specs, out_specs, out_shape, args = [], [], [], []
    for g, (window, _) in enumerate(ATTN_GROUPS):
        keep = min(window, SEQ) // CACHE_ROWS
        n_slabs = BATCH * keep
        assert grid[0] * grid[1] >= n_slabs

        def slab(j, i, n_slabs=n_slabs):
            return jnp.minimum(j * nm + i, n_slabs - 1)

        def src_idx(j, i, g=g, keep=keep, slab=slab):
            s = slab(j, i)
            return (g, (s // keep) * per_batch + (per_batch - keep) + s % keep, 0, 0)

        for a in (k_heads, v_heads):
            args.append(a)
            in_specs.append(pl.BlockSpec(blk, src_idx))
            out_specs.append(pl.BlockSpec(blk[1:], lambda j, i, slab=slab: (slab(j, i), 0, 0)))
            out_shape.append(jax.ShapeDtypeStruct((n_slabs * CACHE_ROWS, HEADS_PER_GROUP, HEAD_DIM), F32))
    tile = pl.BlockSpec((MM_TM, MM_TN), lambda j, i: (i, j))
    outs = pl.pallas_call(
        _wo_caches_kernel,
        grid=grid,
        in_specs=[
            pl.BlockSpec((MM_TM, D_MODEL), lambda j, i: (i, 0)),
            pl.BlockSpec((D_MODEL, MM_TN), lambda j, i: (0, j)),
            tile, *in_specs,
        ],
        out_specs=[tile, *out_specs],
        out_shape=[jax.ShapeDtypeStruct((N_TOK, D_MODEL), F32), *out_shape],
        scratch_shapes=[pltpu.VMEM((D_MODEL, MM_TN), BF16)],
        compiler_params=_cparams("arbitrary", "arbitrary"),
        name="wo_caches",
    )(merged, w_o, x_all, *args)
    return outs[0], outs[1:]


def _norm2_kernel(x_ref, g_ref, xn_ref, xnt_ref):
    x = x_ref[...]
    ms = jnp.mean(x * x, axis=-1, keepdims=True)
    y = x * lax.rsqrt(ms + EPS) * g_ref[...]
    xn_ref[...] = y.astype(BF16)
    xnt_ref[...] = y.T.astype(BF16)


def _norm2(x1, g):
    return pl.pallas_call(
        _norm2_kernel,
        grid=(N_ROW_BLK,),
        in_specs=[
            pl.BlockSpec((ROW_BLK, D_MODEL), lambda i: (i, 0)),
            pl.BlockSpec((1, D_MODEL), lambda i: (0, 0)),
        ],
        out_specs=[
            pl.BlockSpec((ROW_BLK, D_MODEL), lambda i: (i, 0)),
            pl.BlockSpec((D_MODEL, ROW_BLK), lambda i: (0, i)),
        ],
        out_shape=[
            jax.ShapeDtypeStruct((N_TOK, D_MODEL), BF16),
            jax.ShapeDtypeStruct((D_MODEL, N_TOK), BF16),
        ],
        compiler_params=_cparams("arbitrary"),
        name="norm2",
    )(x1, g)


NO_RANK = float(PEER_KEYS - 1)


def _top_values(x, k, with_rank=False):
    rows = lax.broadcasted_iota(jnp.int32, x.shape, 0).astype(F32)
    rank = jnp.full(x.shape, NO_RANK, F32)
    out = []
    for j in range(k):
        m = jnp.max(x, axis=0, keepdims=True)
        first = jnp.min(jnp.where(x == m, rows, float(x.shape[0])), axis=0, keepdims=True)
        out.append(m)
        hit = rows == first
        x = jnp.where(hit, -jnp.inf, x)
        if with_rank:
            rank = jnp.where(hit, float(j), rank)
    return (out, rank) if with_rank else out


def _peer_prep_kernel(x_ref, wq_ref, k1_ref, k2_ref, e1_ref, n_ref, rank_ref, e2_ref, wq_bf, s_scr):
    step = pl.program_id(0)
    slot = step % 2

    @pl.when(step == 0)
    def _():
        s_scr[1] = jnp.zeros(s_scr.shape[1:], F32)

    @pl.when(step % PREP_TILES == 0)
    def _():
        wq_bf[...] = wq_ref[...].astype(BF16)

    half = PEER_DKEY // 2
    q = jnp.dot(x_ref[...], wq_bf[...], preferred_element_type=F32).astype(BF16)
    nt = (((1,), (1,)), ((), ()))
    s_scr[slot, 0] = lax.dot_general(k1_ref[0].astype(BF16), q[:, :half], nt, preferred_element_type=F32)
    s_scr[slot, 1] = lax.dot_general(k2_ref[0].astype(BF16), q[:, half:], nt, preferred_element_type=F32)
    s1 = s_scr[1 - slot, 0]
    s2 = s_scr[1 - slot, 1]
    for c in range(PREP_TM // 128):
        cl = slice(c * 128, (c + 1) * 128)
        x1, x2 = s1[:, cl], s2[:, cl]
        a = _top_values(x1, PEER_TOPK)
        b, rank2 = _top_values(x2, PEER_TOPK, with_rank=True)
        cand = [a[i] + b[j] for i in range(PEER_TOPK) for j in range(PEER_TOPK) if (i + 1) * (j + 1) <= PEER_TOPK]
        cand += [jnp.full_like(a[0], -jnp.inf)] * (-len(cand) % 8)
        top = _top_values(jnp.concatenate(cand, axis=0), PEER_TOPK)
        tau = top[PEER_TOPK - 1]
        z = jnp.ones_like(top[0])
        for v in top[1:]:
            z = z + jnp.exp(v - top[0])
        n = jnp.zeros_like(x1)
        for k, bk in enumerate(b):
            n = jnp.where(x1 + bk >= tau, float(k + 1), n)
        e1_ref[0, :, cl] = jnp.exp(x1 - a[0]) / z
        n_ref[0, :, cl] = n
        rank_ref[0, :, cl] = rank2.astype(BF16)
        e2_ref[0, :, cl] = jnp.exp(x2 - b[0]).astype(BF16)


PREP_TM = 640
PREP_TILES = N_TOK // PREP_TM


def _peer_prep(xn2, w_query, k1, k2):
    tm = PREP_TM
    half = PEER_DKEY // 2
    n_jobs = PEER_HEADS * PREP_TILES
    cur = lambda s: jnp.minimum(s, n_jobs - 1)
    prev = lambda s: jnp.maximum(s - 1, 0)
    big = pl.BlockSpec((1, PEER_KEYS, tm), lambda s: (prev(s) // PREP_TILES, 0, prev(s) % PREP_TILES))
    keys = pl.BlockSpec((1, PEER_KEYS, half), lambda s: (cur(s) // PREP_TILES, 0, 0))
    shp = lambda dt: jax.ShapeDtypeStruct((PEER_HEADS, PEER_KEYS, N_TOK), dt)
    return pl.pallas_call(
        _peer_prep_kernel,
        grid=(n_jobs + 1,),
        in_specs=[
            pl.BlockSpec((tm, D_MODEL), lambda s: (cur(s) % PREP_TILES, 0)),
            pl.BlockSpec((D_MODEL, PEER_DKEY), lambda s: (0, cur(s) // PREP_TILES)),
            keys, keys,
        ],
        out_specs=[big, big, big, big],
        out_shape=[shp(F32), shp(F32), shp(BF16), shp(BF16)],
        scratch_shapes=[pltpu.VMEM((D_MODEL, PEER_DKEY), BF16), pltpu.VMEM((2, 2, PEER_KEYS, PREP_TM), F32)],
        compiler_params=_cparams("arbitrary"),
        name="peer_prep",
    )(xn2, w_query, k1, k2)


I1_PER_TILE = PEER_TE // PEER_KEYS
I1_BLK = 8


def _peer_kernel(xt_ref, u_ref, vt_ref, e1_ref, n_ref, rank_ref, e2_ref, o_ref, w_scr):
    e = pl.program_id(1)

    @pl.when(e == 0)
    def _():
        o_ref[...] = jnp.zeros_like(o_ref)

    st = jnp.dot(u_ref[...], xt_ref[...], preferred_element_type=F32)
    off = (e % (I1_BLK // I1_PER_TILE)) * I1_PER_TILE
    zero = jnp.zeros((), BF16)
    for r in range(I1_PER_TILE):
        rows = slice(r * PEER_KEYS, (r + 1) * PEER_KEYS)
        coef = jnp.zeros((PEER_KEYS, PEER_TM), BF16)
        for h in range(PEER_HEADS):
            n = n_ref[h, pl.ds(off + r, 1), :].astype(BF16)
            g1 = e1_ref[h, pl.ds(off + r, 1), :].astype(BF16)
            coef = coef + g1 * jnp.where(rank_ref[h] < n, e2_ref[h], zero)
        w_scr[rows, :] = coef * _gelu(st[rows, :]).astype(BF16)
    o_ref[...] += jnp.dot(vt_ref[...], w_scr[...], preferred_element_type=F32)


def _peer(xn2t, u_bf, vt_bf, e1, n, rank2, e2):
    nt = pl.cdiv(N_TOK, PEER_TM)
    ne = PEER_EXPERTS // PEER_TE
    per = I1_BLK // I1_PER_TILE
    small = pl.BlockSpec((PEER_HEADS, I1_BLK, PEER_TM), lambda t, e: (0, e // per, t))
    big = pl.BlockSpec((PEER_HEADS, PEER_KEYS, PEER_TM), lambda t, e: (0, 0, t))
    return pl.pallas_call(
        _peer_kernel,
        grid=(nt, ne),
        in_specs=[
            pl.BlockSpec((D_MODEL, PEER_TM), lambda t, e: (0, t)),
            pl.BlockSpec((PEER_TE, D_MODEL), lambda t, e: (e, 0)),
            pl.BlockSpec((D_MODEL, PEER_TE), lambda t, e: (0, e)),
            small, small, big, big,
        ],
        out_specs=pl.BlockSpec((D_MODEL, PEER_TM), lambda t, e: (0, t)),
        out_shape=jax.ShapeDtypeStruct((D_MODEL, N_TOK), F32),
        scratch_shapes=[pltpu.VMEM((PEER_TE, PEER_TM), BF16)],
        compiler_params=_cparams("arbitrary", "arbitrary"),
        name="peer",
    )(xn2t, u_bf, vt_bf, e1, n, rank2, e2)


def _final_kernel(x_ref, pt_ref, yp_ref, ys_ref):
    i = pl.program_id(0)
    y = x_ref[...] + pt_ref[...].T

    @pl.when(i < N_PROMPT_BLK)
    def _():
        yp_ref[...] = y

    @pl.when(i == N_PROMPT_BLK)
    def _():
        ys_ref[...] = y


def _final(x1, peer_t):
    return pl.pallas_call(
        _final_kernel,
        grid=(N_ROW_BLK,),
        in_specs=[
            pl.BlockSpec((ROW_BLK, D_MODEL), lambda i: (i, 0)),
            pl.BlockSpec((D_MODEL, ROW_BLK), lambda i: (0, i)),
        ],
        out_specs=[
            pl.BlockSpec((ROW_BLK, D_MODEL), lambda i: (jnp.minimum(i, N_PROMPT_BLK - 1), 0)),
            pl.BlockSpec((ROW_BLK, D_MODEL), lambda i: (0, 0)),
        ],
        out_shape=[
            jax.ShapeDtypeStruct((N_PROMPT, D_MODEL), F32),
            jax.ShapeDtypeStruct((DEC_BATCH, D_MODEL), F32),
        ],
        compiler_params=_cparams("arbitrary"),
        name="final",
    )(x1, peer_t)


def _bucket(dist):
    max_exact = N_BUCKETS // 2
    dd = np.maximum(dist, max_exact).astype(np.float32)
    large = max_exact + (np.log(dd / np.float32(max_exact)) / np.float32(math.log(RP_MAX_DIST / max_exact))
                         * np.float32(N_BUCKETS - max_exact)).astype(np.int32)
    large = np.minimum(large, N_BUCKETS - 1)
    return np.where(dist < max_exact, dist, large)


def _group_bias(rel_bias, g, d):
    onehot = np.eye(N_BUCKETS, dtype=np.float32)[_bucket(d * np.arange(N_KEYS))]
    bias = jnp.dot(onehot, rel_bias[:, g * HEADS_PER_GROUP:(g + 1) * HEADS_PER_GROUP],
                   precision=lax.Precision.HIGHEST)
    pad = jnp.full((HEADS_PER_GROUP, 2 * ROW_BLK - N_KEYS), NEG, F32)
    ext = jnp.concatenate([bias[::-1].T, pad], axis=1)
    return bias, ext[:, None, :]


def kernel(x_prompt, x_sample, cache_k_w128, cache_v_w128, cache_k_w512, cache_v_w512, cache_k_w2048,
           cache_v_w2048, state_lru_h, state_conv, rel_bias, ln1_g, w_in, q_norm_g, k_norm_g, conv_w,
           conv_b, w_rg, b_rg, w_ig, b_ig, lru_L, w_pa, w_pb, w_o, ln2_g, w_query, sub_keys_1,
           sub_keys_2, peer_u, peer_v):
    caches = ((cache_k_w128, cache_v_w128), (cache_k_w512, cache_v_w512), (cache_k_w2048, cache_v_w2048))
    xp = x_prompt.reshape(N_PROMPT, D_MODEL)
    xs = x_sample.reshape(DEC_BATCH, D_MODEL)
    w = w_in[0]

    xn, x_all = _norm1(xp, xs, ln1_g)
    gspec = [pl.BlockSpec((1, HEAD_DIM), lambda j, i: (0, 0))]
    wide = dict(tm=IN_TM, tn=IN_TN)
    (q32,) = _proj(xn, w, Q_OFF, ATTN_WIDTH, _epi_q, (F32,), (q_norm_g,), gspec, **wide)
    k32, k_heads = _proj(xn, w, K_OFF, ATTN_WIDTH, _epi_k, (F32,), (k_norm_g,), gspec, head_major=True, **wide)
    v32, v_heads = _proj(xn, w, V_OFF, ATTN_WIDTH, _epi_v, (F32,), head_major=True, **wide)
    (xg,) = _proj(xn, w, XB_OFF, 2 * LRU_WIDTH, _epi_f32, (F32,), **wide)
    gates, u_bf, vt_bf = _gates_and_tables(xn, w, peer_u[0], peer_v[0])

    biases, exts = zip(*[_group_bias(rel_bias, g, d) for g, (_, d) in enumerate(ATTN_GROUPS)])
    attn_p = _attn_prompt(q32, k32, v32, jnp.concatenate(exts, axis=0))
    attn_s = _attn_sample(q32, k32, v32, [(ck[0], cv[0]) for ck, cv in caches], biases)
    attn = jnp.concatenate([attn_p, attn_s], axis=0)

    rec_p, h_p = _rec_prompt(xg, conv_w[0], conv_b, w_rg[0], b_rg, w_ig[0], b_ig, lru_L)
    rec_s, h_s = _rec_sample(xg, state_conv[0], state_lru_h[0], conv_w[0], conv_b, w_rg[0], b_rg,
                             w_ig[0], b_ig, lru_L)
    rec = jnp.concatenate([rec_p, rec_s], axis=0)

    merged = _mix(attn, rec, gates, w_pa[0], w_pb[0])
    x1, prompt_caches = _wo_and_caches(merged, w_o[0], x_all, k_heads, v_heads)

    xn2, xn2t = _norm2(x1, ln2_g)
    e1, nsel, rank2, e2 = _peer_prep(xn2, w_query[0], sub_keys_1[0], sub_keys_2[0])
    peer_t = _peer(xn2t, u_bf, vt_bf, e1, nsel, rank2, e2)
    y_p, y_s = _final(x1, peer_t)

    outs = [y_p.reshape(BATCH, SEQ, D_MODEL), y_s.reshape(DEC_BATCH, 1, D_MODEL)]
    for g, (window, d) in enumerate(ATTN_GROUPS):
        keep = min(window, SEQ)
        for a in prompt_caches[2 * g:2 * g + 2]:
            outs.append(a.reshape(1, BATCH, keep, HEADS_PER_GROUP, HEAD_DIM))
        for a in (k_heads, v_heads):
            outs.append(a[g, N_PROMPT:].reshape(1, DEC_BATCH, 1, HEADS_PER_GROUP, HEAD_DIM))
    tail = CONV_WIDTH - 1
    xb_tail = jnp.stack([xg[(b + 1) * SEQ - tail:(b + 1) * SEQ, :LRU_WIDTH] for b in range(BATCH)])
    xb_s = xg[N_PROMPT:, :LRU_WIDTH]
    outs += [
        h_p[None], h_s[None],
        xb_tail[None],
        jnp.concatenate([state_conv[0][:, 1:], xb_s[:, None, :]], axis=1)[None],
    ]
    return tuple(outs)
```

```python
import functools
import math

import jax
import jax.numpy as jnp
import numpy as np
from jax import lax
from jax.experimental import pallas as pl
from jax.experimental.pallas import tpu as pltpu

F32 = jnp.float32
BF16 = jnp.bfloat16

D_MODEL = 4096
BATCH = 4
SEQ = 2048
DEC_BATCH = 128
N_PROMPT = BATCH * SEQ
N_TOK = N_PROMPT + DEC_BATCH
HEAD_DIM = 128
HEADS_PER_GROUP = 8
GROUP_COLS = HEADS_PER_GROUP * HEAD_DIM
ATTN_GROUPS = ((128, 1), (512, 4), (2048, 16))
N_GROUPS = 3
ATTN_WIDTH = N_GROUPS * GROUP_COLS
N_KEYS = 129
N_BUCKETS = 32
RP_MAX_DIST = 2048
LRU_WIDTH = 2048
LRU_BLOCKS = 16
LRU_BLOCK = 128
CONV_WIDTH = 4
LRU_C = 8.0
PEER_HEADS = 8
PEER_KEYS = 128
PEER_EXPERTS = PEER_KEYS * PEER_KEYS
PEER_DKEY = 256
PEER_TOPK = 16
EPS = 1e-6

Q_OFF = 0
K_OFF = ATTN_WIDTH
V_OFF = 2 * ATTN_WIDTH
XB_OFF = 3 * ATTN_WIDTH
GA_OFF = XB_OFF + 2 * LRU_WIDTH

ROW_BLK = 128
N_PROMPT_BLK = N_PROMPT // ROW_BLK
N_ROW_BLK = N_TOK // ROW_BLK
MM_TM = 1040
MM_TN = 512
IN_TM = 416
IN_TN = 1024
PEER_TM = 512
PEER_TE = 512
NEG = -1e30
VMEM_LIMIT = 58 * 1024 * 1024


def _cparams(*sem):
    return pltpu.CompilerParams(dimension_semantics=sem, vmem_limit_bytes=VMEM_LIMIT)


def _gelu(x):
    return jax.nn.gelu(x)


def _sigmoid(x):
    return 1.0 / (1.0 + jnp.exp(-x))


def _norm1_kernel(xp_ref, xs_ref, g_ref, xn_ref, xall_ref):
    i = pl.program_id(0)

    def body(x):
        ms = jnp.mean(x * x, axis=-1, keepdims=True)
        xn_ref[...] = (x * lax.rsqrt(ms + EPS) * g_ref[...]).astype(BF16)
        xall_ref[...] = x

    @pl.when(i < N_PROMPT_BLK)
    def _():
        body(xp_ref[...])

    @pl.when(i == N_PROMPT_BLK)
    def _():
        body(xs_ref[...])


def _norm1(xp, xs, g):
    return pl.pallas_call(
        _norm1_kernel,
        grid=(N_ROW_BLK,),
        in_specs=[
            pl.BlockSpec((ROW_BLK, D_MODEL), lambda i: (jnp.minimum(i, N_PROMPT_BLK - 1), 0)),
            pl.BlockSpec((ROW_BLK, D_MODEL), lambda i: (0, 0)),
            pl.BlockSpec((1, D_MODEL), lambda i: (0, 0)),
        ],
        out_specs=[
            pl.BlockSpec((ROW_BLK, D_MODEL), lambda i: (i, 0)),
            pl.BlockSpec((ROW_BLK, D_MODEL), lambda i: (i, 0)),
        ],
        out_shape=[
            jax.ShapeDtypeStruct((N_TOK, D_MODEL), BF16),
            jax.ShapeDtypeStruct((N_TOK, D_MODEL), F32),
        ],
        compiler_params=_cparams("arbitrary"),
        name="norm1",
    )(xp, xs, g)


def _proj_kernel(*refs, n_extra, epilogue):
    x_ref, w_ref = refs[0], refs[1]
    extra = refs[2:2 + n_extra]
    outs = refs[2 + n_extra:-1]
    wbf_ref = refs[-1]

    @pl.when(pl.program_id(1) == 0)
    def _():
        wbf_ref[...] = w_ref[...].astype(BF16)

    acc = jnp.dot(x_ref[...], wbf_ref[...], preferred_element_type=F32)
    epilogue(acc, extra, outs)


def _proj(x, w, col0, ncols, epilogue, out_dtypes, extra=(), extra_specs=(), tm=MM_TM, tn=MM_TN,
          head_major=False):
    m, k = x.shape
    grid = (ncols // tn, m // tm)
    cb0 = col0 // tn
    kern = functools.partial(_proj_kernel, n_extra=len(extra), epilogue=epilogue)
    out_specs = [pl.BlockSpec((tm, tn), lambda j, i: (i, j)) for _ in out_dtypes]
    out_shape = [jax.ShapeDtypeStruct((m, ncols), dt) for dt in out_dtypes]
    if head_major:
        assert tn == GROUP_COLS
        out_specs.append(pl.BlockSpec((None, tm, HEADS_PER_GROUP, HEAD_DIM), lambda j, i: (j, i, 0, 0)))
        out_shape.append(jax.ShapeDtypeStruct((ncols // tn, m, HEADS_PER_GROUP, HEAD_DIM), F32))
    return pl.pallas_call(
        kern,
        grid=grid,
        in_specs=[
            pl.BlockSpec((tm, k), lambda j, i: (i, 0)),
            pl.BlockSpec((k, tn), lambda j, i: (0, cb0 + j)),
            *extra_specs,
        ],
        out_specs=out_specs,
        out_shape=out_shape,
        scratch_shapes=[pltpu.VMEM((k, tn), BF16)],
        compiler_params=_cparams("arbitrary", "arbitrary"),
        name="proj",
    )(x, w, *extra)


def _head_norm(a, g):
    ms = jnp.mean(a * a, axis=-1, keepdims=True)
    return a * lax.rsqrt(ms + EPS) * g


def _epi_q(acc, extra, outs):
    g = extra[0][...]
    for h in range(acc.shape[1] // HEAD_DIM):
        sl = slice(h * HEAD_DIM, (h + 1) * HEAD_DIM)
        outs[0][:, sl] = _head_norm(acc[:, sl], g) * (HEAD_DIM ** -0.5)


def _epi_k(acc, extra, outs):
    g = extra[0][...]
    for h in range(acc.shape[1] // HEAD_DIM):
        sl = slice(h * HEAD_DIM, (h + 1) * HEAD_DIM)
        y = _head_norm(acc[:, sl], g)
        outs[0][:, sl] = y
        outs[1][:, h, :] = y


def _epi_v(acc, extra, outs):
    outs[0][...] = acc
    for h in range(acc.shape[1] // HEAD_DIM):
        outs[1][:, h, :] = acc[:, h * HEAD_DIM:(h + 1) * HEAD_DIM]


def _epi_f32(acc, extra, outs):
    outs[0][...] = acc


GATE_TM = 832
GATE_TN = 512
TABLE_ROWS = 128


def _gates_tables_kernel(x_ref, w_ref, pu_ref, pv_ref, g_ref, ub_ref, vt_ref, wbf_ref):
    @pl.when(pl.program_id(1) == 0)
    def _():
        wbf_ref[...] = w_ref[...].astype(BF16)

    acc = jnp.dot(x_ref[...], wbf_ref[...], preferred_element_type=F32)
    g_ref[...] = _sigmoid(acc).astype(BF16)
    ub_ref[...] = pu_ref[...].astype(BF16)
    vt_ref[...] = pv_ref[...].T.astype(BF16)


def _gates_and_tables(xn, w, peer_u, peer_v):
    ncols = 2 * D_MODEL
    nm = N_TOK // GATE_TM
    grid = (ncols // GATE_TN, nm)
    n_slabs = PEER_EXPERTS // TABLE_ROWS
    assert grid[0] * grid[1] >= n_slabs
    cb0 = GA_OFF // GATE_TN
    slab = lambda j, i: jnp.minimum(j * nm + i, n_slabs - 1)
    return pl.pallas_call(
        _gates_tables_kernel,
        grid=grid,
        in_specs=[
            pl.BlockSpec((GATE_TM, D_MODEL), lambda j, i: (i, 0)),
            pl.BlockSpec((D_MODEL, GATE_TN), lambda j, i: (0, cb0 + j)),
            pl.BlockSpec((TABLE_ROWS, D_MODEL), lambda j, i: (slab(j, i), 0)),
            pl.BlockSpec((TABLE_ROWS, D_MODEL), lambda j, i: (slab(j, i), 0)),
        ],
        out_specs=[
            pl.BlockSpec((GATE_TM, GATE_TN), lambda j, i: (i, j)),
            pl.BlockSpec((TABLE_ROWS, D_MODEL), lambda j, i: (slab(j, i), 0)),
            pl.BlockSpec((D_MODEL, TABLE_ROWS), lambda j, i: (0, slab(j, i))),
        ],
        out_shape=[
            jax.ShapeDtypeStruct((N_TOK, ncols), BF16),
            jax.ShapeDtypeStruct((PEER_EXPERTS, D_MODEL), BF16),
            jax.ShapeDtypeStruct((D_MODEL, PEER_EXPERTS), BF16),
        ],
        scratch_shapes=[pltpu.VMEM((D_MODEL, GATE_TN), BF16)],
        compiler_params=_cparams("arbitrary", "arbitrary"),
        name="gates_tables",
    )(xn, w, peer_u, peer_v)


def _attn_p_kernel(*refs):
    q_refs, k_refs, v_refs, ext_refs = refs[0:3], refs[3:6], refs[6:9], refs[9:12]
    out_ref = refs[12]
    o_scr, l_scr = refs[13:16], refs[16:19]
    nt = (((1,), (1,)), ((), ()))

    for g, (_, d) in enumerate(ATTN_GROUPS):
        q_ref, k_ref, v_ref = q_refs[g], k_refs[g], v_refs[g]
        band = pltpu.roll(jnp.broadcast_to(ext_refs[g][0], (ROW_BLK, 2 * ROW_BLK)), 0, 1,
                          stride=1, stride_axis=0)
        band_cur = band[:, ROW_BLK:]
        nblk = SEQ // d // ROW_BLK

        def rows(r, i, d=d):
            start = r + (d * ROW_BLK) * i
            return pl.ds(start, ROW_BLK) if d == 1 else pl.ds(start, ROW_BLK, stride=d)

        def blocks(todo, g=g, q_ref=q_ref, k_ref=k_ref, v_ref=v_ref, band=band, band_cur=band_cur, rows=rows):
            staged = []
            for r, i, first in todo:
                cur = rows(r, i)
                q = q_ref[cur, :].astype(BF16)
                if first:
                    kk = k_ref[cur, :].astype(BF16)
                    vv = v_ref[cur, :].astype(BF16)
                    bias = band_cur
                else:
                    prev = rows(r, i - 1)
                    kk = jnp.concatenate([k_ref[prev, :], k_ref[cur, :]], axis=0).astype(BF16)
                    vv = jnp.concatenate([v_ref[prev, :], v_ref[cur, :]], axis=0).astype(BF16)
                    bias = band
                staged.append((cur, vv, lax.dot_general(q, kk, nt, preferred_element_type=F32) + bias))
            soft = []
            for cur, vv, lg in staged:
                m = jnp.max(lg, axis=-1, keepdims=True)
                p = jnp.exp(lg - m)
                s = jnp.sum(p, axis=-1, keepdims=True)
                soft.append((cur, vv, p.astype(BF16), s, m))
            for cur, vv, p, s, m in soft:
                o_scr[g][cur, :] = jnp.dot(p, vv, preferred_element_type=F32) / s
                l_scr[g][cur, :] = jnp.broadcast_to(m + jnp.log(s), (ROW_BLK, HEAD_DIM))

        if d == 1:
            blocks([(0, 0, True)])
            per_trip = 5

            def trip(t, carry, blocks=blocks):
                blocks([(0, 1 + per_trip * t + u, False) for u in range(per_trip)])
                return carry

            lax.fori_loop(0, (nblk - 1) // per_trip, trip, 0)
        elif nblk > 1:
            def trip(t, carry, blocks=blocks, nblk=nblk):
                blocks([(r, i, i == 0) for r in (2 * t, 2 * t + 1) for i in range(nblk)])
                return carry

            lax.fori_loop(0, d // 2, trip, 0)
        else:
            per_trip = 8

            def trip(t, carry, blocks=blocks):
                blocks([(per_trip * t + u, 0, True) for u in range(per_trip)])
                return carry

            lax.fori_loop(0, d // per_trip, trip, 0)

    l0, l1, l2 = l_scr[0][...], l_scr[1][...], l_scr[2][...]
    m = jnp.maximum(jnp.maximum(l0, l1), l2)
    e0, e1, e2 = jnp.exp(l0 - m), jnp.exp(l1 - m), jnp.exp(l2 - m)
    acc = e0 * o_scr[0][...] + e1 * o_scr[1][...] + e2 * o_scr[2][...]
    out_ref[...] = (acc / (e0 + e1 + e2)).astype(BF16)


def _attn_prompt(q32, k32, v32, ext):
    def col(g):
        return pl.BlockSpec((SEQ, HEAD_DIM), lambda b, h: (b, g * HEADS_PER_GROUP + h))

    def erow(g):
        return pl.BlockSpec((1, 1, 2 * ROW_BLK), lambda b, h: (g * HEADS_PER_GROUP + h, 0, 0))

    groups = range(N_GROUPS)
    return pl.pallas_call(
        _attn_p_kernel,
        grid=(BATCH, HEADS_PER_GROUP),
        in_specs=[col(g) for g in groups] * 3 + [erow(g) for g in groups],
        out_specs=pl.BlockSpec((SEQ, HEAD_DIM), lambda b, h: (b, h)),
        out_shape=jax.ShapeDtypeStruct((N_PROMPT, GROUP_COLS), BF16),
        scratch_shapes=[pltpu.VMEM((SEQ, HEAD_DIM), F32) for _ in range(2 * N_GROUPS)],
        compiler_params=_cparams("arbitrary", "arbitrary"),
        name="attn_p",
    )(q32, q32, q32, k32, k32, k32, v32, v32, v32, ext, ext, ext)


SAMPLE_BB = 4


def _attn_s_kernel(*refs):
    q_ref, kn_ref, vn_ref = refs[0:3]
    ck_refs, cv_refs = refs[3:6], refs[6:9]
    bc_ref, b0_ref, o_ref = refs[9], refs[10], refs[11]

    def one(bi, carry):
        outs, lses = [], []
        for g in range(N_GROUPS):
            hs = slice(g * HEADS_PER_GROUP, (g + 1) * HEADS_PER_GROUP)
            q = q_ref[bi, hs, :]
            lc = jnp.sum(ck_refs[g][bi] * q[None], axis=-1, keepdims=True) + bc_ref[g]
            ln = jnp.sum(kn_ref[bi, hs, :] * q, axis=-1, keepdims=True) + b0_ref[g]
            m = jnp.maximum(jnp.max(lc, axis=0), ln)
            p = jnp.exp(lc - m[None])
            pn = jnp.exp(ln - m)
            s = jnp.sum(p, axis=0) + pn
            outs.append((jnp.sum(p * cv_refs[g][bi], axis=0) + pn * vn_ref[bi, hs, :]) / s)
            lses.append(m + jnp.log(s))
        m = jnp.maximum(jnp.maximum(lses[0], lses[1]), lses[2])
        e = [jnp.exp(l - m) for l in lses]
        acc = e[0] * outs[0] + e[1] * outs[1] + e[2] * outs[2]
        o_ref[bi] = acc / (e[0] + e[1] + e[2])
        return carry

    lax.fori_loop(0, SAMPLE_BB, one, 0)


def _attn_sample(q32, k32, v32, caches, biases):
    heads3 = lambda a: a[N_PROMPT:].reshape(DEC_BATCH, N_GROUPS * HEADS_PER_GROUP, HEAD_DIM)
    new = pl.BlockSpec((SAMPLE_BB, N_GROUPS * HEADS_PER_GROUP, HEAD_DIM), lambda s: (s, 0, 0))
    bc = jnp.stack([b[::-1][:N_KEYS - 1] for b in biases])[..., None]
    b0 = jnp.stack([b[0] for b in biases])[..., None]
    cache_specs, cache_args = [], []
    for which in range(2):
        for g, (window, d) in enumerate(ATTN_GROUPS):
            c = caches[g][which]
            cache_args.append(c.reshape(DEC_BATCH, c.shape[1] // d, d, HEADS_PER_GROUP, HEAD_DIM))
            cache_specs.append(pl.BlockSpec((SAMPLE_BB, N_KEYS - 1, None, HEADS_PER_GROUP, HEAD_DIM),
                                            lambda s: (s, 0, 0, 0, 0)))
    o = pl.pallas_call(
        _attn_s_kernel,
        grid=(DEC_BATCH // SAMPLE_BB,),
        in_specs=[
            new, new, new, *cache_specs,
            pl.BlockSpec((N_GROUPS, N_KEYS - 1, HEADS_PER_GROUP, 1), lambda s: (0, 0, 0, 0)),
            pl.BlockSpec((N_GROUPS, HEADS_PER_GROUP, 1), lambda s: (0, 0, 0)),
        ],
        out_specs=pl.BlockSpec((SAMPLE_BB, HEADS_PER_GROUP, HEAD_DIM), lambda s: (s, 0, 0)),
        out_shape=jax.ShapeDtypeStruct((DEC_BATCH, HEADS_PER_GROUP, HEAD_DIM), F32),
        compiler_params=_cparams("arbitrary"),
        name="attn_s",
    )(heads3(q32), heads3(k32), heads3(v32), *cache_args, bc, b0)
    return o.reshape(DEC_BATCH, GROUP_COLS).astype(BF16)


SCAN_CHUNK = 64


def _softplus(x):
    return jnp.maximum(x, 0.0) + jnp.log1p(jnp.exp(-jnp.abs(x)))


def _gates(conv, wr_ref, br_ref, wi_ref, bi_ref, l_ref):
    cb = conv.astype(BF16)
    r = _sigmoid(jnp.dot(cb, wr_ref[0].astype(BF16), preferred_element_type=F32) + br_ref[...])
    ig = _sigmoid(jnp.dot(cb, wi_ref[0].astype(BF16), preferred_element_type=F32) + bi_ref[...])
    log_a = -LRU_C * r * _softplus(-l_ref[...])
    return log_a, ig


def _decay(log_a):
    a = jnp.exp(log_a)
    return a, jnp.sqrt(-jnp.tanh(log_a) * (a * a + 1.0))


def _rec_p_kernel(xb_ref, gb_ref, cw_ref, cb_ref, wr_ref, br_ref, wi_ref, bi_ref, l_ref,
                  y_ref, h_ref, a_scr, b_scr):
    t = SEQ
    x = xb_ref[...]
    row = lax.broadcasted_iota(jnp.int32, (t, LRU_BLOCK), 0)
    conv = cb_ref[...] + x * cw_ref[CONV_WIDTH - 1:CONV_WIDTH, :]
    for s in range(1, CONV_WIDTH):
        xs = jnp.where(row >= s, pltpu.roll(x, s, 0), 0.0)
        conv = conv + xs * cw_ref[CONV_WIDTH - 1 - s:CONV_WIDTH - s, :]
    log_a, ig = _gates(conv, wr_ref, br_ref, wi_ref, bi_ref, l_ref)
    a, mult = _decay(log_a)
    start = row == 0
    a_scr[...] = jnp.where(start, 0.0, a)
    b_scr[...] = jnp.where(start, 1.0, mult) * ig * conv

    crow = lax.broadcasted_iota(jnp.int32, (SCAN_CHUNK, LRU_BLOCK), 0)

    def chunk(c, h_prev):
        rows = pl.ds(pl.multiple_of(c * SCAN_CHUNK, SCAN_CHUNK), SCAN_CHUNK)
        a_c = a_scr[rows, :]
        b_c = b_scr[rows, :]
        s = 1
        while s < SCAN_CHUNK:
            keep = crow >= s
            a_sh = jnp.where(keep, pltpu.roll(a_c, s, 0), 1.0)
            b_sh = jnp.where(keep, pltpu.roll(b_c, s, 0), 0.0)
            b_c = a_c * b_sh + b_c
            a_c = a_c * a_sh
            s *= 2
        h = a_c * h_prev + b_c
        y_ref[rows, :] = (h * _gelu(gb_ref[rows, :])).astype(BF16)
        return h[SCAN_CHUNK - 1:SCAN_CHUNK, :]

    h_last = lax.fori_loop(0, t // SCAN_CHUNK, chunk, jnp.zeros((1, LRU_BLOCK), F32))
    h_ref[0] = h_last


def _rec_prompt(xg, conv_w, conv_b, w_rg, b_rg, w_ig, b_ig, lru_l):
    nb = LRU_BLOCKS
    vec = pl.BlockSpec((1, LRU_BLOCK), lambda b, n: (0, n))
    wblk = pl.BlockSpec((1, LRU_BLOCK, LRU_BLOCK), lambda b, n: (n, 0, 0))
    y, h = pl.pallas_call(
        _rec_p_kernel,
        grid=(BATCH, nb),
        in_specs=[
            pl.BlockSpec((SEQ, LRU_BLOCK), lambda b, n: (b, n)),
            pl.BlockSpec((SEQ, LRU_BLOCK), lambda b, n: (b, nb + n)),
            pl.BlockSpec((CONV_WIDTH, LRU_BLOCK), lambda b, n: (0, n)),
            vec, wblk, vec, wblk, vec, vec,
        ],
        out_specs=[
            pl.BlockSpec((SEQ, LRU_BLOCK), lambda b, n: (b, n)),
            pl.BlockSpec((1, 1, LRU_BLOCK), lambda b, n: (b, 0, n)),
        ],
        out_shape=[
            jax.ShapeDtypeStruct((N_PROMPT, LRU_WIDTH), BF16),
            jax.ShapeDtypeStruct((BATCH, 1, LRU_WIDTH), F32),
        ],
        scratch_shapes=[pltpu.VMEM((SEQ, LRU_BLOCK), F32), pltpu.VMEM((SEQ, LRU_BLOCK), F32)],
        compiler_params=_cparams("arbitrary", "arbitrary"),
        name="rec_p",
    )(xg, xg, conv_w, conv_b, w_rg, b_rg, w_ig, b_ig, lru_l)
    return y, h.reshape(BATCH, LRU_WIDTH)


def _rec_s_kernel(xb_ref, gb_ref, c0_ref, c1_ref, c2_ref, h0_ref, cw_ref, cb_ref,
                  wr_ref, br_ref, wi_ref, bi_ref, l_ref, y_ref, h_ref):
    conv = (cb_ref[...] + c0_ref[...] * cw_ref[0:1, :] + c1_ref[...] * cw_ref[1:2, :]
            + c2_ref[...] * cw_ref[2:3, :] + xb_ref[...] * cw_ref[3:4, :])
    log_a, ig = _gates(conv, wr_ref, br_ref, wi_ref, bi_ref, l_ref)
    a, mult = _decay(log_a)
    h = a * h0_ref[...] + mult * ig * conv
    h_ref[...] = h
    y_ref[...] = (h * _gelu(gb_ref[...])).astype(BF16)


def _rec_sample(xg, state_conv, h0, conv_w, conv_b, w_rg, b_rg, w_ig, b_ig, lru_l):
    nb = LRU_BLOCKS
    sc = state_conv.reshape(DEC_BATCH, (CONV_WIDTH - 1) * LRU_WIDTH)
    srow = N_PROMPT // DEC_BATCH
    vec = pl.BlockSpec((1, LRU_BLOCK), lambda n: (0, n))
    wblk = pl.BlockSpec((1, LRU_BLOCK, LRU_BLOCK), lambda n: (n, 0, 0))
    tile = lambda f: pl.BlockSpec((DEC_BATCH, LRU_BLOCK), f)
    return pl.pallas_call(
        _rec_s_kernel,
        grid=(nb,),
        in_specs=[
            tile(lambda n: (srow, n)),
            tile(lambda n: (srow, nb + n)),
            tile(lambda n: (0, n)),
            tile(lambda n: (0, nb + n)),
            tile(lambda n: (0, 2 * nb + n)),
            tile(lambda n: (0, n)),
            pl.BlockSpec((CONV_WIDTH, LRU_BLOCK), lambda n: (0, n)),
            vec, wblk, vec, wblk, vec, vec,
        ],
        out_specs=[tile(lambda n: (0, n)), tile(lambda n: (0, n))],
        out_shape=[
            jax.ShapeDtypeStruct((DEC_BATCH, LRU_WIDTH), BF16),
            jax.ShapeDtypeStruct((DEC_BATCH, LRU_WIDTH), F32),
        ],
        compiler_params=_cparams("arbitrary"),
        name="rec_s",
    )(xg, xg, sc, sc, sc, h0, conv_w, conv_b, w_rg, b_rg, w_ig, b_ig, lru_l)


def _mix_kernel(at_ref, rc_ref, ga_ref, gr_ref, wa_ref, wb_ref, o_ref, wa_bf, wb_bf):
    @pl.when(pl.program_id(1) == 0)
    def _():
        wa_bf[...] = wa_ref[...].astype(BF16)
        wb_bf[...] = wb_ref[...].astype(BF16)

    pa = jnp.dot(at_ref[...], wa_bf[...], preferred_element_type=F32)
    pb = jnp.dot(rc_ref[...], wb_bf[...], preferred_element_type=F32)
    o_ref[...] = (ga_ref[...].astype(F32) * pa + gr_ref[...].astype(F32) * pb).astype(BF16)


def _mix(attn, rec, gates, w_pa, w_pb):
    nj = D_MODEL // MM_TN
    return pl.pallas_call(
        _mix_kernel,
        grid=(nj, N_TOK // MM_TM),
        in_specs=[
            pl.BlockSpec((MM_TM, GROUP_COLS), lambda j, i: (i, 0)),
            pl.BlockSpec((MM_TM, LRU_WIDTH), lambda j, i: (i, 0)),
            pl.BlockSpec((MM_TM, MM_TN), lambda j, i: (i, j)),
            pl.BlockSpec((MM_TM, MM_TN), lambda j, i: (i, nj + j)),
            pl.BlockSpec((GROUP_COLS, MM_TN), lambda j, i: (0, j)),
            pl.BlockSpec((LRU_WIDTH, MM_TN), lambda j, i: (0, j)),
        ],
        out_specs=pl.BlockSpec((MM_TM, MM_TN), lambda j, i: (i, j)),
        out_shape=jax.ShapeDtypeStruct((N_TOK, D_MODEL), BF16),
        scratch_shapes=[pltpu.VMEM((GROUP_COLS, MM_TN), BF16), pltpu.VMEM((LRU_WIDTH, MM_TN), BF16)],
        compiler_params=_cparams("arbitrary", "arbitrary"),
        name="mix",
    )(attn, rec, gates, gates, w_pa, w_pb)


CACHE_ROWS = 128


def _wo_caches_kernel(*refs):
    m_ref, w_ref, x_ref = refs[0:3]
    src = refs[3:9]
    o_ref = refs[9]
    dst = refs[10:16]
    wbf_ref = refs[16]

    @pl.when(pl.program_id(1) == 0)
    def _():
        wbf_ref[...] = w_ref[...].astype(BF16)

    o_ref[...] = x_ref[...] + jnp.dot(m_ref[...], wbf_ref[...], preferred_element_type=F32)
    for s, d in zip(src, dst):
        d[...] = s[...]


def _wo_and_caches(merged, w_o, x_all, k_heads, v_heads):
    nm = N_TOK // MM_TM
    grid = (D_MODEL // MM_TN, nm)
    blk = (None, CACHE_ROWS, HEADS_PER_GROUP, HEAD_DIM)
    per_batch = SEQ // CACHE_ROWS
    in_specs, out_specs, out_shape, args = [], [], [], []
    for g, (window, _) in enumerate(ATTN_GROUPS):
        keep = min(window, SEQ) // CACHE_ROWS
        n_slabs = BATCH * keep
        assert grid[0] * grid[1] >= n_slabs

        def slab(j, i, n_slabs=n_slabs):
            return jnp.minimum(j * nm + i, n_slabs - 1)

        def src_idx(j, i, g=g, keep=keep, slab=slab):
            s = slab(j, i)
            return (g, (s // keep) * per_batch + (per_batch - keep) + s % keep, 0, 0)

        for a in (k_heads, v_heads):
            args.append(a)
            in_specs.append(pl.BlockSpec(blk, src_idx))
            out_specs.append(pl.BlockSpec(blk[1:], lambda j, i, slab=slab: (slab(j, i), 0, 0)))
            out_shape.append(jax.ShapeDtypeStruct((n_slabs * CACHE_ROWS, HEADS_PER_GROUP, HEAD_DIM), F32))
    tile = pl.BlockSpec((MM_TM, MM_TN), lambda j, i: (i, j))
    outs = pl.pallas_call(
        _wo_caches_kernel,
        grid=grid,
        in_specs=[
            pl.BlockSpec((MM_TM, D_MODEL), lambda j, i: (i, 0)),
            pl.BlockSpec((D_MODEL, MM_TN), lambda j, i: (0, j)),
            tile, *in_specs,
        ],
        out_specs=[tile, *out_specs],
        out_shape=[jax.ShapeDtypeStruct((N_TOK, D_MODEL), F32), *out_shape],
        scratch_shapes=[pltpu.VMEM((D_MODEL, MM_TN), BF16)],
        compiler_params=_cparams("arbitrary", "arbitrary"),
        name="wo_caches",
    )(merged, w_o, x_all, *args)
    return outs[0], outs[1:]


def _norm2_kernel(x_ref, g_ref, xn_ref, xnt_ref):
    x = x_ref[...]
    ms = jnp.mean(x * x, axis=-1, keepdims=True)
    y = x * lax.rsqrt(ms + EPS) * g_ref[...]
    xn_ref[...] = y.astype(BF16)
    xnt_ref[...] = y.T.astype(BF16)


def _norm2(x1, g):
    return pl.pallas_call(
        _norm2_kernel,
        grid=(N_ROW_BLK,),
        in_specs=[
            pl.BlockSpec((ROW_BLK, D_MODEL), lambda i: (i, 0)),
            pl.BlockSpec((1, D_MODEL), lambda i: (0, 0)),
        ],
        out_specs=[
            pl.BlockSpec((ROW_BLK, D_MODEL), lambda i: (i, 0)),
            pl.BlockSpec((D_MODEL, ROW_BLK), lambda i: (0, i)),
        ],
        out_shape=[
            jax.ShapeDtypeStruct((N_TOK, D_MODEL), BF16),
            jax.ShapeDtypeStruct((D_MODEL, N_TOK), BF16),
        ],
        compiler_params=_cparams("arbitrary"),
        name="norm2",
    )(x1, g)


NO_RANK = float(PEER_KEYS - 1)


def _top_values(x, k, with_rank=False):
    rows = lax.broadcasted_iota(jnp.int32, x.shape, 0).astype(F32)
    rank = jnp.full(x.shape, NO_RANK, F32)
    out = []
    for j in range(k):
        m = jnp.max(x, axis=0, keepdims=True)
        first = jnp.min(jnp.where(x == m, rows, float(x.shape[0])), axis=0, keepdims=True)
        out.append(m)
        hit = rows == first
        x = jnp.where(hit, -jnp.inf, x)
        if with_rank:
            rank = jnp.where(hit, float(j), rank)
    return (out, rank) if with_rank else out


def _peer_prep_kernel(x_ref, wq_ref, k1_ref, k2_ref, e1_ref, n_ref, rank_ref, e2_ref, wq_bf, s_scr):
    step = pl.program_id(0)
    slot = step % 2

    @pl.when(step == 0)
    def _():
        s_scr[1] = jnp.zeros(s_scr.shape[1:], F32)

    @pl.when(step % PREP_TILES == 0)
    def _():
        wq_bf[...] = wq_ref[...].astype(BF16)

    half = PEER_DKEY // 2
    q = jnp.dot(x_ref[...], wq_bf[...], preferred_element_type=F32).astype(BF16)
    nt = (((1,), (1,)), ((), ()))
    s_scr[slot, 0] = lax.dot_general(k1_ref[0].astype(BF16), q[:, :half], nt, preferred_element_type=F32)
    s_scr[slot, 1] = lax.dot_general(k2_ref[0].astype(BF16), q[:, half:], nt, preferred_element_type=F32)
    s1 = s_scr[1 - slot, 0]
    s2 = s_scr[1 - slot, 1]
    for c in range(PREP_TM // 128):
        cl = slice(c * 128, (c + 1) * 128)
        x1, x2 = s1[:, cl], s2[:, cl]
        a = _top_values(x1, PEER_TOPK)
        b, rank2 = _top_values(x2, PEER_TOPK, with_rank=True)
        cand = [a[i] + b[j] for i in range(PEER_TOPK) for j in range(PEER_TOPK) if (i + 1) * (j + 1) <= PEER_TOPK]
        cand += [jnp.full_like(a[0], -jnp.inf)] * (-len(cand) % 8)
        top = _top_values(jnp.concatenate(cand, axis=0), PEER_TOPK)
        tau = top[PEER_TOPK - 1]
        z = jnp.ones_like(top[0])
        for v in top[1:]:
            z = z + jnp.exp(v - top[0])
        n = jnp.zeros_like(x1)
        for k, bk in enumerate(b):
            n = jnp.where(x1 + bk >= tau, float(k + 1), n)
        e1_ref[0, :, cl] = jnp.exp(x1 - a[0]) / z
        n_ref[0, :, cl] = n
        rank_ref[0, :, cl] = rank2.astype(BF16)
        e2_ref[0, :, cl] = jnp.exp(x2 - b[0]).astype(BF16)


PREP_TM = 640
PREP_TILES = N_TOK // PREP_TM


def _peer_prep(xn2, w_query, k1, k2):
    tm = PREP_TM
    half = PEER_DKEY // 2
    n_jobs = PEER_HEADS * PREP_TILES
    cur = lambda s: jnp.minimum(s, n_jobs - 1)
    prev = lambda s: jnp.maximum(s - 1, 0)
    big = pl.BlockSpec((1, PEER_KEYS, tm), lambda s: (prev(s) // PREP_TILES, 0, prev(s) % PREP_TILES))
    keys = pl.BlockSpec((1, PEER_KEYS, half), lambda s: (cur(s) // PREP_TILES, 0, 0))
    shp = lambda dt: jax.ShapeDtypeStruct((PEER_HEADS, PEER_KEYS, N_TOK), dt)
    return pl.pallas_call(
        _peer_prep_kernel,
        grid=(n_jobs + 1,),
        in_specs=[
            pl.BlockSpec((tm, D_MODEL), lambda s: (cur(s) % PREP_TILES, 0)),
            pl.BlockSpec((D_MODEL, PEER_DKEY), lambda s: (0, cur(s) // PREP_TILES)),
            keys, keys,
        ],
        out_specs=[big, big, big, big],
        out_shape=[shp(F32), shp(F32), shp(BF16), shp(BF16)],
        scratch_shapes=[pltpu.VMEM((D_MODEL, PEER_DKEY), BF16), pltpu.VMEM((2, 2, PEER_KEYS, PREP_TM), F32)],
        compiler_params=_cparams("arbitrary"),
        name="peer_prep",
    )(xn2, w_query, k1, k2)


I1_PER_TILE = PEER_TE // PEER_KEYS
I1_BLK = 8


def _peer_kernel(xt_ref, u_ref, vt_ref, e1_ref, n_ref, rank_ref, e2_ref, o_ref, w_scr, acc_ref):
    e = pl.program_id(1)

    @pl.when(e == 0)
    def _():
        acc_ref[...] = jnp.zeros_like(acc_ref)

    st = jnp.dot(u_ref[...], xt_ref[...], preferred_element_type=F32)
    off = (e % (I1_BLK // I1_PER_TILE)) * I1_PER_TILE
    zero = jnp.zeros((), BF16)
    for r in range(I1_PER_TILE):
        rows = slice(r * PEER_KEYS, (r + 1) * PEER_KEYS)
        coef = jnp.zeros((PEER_KEYS, PEER_TM), BF16)
        for h in range(PEER_HEADS):
            n = n_ref[h, pl.ds(off + r, 1), :].astype(BF16)
            g1 = e1_ref[h, pl.ds(off + r, 1), :].astype(BF16)
            coef = coef + g1 * jnp.where(rank_ref[h] < n, e2_ref[h], zero)
        w_scr[rows, :] = coef * _gelu(st[rows, :]).astype(BF16)
    acc_ref[...] += jnp.dot(vt_ref[...], w_scr[...], preferred_element_type=F32)

    @pl.when(e == pl.num_programs(1) - 1)
    def _():
        o_ref[...] = acc_ref[...].astype(BF16)


def _peer(xn2t, u_bf, vt_bf, e1, n, rank2, e2):
    nt = pl.cdiv(N_TOK, PEER_TM)
    ne = PEER_EXPERTS // PEER_TE
    per = I1_BLK // I1_PER_TILE
    small = pl.BlockSpec((PEER_HEADS, I1_BLK, PEER_TM), lambda t, e: (0, e // per, t))
    big = pl.BlockSpec((PEER_HEADS, PEER_KEYS, PEER_TM), lambda t, e: (0, 0, t))
    return pl.pallas_call(
        _peer_kernel,
        grid=(nt, ne),
        in_specs=[
            pl.BlockSpec((D_MODEL, PEER_TM), lambda t, e: (0, t)),
            pl.BlockSpec((PEER_TE, D_MODEL), lambda t, e: (e, 0)),
            pl.BlockSpec((D_MODEL, PEER_TE), lambda t, e: (0, e)),
            small, small, big, big,
        ],
        out_specs=pl.BlockSpec((D_MODEL, PEER_TM), lambda t, e: (0, t)),
        out_shape=jax.ShapeDtypeStruct((D_MODEL, N_TOK), BF16),
        scratch_shapes=[pltpu.VMEM((PEER_TE, PEER_TM), BF16), pltpu.VMEM((D_MODEL, PEER_TM), F32)],
        compiler_params=_cparams("arbitrary", "arbitrary"),
        name="peer",
    )(xn2t, u_bf, vt_bf, e1, n, rank2, e2)


def _final_kernel(x_ref, pt_ref, yp_ref, ys_ref):
    i = pl.program_id(0)
    y = x_ref[...] + pt_ref[...].astype(F32).T

    @pl.when(i < N_PROMPT_BLK)
    def _():
        yp_ref[...] = y

    @pl.when(i == N_PROMPT_BLK)
    def _():
        ys_ref[...] = y


def _final(x1, peer_t):
    return pl.pallas_call(
        _final_kernel,
        grid=(N_ROW_BLK,),
        in_specs=[
            pl.BlockSpec((ROW_BLK, D_MODEL), lambda i: (i, 0)),
            pl.BlockSpec((D_MODEL, ROW_BLK), lambda i: (0, i)),
        ],
        out_specs=[
            pl.BlockSpec((ROW_BLK, D_MODEL), lambda i: (jnp.minimum(i, N_PROMPT_BLK - 1), 0)),
            pl.BlockSpec((ROW_BLK, D_MODEL), lambda i: (0, 0)),
        ],
        out_shape=[
            jax.ShapeDtypeStruct((N_PROMPT, D_MODEL), F32),
            jax.ShapeDtypeStruct((DEC_BATCH, D_MODEL), F32),
        ],
        compiler_params=_cparams("arbitrary"),
        name="final",
    )(x1, peer_t)


def _bucket(dist):
    max_exact = N_BUCKETS // 2
    dd = np.maximum(dist, max_exact).astype(np.float32)
    large = max_exact + (np.log(dd / np.float32(max_exact)) / np.float32(math.log(RP_MAX_DIST / max_exact))
                         * np.float32(N_BUCKETS - max_exact)).astype(np.int32)
    large = np.minimum(large, N_BUCKETS - 1)
    return np.where(dist < max_exact, dist, large)


def _group_bias(rel_bias, g, d):
    onehot = np.eye(N_BUCKETS, dtype=np.float32)[_bucket(d * np.arange(N_KEYS))]
    bias = jnp.dot(onehot, rel_bias[:, g * HEADS_PER_GROUP:(g + 1) * HEADS_PER_GROUP],
                   precision=lax.Precision.HIGHEST)
    pad = jnp.full((HEADS_PER_GROUP, 2 * ROW_BLK - N_KEYS), NEG, F32)
    ext = jnp.concatenate([bias[::-1].T, pad], axis=1)
    return bias, ext[:, None, :]


def kernel(x_prompt, x_sample, cache_k_w128, cache_v_w128, cache_k_w512, cache_v_w512, cache_k_w2048,
           cache_v_w2048, state_lru_h, state_conv, rel_bias, ln1_g, w_in, q_norm_g, k_norm_g, conv_w,
           conv_b, w_rg, b_rg, w_ig, b_ig, lru_L, w_pa, w_pb, w_o, ln2_g, w_query, sub_keys_1,
           sub_keys_2, peer_u, peer_v):
    caches = ((cache_k_w128, cache_v_w128), (cache_k_w512, cache_v_w512), (cache_k_w2048, cache_v_w2048))
    xp = x_prompt.reshape(N_PROMPT, D_MODEL)
    xs = x_sample.reshape(DEC_BATCH, D_MODEL)
    w = w_in[0]

    xn, x_all = _norm1(xp, xs, ln1_g)
    gspec = [pl.BlockSpec((1, HEAD_DIM), lambda j, i: (0, 0))]
    wide = dict(tm=IN_TM, tn=IN_TN)
    (q32,) = _proj(xn, w, Q_OFF, ATTN_WIDTH, _epi_q, (F32,), (q_norm_g,), gspec, **wide)
    k32, k_heads = _proj(xn, w, K_OFF, ATTN_WIDTH, _epi_k, (F32,), (k_norm_g,), gspec, head_major=True, **wide)
    v32, v_heads = _proj(xn, w, V_OFF, ATTN_WIDTH, _epi_v, (F32,), head_major=True, **wide)
    (xg,) = _proj(xn, w, XB_OFF, 2 * LRU_WIDTH, _epi_f32, (F32,), **wide)
    gates, u_bf, vt_bf = _gates_and_tables(xn, w, peer_u[0], peer_v[0])

    biases, exts = zip(*[_group_bias(rel_bias, g, d) for g, (_, d) in enumerate(ATTN_GROUPS)])
    attn_p = _attn_prompt(q32, k32, v32, jnp.concatenate(exts, axis=0))
    attn_s = _attn_sample(q32, k32, v32, [(ck[0], cv[0]) for ck, cv in caches], biases)
    attn = jnp.concatenate([attn_p, attn_s], axis=0)

    rec_p, h_p = _rec_prompt(xg, conv_w[0], conv_b, w_rg[0], b_rg, w_ig[0], b_ig, lru_L)
    rec_s, h_s = _rec_sample(xg, state_conv[0], state_lru_h[0], conv_w[0], conv_b, w_rg[0], b_rg,
                             w_ig[0], b_ig, lru_L)
    rec = jnp.concatenate([rec_p, rec_s], axis=0)

    merged = _mix(attn, rec, gates, w_pa[0], w_pb[0])
    x1, prompt_caches = _wo_and_caches(merged, w_o[0], x_all, k_heads, v_heads)

    xn2, xn2t = _norm2(x1, ln2_g)
    e1, nsel, rank2, e2 = _peer_prep(xn2, w_query[0], sub_keys_1[0], sub_keys_2[0])
    peer_t = _peer(xn2t, u_bf, vt_bf, e1, nsel, rank2, e2)
    y_p, y_s = _final(x1, peer_t)

    outs = [y_p.reshape(BATCH, SEQ, D_MODEL), y_s.reshape(DEC_BATCH, 1, D_MODEL)]
    for g, (window, d) in enumerate(ATTN_GROUPS):
        keep = min(window, SEQ)
        for a in prompt_caches[2 * g:2 * g + 2]:
            outs.append(a.reshape(1, BATCH, keep, HEADS_PER_GROUP, HEAD_DIM))
        for a in (k_heads, v_heads):
            outs.append(a[g, N_PROMPT:].reshape(1, DEC_BATCH, 1, HEADS_PER_GROUP, HEAD_DIM))
    tail = CONV_WIDTH - 1
    xb_tail = jnp.stack([xg[(b + 1) * SEQ - tail:(b + 1) * SEQ, :LRU_WIDTH] for b in range(BATCH)])
    xb_s = xg[N_PROMPT:, :LRU_WIDTH]
    outs += [
        h_p[None], h_s[None],
        xb_tail[None],
        jnp.concatenate([state_conv[0][:, 1:], xb_s[:, None, :]], axis=1)[None],
    ]
    return tuple(outs)
```
